```python
import math
import jax, jax.numpy as jnp
from jax import lax
import numpy as np

D_MODEL = 1024
BATCH = 4
SEQ = 4096
DEPTH = 1
DEC_BATCH = 128
DEC_SEQ = 1
PAST_LEN = 8192
PAGE_SIZE = 128

D_MIX = D_MODEL
ATTN_WIDTH = D_MIX // 2
SSM_WIDTH = D_MIX - ATTN_WIDTH
HEAD_DIM = 64
N_HEADS = ATTN_WIDTH // HEAD_DIM
N_KV_HEADS = 2
KV_REP = N_HEADS // N_KV_HEADS
WINDOW = 128
ATTN_SCALE = HEAD_DIM ** -0.5
N_BUCKETS = 32
MAX_DISTANCE = 128
SSM_HEAD_DIM = 64
SSM_HEADS = SSM_WIDTH // SSM_HEAD_DIM
SSM_GROUPS = 2
SSM_REP = SSM_HEADS // SSM_GROUPS
D_STATE = 128
CONV_W = 4
CONV_DIM = SSM_WIDTH + 2 * SSM_GROUPS * D_STATE
SSD_CHUNK = 128
D_FF = -(-(8 * D_MODEL) // (3 * 256)) * 256
Q_COLS = N_HEADS * HEAD_DIM
KV_COLS = N_KV_HEADS * HEAD_DIM
SPLIT_IDX = (Q_COLS, Q_COLS + KV_COLS, Q_COLS + 2 * KV_COLS, Q_COLS + 2 * KV_COLS + SSM_WIDTH, Q_COLS + 2 * KV_COLS + SSM_WIDTH + CONV_DIM)
D_IN_PROJ = Q_COLS + 2 * KV_COLS + SSM_WIDTH + CONV_DIM + SSM_HEADS
EPS = 1e-6

kernel_name = "hymba_swa_sink_ssd_decode_step"


def rms_norm(x, w):
    xf = x.astype(jnp.float32)
    xf = xf * lax.rsqrt(jnp.mean(xf * xf, axis=-1, keepdims=True) + EPS)
    return (xf * w.astype(jnp.float32)).astype(x.dtype)


def rel_pos_bias(dist, rel_bias):
    n = jnp.maximum(dist, 0)
    exact = N_BUCKETS // 2
    nf = jnp.maximum(n, 1).astype(jnp.float32)
    large = exact + (jnp.log(nf / exact) / math.log(MAX_DISTANCE / exact) * (N_BUCKETS - exact)).astype(jnp.int32)
    bucket = jnp.where(n < exact, n, jnp.minimum(large, N_BUCKETS - 1))
    b = rel_bias[bucket].astype(jnp.float32)
    return jnp.transpose(b, (2, 0, 1)).reshape(N_KV_HEADS, KV_REP, dist.shape[0], dist.shape[1])


def sink_softmax(s, sinks, mask):
    s = jnp.where(mask, s, -jnp.inf)
    sk = sinks.astype(jnp.float32).reshape(N_KV_HEADS, KV_REP, 1, 1)
    m = jnp.maximum(jnp.max(s, axis=-1, keepdims=True), sk)
    e = jnp.exp(s - m)
    return e / (jnp.sum(e, axis=-1, keepdims=True) + jnp.exp(sk - m))


def swa_prompt(q, k, v, sinks, rel_bias):
    b, L = q.shape[0], q.shape[1]
    nb = L // WINDOW
    qb = q.reshape(b, nb, WINDOW, N_KV_HEADS, KV_REP, HEAD_DIM)
    kb = k.reshape(b, nb, WINDOW, N_KV_HEADS, HEAD_DIM)
    vb = v.reshape(b, nb, WINDOW, N_KV_HEADS, HEAD_DIM)
    pad = ((0, 0), (1, 0), (0, 0), (0, 0), (0, 0))
    kc = jnp.concatenate([jnp.pad(kb[:, :-1], pad), kb], axis=2)
    vc = jnp.concatenate([jnp.pad(vb[:, :-1], pad), vb], axis=2)
    qi = jnp.arange(WINDOW)[:, None]
    kj = jnp.arange(2 * WINDOW)[None, :]
    dist = qi + WINDOW - kj
    band = (dist >= 0) & (dist < WINDOW)
    has_prev = (jnp.arange(nb) > 0)[:, None, None] | (kj >= WINDOW)[None]
    mask = (band[None] & has_prev)[None, :, None, None]
    s = jnp.einsum('bnqhrd,bnkhd->bnhrqk', qb, kc).astype(jnp.float32) * ATTN_SCALE + rel_pos_bias(dist, rel_bias)
    p = sink_softmax(s, sinks, mask)
    o = jnp.einsum('bnhrqk,bnkhd->bnqhrd', p.astype(vc.dtype), vc)
    return o.reshape(b, L, ATTN_WIDTH)


def swa_sample(q, k, v, k_buf, v_buf, sinks, rel_bias):
    b, S = q.shape[0], q.shape[1]
    kc = jnp.concatenate([k_buf.astype(k.dtype), k], axis=1)
    vc = jnp.concatenate([v_buf.astype(v.dtype), v], axis=1)
    dist = jnp.arange(S)[:, None] + WINDOW - jnp.arange(WINDOW + S)[None, :]
    mask = (dist >= 0) & (dist < WINDOW)
    s = jnp.einsum('bqhrd,bkhd->bhrqk', q, kc).astype(jnp.float32) * ATTN_SCALE + rel_pos_bias(dist, rel_bias)
    p = sink_softmax(s, sinks, mask)
    o = jnp.einsum('bhrqk,bkhd->bqhrd', p.astype(vc.dtype), vc)
    return o.reshape(b, S, ATTN_WIDTH), kc[:, -WINDOW:], vc[:, -WINDOW:]


def causal_conv(xbc, conv_buf, w, bias):
    L = xbc.shape[1]
    xp = jnp.concatenate([conv_buf.astype(xbc.dtype), xbc], axis=1)
    out = bias + sum(xp[:, j:j + L] * w[j] for j in range(CONV_W))
    return jax.nn.silu(out), xp[:, -(CONV_W - 1):]


def segsum_exp(cs):
    q = cs.shape[-1]
    causal = jnp.tril(jnp.ones((q, q), dtype=bool))
    return jnp.exp(jnp.where(causal, cs[..., :, None] - cs[..., None, :], -jnp.inf))


def ssd(x, dt, A, B, C, h0):
    b, L = x.shape[0], x.shape[1]
    q = min(SSD_CHUNK, L)
    nc = -(-L // q)
    pad = nc * q - L
    if pad:
        padf = lambda t: jnp.pad(t, [(0, 0), (0, pad)] + [(0, 0)] * (t.ndim - 2))
        x, dt, B, C = padf(x), padf(dt), padf(B), padf(C)
    x = x.reshape(b, nc, q, SSM_GROUPS, SSM_REP, SSM_HEAD_DIM)
    dt = dt.reshape(b, nc, q, SSM_GROUPS, SSM_REP)
    B = B.reshape(b, nc, q, SSM_GROUPS, D_STATE)
    C = C.reshape(b, nc, q, SSM_GROUPS, D_STATE)
    cs = jnp.cumsum(jnp.moveaxis(dt * A.reshape(SSM_GROUPS, SSM_REP), 2, -1), axis=-1)
    xdt = x * dt[..., None]
    CB = jnp.einsum('bcign,bcjgn->bcgij', C, B)
    M = CB[:, :, :, None] * segsum_exp(cs)
    y_diag = jnp.einsum('bcgrij,bcjgrp->bcigrp', M, xdt)
    decay_to_end = jnp.exp(cs[..., -1:] - cs)
    chunk_states = jnp.einsum('bcjgn,bcgrj,bcjgrp->bcgrpn', B, decay_to_end, xdt)
    chunk_decay = jnp.exp(cs[..., -1])

    def step(h, inp):
        st, dec = inp
        return h * dec[..., None, None] + st, h

    h_final, h_prev = lax.scan(step, h0.reshape(b, SSM_GROUPS, SSM_REP, SSM_HEAD_DIM, D_STATE),
                               (jnp.moveaxis(chunk_states, 1, 0), jnp.moveaxis(chunk_decay, 1, 0)))
    h_prev = jnp.moveaxis(h_prev, 0, 1)
    y_off = jnp.einsum('bcign,bcgri,bcgrpn->bcigrp', C, jnp.exp(cs), h_prev)
    y = (y_diag + y_off).reshape(b, nc * q, SSM_HEADS, SSM_HEAD_DIM)[:, :L]
    return y, h_final.reshape(b, SSM_HEADS, SSM_HEAD_DIM, D_STATE)


def ssm_mixer(z, xbc, dt_raw, conv_buf, h0, conv_w, conv_b, dt_bias, A_log, D_skip, norm_w):
    b, L = z.shape[0], z.shape[1]
    xbc_c, new_buf = causal_conv(xbc, conv_buf, conv_w, conv_b)
    xs, Bm, Cm = jnp.split(xbc_c.astype(jnp.float32), (SSM_WIDTH, SSM_WIDTH + SSM_GROUPS * D_STATE), axis=-1)
    xs = xs.reshape(b, L, SSM_HEADS, SSM_HEAD_DIM)
    Bm = Bm.reshape(b, L, SSM_GROUPS, D_STATE)
    Cm = Cm.reshape(b, L, SSM_GROUPS, D_STATE)
    dt = jax.nn.softplus(dt_raw.astype(jnp.float32) + dt_bias.astype(jnp.float32))
    A = -jnp.exp(A_log.astype(jnp.float32))
    y, h = ssd(xs, dt, A, Bm, Cm, h0.astype(jnp.float32))
    y = y + xs * D_skip.astype(jnp.float32)[:, None]
    y = y.reshape(b, L, SSM_WIDTH) * jax.nn.silu(z.astype(jnp.float32))
    yg = y.reshape(b, L, SSM_GROUPS, SSM_WIDTH // SSM_GROUPS)
    yg = yg * lax.rsqrt(jnp.mean(yg * yg, axis=-1, keepdims=True) + EPS)
    y = yg.reshape(b, L, SSM_WIDTH) * norm_w.astype(jnp.float32)
    return y.astype(z.dtype), new_buf, h


def hybrid_layer(x, k_buf, v_buf, conv_buf, h0, rel_bias, norm1_w, w_in, attn_sinks, conv_w, conv_b,
                 dt_bias, A_log, D_skip, ssm_norm_w, w_out, norm2_w, w_gate, w_up, w_down):
    b, L = x.shape[0], x.shape[1]
    h = rms_norm(x, norm1_w)
    proj = h @ w_in
    q, k, v, z, xbc, dt_raw = jnp.split(proj, SPLIT_IDX, axis=-1)
    q = q.reshape(b, L, N_KV_HEADS, KV_REP, HEAD_DIM)
    k = k.reshape(b, L, N_KV_HEADS, HEAD_DIM)
    v = v.reshape(b, L, N_KV_HEADS, HEAD_DIM)
    if k_buf is None:
        a = swa_prompt(q, k, v, attn_sinks, rel_bias)
        new_k, new_v = k[:, -WINDOW:], v[:, -WINDOW:]
    else:
        a, new_k, new_v = swa_sample(q, k, v, k_buf, v_buf, attn_sinks, rel_bias)
    s_out, new_conv, new_h = ssm_mixer(z, xbc, dt_raw, conv_buf, h0, conv_w, conv_b, dt_bias, A_log, D_skip, ssm_norm_w)
    x = x + jnp.concatenate([a, s_out], axis=-1) @ w_out
    h2 = rms_norm(x, norm2_w)
    x = x + (jax.nn.silu(h2 @ w_gate) * (h2 @ w_up)) @ w_down
    return x, new_k, new_v, new_conv, new_h


def setup_inputs(seed: int = 0) -> dict:
    key = jax.random.key(seed)
    ks = jax.random.split(key, 24)
    f32 = jnp.float32

    def nrm(k, shape, s):
        return jax.random.normal(k, shape, f32) * s

    dt0 = jnp.exp(jax.random.uniform(ks[12], (DEPTH, SSM_HEADS), f32, math.log(1e-3), math.log(1e-1)))
    return {
        "x_prompt": nrm(ks[0], (BATCH, SEQ, D_MODEL), 1.0),
        "x_sample": nrm(ks[1], (DEC_BATCH, DEC_SEQ, D_MODEL), 1.0),
        "cache_k": nrm(ks[2], (DEPTH, DEC_BATCH, WINDOW, N_KV_HEADS, HEAD_DIM), 1.0),
        "cache_v": nrm(ks[3], (DEPTH, DEC_BATCH, WINDOW, N_KV_HEADS, HEAD_DIM), 1.0),
        "state_conv": nrm(ks[4], (DEPTH, DEC_BATCH, CONV_W - 1, CONV_DIM), 1.0),
        "state_ssm": nrm(ks[5], (DEPTH, DEC_BATCH, SSM_HEADS, SSM_HEAD_DIM, D_STATE), 0.5),
        "rel_bias": nrm(ks[6], (N_BUCKETS, N_HEADS), 0.5),
        "norm1_w": 1.0 + nrm(ks[7], (DEPTH, D_MODEL), 0.02),
        "w_in": nrm(ks[8], (DEPTH, D_MODEL, D_IN_PROJ), D_MODEL ** -0.5),
        "attn_sinks": nrm(ks[9], (DEPTH, N_HEADS), 1.0),
        "conv_w": nrm(ks[10], (DEPTH, CONV_W, CONV_DIM), CONV_W ** -0.5),
        "conv_b": nrm(ks[11], (DEPTH, CONV_DIM), 0.02),
        "dt_bias": dt0 + jnp.log(-jnp.expm1(-dt0)),
        "A_log": jnp.log(jax.random.uniform(ks[13], (DEPTH, SSM_HEADS), f32, 1.0, 16.0)),
        "D_skip": 1.0 + nrm(ks[14], (DEPTH, SSM_HEADS), 0.1),
        "ssm_norm_w": 1.0 + nrm(ks[15], (DEPTH, SSM_WIDTH), 0.02),
        "w_out": nrm(ks[16], (DEPTH, D_MIX, D_MODEL), D_MIX ** -0.5),
        "norm2_w": 1.0 + nrm(ks[17], (DEPTH, D_MODEL), 0.02),
        "w_gate": nrm(ks[18], (DEPTH, D_MODEL, D_FF), D_MODEL ** -0.5),
        "w_up": nrm(ks[19], (DEPTH, D_MODEL, D_FF), D_MODEL ** -0.5),
        "w_down": nrm(ks[20], (DEPTH, D_FF, D_MODEL), D_FF ** -0.5),
        "final_norm_w": 1.0 + nrm(ks[21], (D_MODEL,), 0.02),
    }


def reference(x_prompt, x_sample, cache_k, cache_v, state_conv, state_ssm, rel_bias, norm1_w, w_in,
              attn_sinks, conv_w, conv_b, dt_bias, A_log, D_skip, ssm_norm_w, w_out, norm2_w,
              w_gate, w_up, w_down, final_norm_w):
    yp, ys = x_prompt, x_sample
    kp, vp, cp, sp, kss, vss, css, sss = [], [], [], [], [], [], [], []
    conv0 = jnp.zeros((x_prompt.shape[0], CONV_W - 1, CONV_DIM), x_prompt.dtype)
    h00 = jnp.zeros((x_prompt.shape[0], SSM_HEADS, SSM_HEAD_DIM, D_STATE), jnp.float32)
    for l in range(DEPTH):
        wts = (rel_bias, norm1_w[l], w_in[l], attn_sinks[l], conv_w[l], conv_b[l], dt_bias[l], A_log[l],
               D_skip[l], ssm_norm_w[l], w_out[l], norm2_w[l], w_gate[l], w_up[l], w_down[l])
        yp, k1, v1, c1, s1 = hybrid_layer(yp, None, None, conv0, h00, *wts)
        ys, k2, v2, c2, s2 = hybrid_layer(ys, cache_k[l], cache_v[l], state_conv[l], state_ssm[l], *wts)
        kp.append(k1.astype(cache_k.dtype)); vp.append(v1.astype(cache_v.dtype))
        cp.append(c1.astype(state_conv.dtype)); sp.append(s1.astype(state_ssm.dtype))
        kss.append(k2.astype(cache_k.dtype)); vss.append(v2.astype(cache_v.dtype))
        css.append(c2.astype(state_conv.dtype)); sss.append(s2.astype(state_ssm.dtype))
    yp = rms_norm(yp, final_norm_w)
    ys = rms_norm(ys, final_norm_w)
    return (yp, ys, jnp.stack(kp), jnp.stack(vp), jnp.stack(cp), jnp.stack(sp),
            jnp.stack(kss), jnp.stack(vss), jnp.stack(css), jnp.stack(sss))
```

```python
import functools
import math

import numpy as np
import jax
import jax.numpy as jnp
from jax import lax
from jax.experimental import pallas as pl
from jax.experimental.pallas import tpu as pltpu

F32 = jnp.float32
BF16 = jnp.bfloat16

D_MODEL = 1024
HEAD_DIM = 64
N_HEADS = 8
N_KV_HEADS = 2
KV_REP = 4
WINDOW = 128
ATTN_WIDTH = 512
ATTN_SCALE = HEAD_DIM ** -0.5
N_BUCKETS = 32
MAX_DISTANCE = 128
SSM_WIDTH = 512
SSM_HEADS = 8
SSM_GROUPS = 2
SSM_REP = 4
SSM_HEAD_DIM = 64
D_STATE = 128
CONV_W = 4
CONV_DIM = 1024
CHUNK = 128
D_FF = 2816
EPS = 1e-6

COL_Q, COL_Z, COL_XBC, COL_K, COL_V, COL_DT = 0, 512, 1024, 2048, 2176, 2304
PROJ_COLS = 2432
DT_PAD = 128

FF_CHUNK = 256
N_FF_CHUNKS = D_FF // FF_CHUNK

VMEM_LIMIT = 56 * 1024 * 1024

_NT = (((1,), (1,)), ((), ()))


def _bucket_table(dist):
    n = np.maximum(dist, 0)
    exact = N_BUCKETS // 2
    nf = np.maximum(n, 1).astype(np.float32)
    large = exact + (np.log(nf / exact) / math.log(MAX_DISTANCE / exact) * (N_BUCKETS - exact)).astype(np.int32)
    return np.where(n < exact, n, np.minimum(large, N_BUCKETS - 1)).astype(np.int32)


def _prompt_buckets():
    dist = np.arange(WINDOW)[:, None] + WINDOW - np.arange(2 * WINDOW)[None, :]
    band = (dist >= 0) & (dist < WINDOW)
    return np.where(band, _bucket_table(dist), -1).astype(np.int32)


def _sample_buckets():
    dist = WINDOW - np.arange(WINDOW)[None, :]
    band = (dist >= 0) & (dist < WINDOW)
    row = np.where(band, _bucket_table(dist), -1).astype(np.int32)
    return np.tile(row, (N_HEADS, 1))


def _rms(x, w):
    return x * lax.rsqrt(jnp.mean(x * x, axis=-1, keepdims=True) + EPS) * w


def _silu(x):
    return x * jax.nn.sigmoid(x)


def _softplus(x):
    return jnp.maximum(x, 0.0) + jnp.log1p(jnp.exp(-jnp.abs(x)))


def _const_spec(shape):
    nd = len(shape)
    return pl.BlockSpec(shape, lambda *_: (0,) * nd, pipeline_mode=pl.Buffered(1))


def _bias_kernel(rb_ref, bp_ref, bs_ref, tp_ref, ts_ref, tn_ref):
    bp = bp_ref[...]
    bs = bs_ref[...]
    rowid = lax.broadcasted_iota(jnp.int32, (N_HEADS, WINDOW), 0)
    ts = jnp.zeros((N_HEADS, WINDOW), F32)
    tn = jnp.zeros((N_HEADS, WINDOW), F32)
    for h in range(N_HEADS):
        tp = jnp.zeros((WINDOW, 2 * WINDOW), F32)
        for bk in range(N_BUCKETS):
            v = rb_ref[bk, h]
            tp = jnp.where(bp == bk, v, tp)
            ts = jnp.where((bs == bk) & (rowid == h), v, ts)
        tp_ref[h] = jnp.where(bp < 0, -jnp.inf, tp)
        tn = jnp.where(rowid == h, rb_ref[0, h], tn)
    ts_ref[...] = jnp.where(bs < 0, -jnp.inf, ts)
    tn_ref[...] = tn


def _bias_tables(rel_bias):
    return pl.pallas_call(
        _bias_kernel,
        out_shape=(jax.ShapeDtypeStruct((N_HEADS, WINDOW, 2 * WINDOW), F32),
                   jax.ShapeDtypeStruct((N_HEADS, WINDOW), F32),
                   jax.ShapeDtypeStruct((N_HEADS, WINDOW), F32)),
        in_specs=[pl.BlockSpec(memory_space=pltpu.SMEM),
                  pl.BlockSpec(memory_space=pltpu.VMEM),
                  pl.BlockSpec(memory_space=pltpu.VMEM)],
        name="bias_tables",
    )(rel_bias, jnp.asarray(_prompt_buckets()), jnp.asarray(_sample_buckets()))


def _inproj_kernel(x_ref, nw_ref, w_ref, o_ref):
    h = _rms(x_ref[...], nw_ref[...]).astype(BF16)
    o_ref[...] = jnp.dot(h, w_ref[...], preferred_element_type=F32)


def _inproj(x2d, norm_w, w_perm, tm):
    n = x2d.shape[0]
    return pl.pallas_call(
        _inproj_kernel,
        out_shape=jax.ShapeDtypeStruct((n, PROJ_COLS), F32),
        grid=(n // tm,),
        in_specs=[pl.BlockSpec((tm, D_MODEL), lambda i: (i, 0)),
                  _const_spec((1, D_MODEL)),
                  _const_spec((D_MODEL, PROJ_COLS))],
        out_specs=pl.BlockSpec((tm, PROJ_COLS), lambda i: (i, 0)),
        compiler_params=pltpu.CompilerParams(dimension_semantics=("arbitrary",),
                                             vmem_limit_bytes=VMEM_LIMIT),
        name="inproj",
    )(x2d, norm_w, w_perm)


def _attn_prompt_kernel(sink_ref, q_ref, kp_ref, kc_ref, vp_ref, vc_ref, tab_ref, o_ref):
    n = pl.program_id(1)
    q = q_ref[...].astype(BF16)
    k = jnp.concatenate([kp_ref[...], kc_ref[...]], axis=0).astype(BF16)
    v = jnp.concatenate([vp_ref[...], vc_ref[...]], axis=0).astype(BF16)
    col = lax.broadcasted_iota(jnp.int32, (WINDOW, 2 * WINDOW), 1)
    first_valid = jnp.where(n > 0, 0, WINDOW)
    ok = col >= first_valid
    outs = []
    for hh in range(N_HEADS):
        g = hh // KV_REP
        s = lax.dot_general(q[:, hh * HEAD_DIM:(hh + 1) * HEAD_DIM], k[:, g * HEAD_DIM:(g + 1) * HEAD_DIM],
                            _NT, preferred_element_type=F32)
        s = s * ATTN_SCALE + tab_ref[hh]
        s = jnp.where(ok, s, -jnp.inf)
        sk = sink_ref[hh]
        m = jnp.maximum(jnp.max(s, axis=-1, keepdims=True), sk)
        e = jnp.exp(s - m)
        den = jnp.sum(e, axis=-1, keepdims=True) + jnp.exp(sk - m)
        p = (e / den).astype(BF16)
        outs.append(jnp.dot(p, v[:, g * HEAD_DIM:(g + 1) * HEAD_DIM], preferred_element_type=F32))
    o_ref[...] = jnp.concatenate(outs, axis=-1)


def _attn_prompt(proj, sinks, tab, batch, seq):
    nb = seq // WINDOW
    kcol, vcol = COL_K // WINDOW, COL_V // WINDOW

    def cur(c):
        return lambda b, n: (b * nb + n, c)

    def prev(c):
        return lambda b, n: (b * nb + jnp.maximum(n - 1, 0), c)

    return pl.pallas_call(
        _attn_prompt_kernel,
        out_shape=jax.ShapeDtypeStruct((batch * seq, ATTN_WIDTH), F32),
        grid=(batch, nb),
        in_specs=[pl.BlockSpec(memory_space=pltpu.SMEM),
                  pl.BlockSpec((WINDOW, ATTN_WIDTH), lambda b, n: (b * nb + n, 0)),
                  pl.BlockSpec((WINDOW, WINDOW), prev(kcol)),
                  pl.BlockSpec((WINDOW, WINDOW), cur(kcol)),
                  pl.BlockSpec((WINDOW, WINDOW), prev(vcol)),
                  pl.BlockSpec((WINDOW, WINDOW), cur(vcol)),
                  _const_spec((N_HEADS, WINDOW, 2 * WINDOW))],
        out_specs=pl.BlockSpec((WINDOW, ATTN_WIDTH), lambda b, n: (b * nb + n, 0)),
        compiler_params=pltpu.CompilerParams(dimension_semantics=("arbitrary", "arbitrary"),
                                             vmem_limit_bytes=VMEM_LIMIT),
        name="attn_prompt",
    )(sinks, proj, proj, proj, proj, proj, tab)


def _gated_group_norm(y, z, nw):
    y = y * _silu(z)
    half = SSM_WIDTH // SSM_GROUPS
    parts = []
    for g in range(SSM_GROUPS):
        yg = y[:, g * half:(g + 1) * half]
        parts.append(yg * lax.rsqrt(jnp.mean(yg * yg, axis=-1, keepdims=True) + EPS))
    return jnp.concatenate(parts, axis=-1) * nw


def _ssd_prompt_kernel(z_ref, xbc_ref, dt_ref, cw_ref, cb_ref, dtb_ref, alog_ref, dsk_ref, nw_ref,
                       s_ref, conv_ref, st_ref, xp_ref):
    c = pl.program_id(1)
    last = pl.num_programs(1) - 1

    @pl.when(c == 0)
    def _():
        xp_ref[0:8, :] = jnp.zeros((8, CONV_DIM), F32)
        st_ref[...] = jnp.zeros_like(st_ref)

    xp_ref[8:8 + CHUNK, :] = xbc_ref[...]
    tot = xp_ref[5:5 + CHUNK, :] * cw_ref[0:1, :]
    for j in range(1, CONV_W):
        tot = tot + xp_ref[5 + j:5 + j + CHUNK, :] * cw_ref[j:j + 1, :]
    xc = _silu(cb_ref[...] + tot)
    tail = xp_ref[5 + CHUNK:8 + CHUNK, :]

    @pl.when(c == last)
    def _():
        conv_ref[0] = tail

    xp_ref[5:8, :] = tail

    xs = xc[:, :SSM_WIDTH]
    bm = xc[:, SSM_WIDTH:SSM_WIDTH + SSM_GROUPS * D_STATE]
    cm = xc[:, SSM_WIDTH + SSM_GROUPS * D_STATE:]

    dt = _softplus(dt_ref[...] + dtb_ref[...])
    a = dt * (-jnp.exp(alog_ref[...]))
    row = lax.broadcasted_iota(jnp.int32, (CHUNK, CHUNK), 0)
    col = lax.broadcasted_iota(jnp.int32, (CHUNK, CHUNK), 1)
    causal = row >= col
    cs = jnp.dot(causal.astype(F32), a, precision=lax.Precision.HIGHEST, preferred_element_type=F32)
    cs_t = cs.T
    dt_t = dt.T

    ys = []
    for g in range(SSM_GROUPS):
        bg = bm[:, g * D_STATE:(g + 1) * D_STATE].astype(BF16)
        cg = cm[:, g * D_STATE:(g + 1) * D_STATE].astype(BF16)
        cb = lax.dot_general(cg, bg, _NT, preferred_element_type=F32)
        xt = [xs[:, (2 * g + k) * 128:(2 * g + k + 1) * 128].T for k in range(2)]
        for r in range(SSM_REP):
            h = g * SSM_REP + r
            cs_col = cs[:, h:h + 1]
            cs_row = cs_t[h:h + 1, :]
            dt_row = dt_t[h:h + 1, :]
            decay = jnp.exp(jnp.where(causal, cs_col - cs_row, -jnp.inf))
            m = (cb * decay * dt_row).astype(BF16)
            xh = xs[:, h * SSM_HEAD_DIM:(h + 1) * SSM_HEAD_DIM]
            y_diag = jnp.dot(m, xh.astype(BF16), preferred_element_type=F32)
            h_prev = st_ref[0, h]
            y_off = lax.dot_general(cg, h_prev.astype(BF16), _NT, preferred_element_type=F32) * jnp.exp(cs_col)
            cs_end = cs_t[h:h + 1, CHUNK - 1:CHUNK]
            w_row = jnp.exp(cs_end - cs_row) * dt_row
            xt_h = xt[r // 2][(r % 2) * SSM_HEAD_DIM:(r % 2 + 1) * SSM_HEAD_DIM, :]
            st = jnp.dot((xt_h * w_row).astype(BF16), bg, preferred_element_type=F32)
            st_ref[0, h] = h_prev * jnp.exp(cs_end) + st
            ys.append(y_diag + y_off)
    y = jnp.concatenate(ys, axis=-1) + xs * dsk_ref[...]
    s_ref[...] = _gated_group_norm(y, z_ref[...], nw_ref[...])


def _ssd_prompt(proj, conv_w, conv_b, dtb, alog, dsk, nw, batch, seq):
    nc = seq // CHUNK
    return pl.pallas_call(
        _ssd_prompt_kernel,
        out_shape=(jax.ShapeDtypeStruct((batch * seq, SSM_WIDTH), F32),
                   jax.ShapeDtypeStruct((batch, CONV_W - 1, CONV_DIM), F32),
                   jax.ShapeDtypeStruct((batch, SSM_HEADS, SSM_HEAD_DIM, D_STATE), F32)),
        grid=(batch, nc),
        in_specs=[pl.BlockSpec((CHUNK, SSM_WIDTH), lambda b, c: (b * nc + c, COL_Z // SSM_WIDTH)),
                  pl.BlockSpec((CHUNK, CONV_DIM), lambda b, c: (b * nc + c, COL_XBC // CONV_DIM)),
                  pl.BlockSpec((CHUNK, DT_PAD), lambda b, c: (b * nc + c, COL_DT // DT_PAD)),
                  _const_spec((CONV_W, CONV_DIM)),
                  _const_spec((1, CONV_DIM)),
                  _const_spec((1, DT_PAD)),
                  _const_spec((1, DT_PAD)),
                  _const_spec((1, SSM_WIDTH)),
                  _const_spec((1, SSM_WIDTH))],
        out_specs=(pl.BlockSpec((CHUNK, SSM_WIDTH), lambda b, c: (b * nc + c, 0)),
                   pl.BlockSpec((1, CONV_W - 1, CONV_DIM), lambda b, c: (b, 0, 0)),
                   pl.BlockSpec((1, SSM_HEADS, SSM_HEAD_DIM, D_STATE), lambda b, c: (b, 0, 0, 0))),
        scratch_shapes=[pltpu.VMEM((8 + CHUNK, CONV_DIM), F32)],
        compiler_params=pltpu.CompilerParams(dimension_semantics=("arbitrary", "arbitrary"),
                                             vmem_limit_bytes=VMEM_LIMIT),
        name="ssd_prompt",
    )(proj, proj, proj, conv_w, conv_b, dtb, alog, dsk, nw)


def _tail_kernel(x_ref, a_ref, s_ref, wo_ref, n2_ref, wg_ref, wu_ref, wd_ref, fn_ref, o_ref):
    x1 = (x_ref[...]
          + jnp.dot(a_ref[...].astype(BF16), wo_ref[0:ATTN_WIDTH, :], preferred_element_type=F32)
          + jnp.dot(s_ref[...].astype(BF16), wo_ref[ATTN_WIDTH:, :], preferred_element_type=F32))
    h2 = _rms(x1, n2_ref[...]).astype(BF16)
    acc = None
    for j in range(N_FF_CHUNKS):
        gate = jnp.dot(h2, wg_ref[j], preferred_element_type=F32)
        up = jnp.dot(h2, wu_ref[j], preferred_element_type=F32)
        act = (_silu(gate) * up).astype(BF16)
        part = jnp.dot(act, wd_ref[j], preferred_element_type=F32)
        acc = part if acc is None else acc + part
    o_ref[...] = _rms(x1 + acc, fn_ref[...])


def _tail(x2d, a, s, wo, n2, wg, wu, wd, fn, tm):
    n = x2d.shape[0]
    return pl.pallas_call(
        _tail_kernel,
        out_shape=jax.ShapeDtypeStruct((n, D_MODEL), F32),
        grid=(n // tm,),
        in_specs=[pl.BlockSpec((tm, D_MODEL), lambda i: (i, 0)),
                  pl.BlockSpec((tm, ATTN_WIDTH), lambda i: (i, 0)),
                  pl.BlockSpec((tm, SSM_WIDTH), lambda i: (i, 0)),
                  _const_spec((D_MODEL, D_MODEL)),
                  _const_spec((1, D_MODEL)),
                  _const_spec((N_FF_CHUNKS, D_MODEL, FF_CHUNK)),
                  _const_spec((N_FF_CHUNKS, D_MODEL, FF_CHUNK)),
                  _const_spec((N_FF_CHUNKS, FF_CHUNK, D_MODEL)),
                  _const_spec((1, D_MODEL))],
        out_specs=pl.BlockSpec((tm, D_MODEL), lambda i: (i, 0)),
        compiler_params=pltpu.CompilerParams(dimension_semantics=("arbitrary",),
                                             vmem_limit_bytes=VMEM_LIMIT),
        name="outproj_ffn",
    )(x2d, a, s, wo, n2, wg, wu, wd, fn)


SAMPLE_BT = 8


def _attn_sample_kernel(q_ref, kn_ref, vn_ref, ck_ref, cv_ref, ts_ref, tn_ref, sink_ref,
                        a_ref, nk_ref, nv_ref):
    rowid = lax.broadcasted_iota(jnp.int32, (N_HEADS, HEAD_DIM), 0)
    ts = ts_ref[...]
    tn = tn_ref[:, 0:1]
    sink = sink_ref[:, 0:1]
    for bb in range(SAMPLE_BT):
        qb = q_ref[bb]
        qbd = jnp.concatenate([jnp.where(rowid < KV_REP, qb, 0.0), jnp.where(rowid >= KV_REP, qb, 0.0)], axis=1)
        kn = kn_ref[bb:bb + 1, :]
        vn = vn_ref[bb:bb + 1, :]
        s_c = lax.dot_general(qbd.astype(BF16), ck_ref[bb].astype(BF16), _NT,
                              preferred_element_type=F32) * ATTN_SCALE + ts
        s_n = jnp.sum(qbd * kn, axis=-1, keepdims=True) * ATTN_SCALE + tn
        m = jnp.maximum(jnp.maximum(jnp.max(s_c, axis=-1, keepdims=True), s_n), sink)
        e_c = jnp.exp(s_c - m)
        e_n = jnp.exp(s_n - m)
        den = jnp.sum(e_c, axis=-1, keepdims=True) + e_n + jnp.exp(sink - m)
        o = jnp.dot((e_c / den).astype(BF16), cv_ref[bb].astype(BF16), preferred_element_type=F32) + (e_n / den) * vn
        a_ref[bb] = jnp.where(rowid < KV_REP, o[:, :HEAD_DIM], o[:, HEAD_DIM:])
        nk_ref[bb, 0:WINDOW - 1, :] = ck_ref[bb, 1:WINDOW, :]
        nk_ref[bb, WINDOW - 1:WINDOW, :] = kn
        nv_ref[bb, 0:WINDOW - 1, :] = cv_ref[bb, 1:WINDOW, :]
        nv_ref[bb, WINDOW - 1:WINDOW, :] = vn


def _attn_sample(q3, proj, ck, cv, ts, tn, sink_b):
    nb = q3.shape[0]
    bt = SAMPLE_BT
    kv = N_KV_HEADS * HEAD_DIM
    return pl.pallas_call(
        _attn_sample_kernel,
        out_shape=(jax.ShapeDtypeStruct((nb, N_HEADS, HEAD_DIM), F32),
                   jax.ShapeDtypeStruct((nb, WINDOW, kv), F32),
                   jax.ShapeDtypeStruct((nb, WINDOW, kv), F32)),
        grid=(nb // bt,),
        in_specs=[pl.BlockSpec((bt, N_HEADS, HEAD_DIM), lambda i: (i, 0, 0)),
                  pl.BlockSpec((bt, kv), lambda i: (i, COL_K // kv)),
                  pl.BlockSpec((bt, kv), lambda i: (i, COL_V // kv)),
                  pl.BlockSpec((bt, WINDOW, kv), lambda i: (i, 0, 0)),
                  pl.BlockSpec((bt, WINDOW, kv), lambda i: (i, 0, 0)),
                  _const_spec((N_HEADS, WINDOW)),
                  _const_spec((N_HEADS, WINDOW)),
                  _const_spec((N_HEADS, WINDOW))],
        out_specs=(pl.BlockSpec((bt, N_HEADS, HEAD_DIM), lambda i: (i, 0, 0)),
                   pl.BlockSpec((bt, WINDOW, kv), lambda i: (i, 0, 0)),
                   pl.BlockSpec((bt, WINDOW, kv), lambda i: (i, 0, 0))),
        compiler_params=pltpu.CompilerParams(dimension_semantics=("arbitrary",),
                                             vmem_limit_bytes=VMEM_LIMIT),
        name="attn_sample",
    )(q3, proj, proj, ck, cv, ts, tn, sink_b)


def _ssm_sample_prep_kernel(xbc_ref, dt_ref, cst_ref, cw_ref, cb_ref, dtb_ref, alog_ref,
                            xs_ref, bm_ref, cm_ref, xdt_t_ref, dec_ref, nc_ref):
    xbc = xbc_ref[...]
    tot = cst_ref[0] * cw_ref[0:1, :]
    tot = tot + cst_ref[1] * cw_ref[1:2, :]
    tot = tot + cst_ref[2] * cw_ref[2:3, :]
    tot = tot + xbc * cw_ref[3:4, :]
    xc = _silu(cb_ref[...] + tot)
    nc_ref[0] = cst_ref[1]
    nc_ref[1] = cst_ref[2]
    nc_ref[2] = xbc
    xs = xc[:, :SSM_WIDTH]
    xs_ref[...] = xs
    bm_ref[...] = xc[:, SSM_WIDTH:SSM_WIDTH + SSM_GROUPS * D_STATE]
    cm_ref[...] = xc[:, SSM_WIDTH + SSM_GROUPS * D_STATE:]
    dt = _softplus(dt_ref[...] + dtb_ref[...])
    dec_ref[...] = jnp.exp(dt * (-jnp.exp(alog_ref[...])))
    dt_t = dt.T
    for k in range(SSM_WIDTH // 128):
        xt = xs[:, k * 128:(k + 1) * 128].T
        for half in range(2):
            h = 2 * k + half
            lo = h * SSM_HEAD_DIM
            xdt_t_ref[lo:lo + SSM_HEAD_DIM, :] = xt[half * SSM_HEAD_DIM:(half + 1) * SSM_HEAD_DIM, :] * dt_t[h:h + 1, :]


def _ssm_sample_prep(proj, conv_state_t, conv_w, conv_b, dtb, alog):
    nb = proj.shape[0]
    return pl.pallas_call(
        _ssm_sample_prep_kernel,
        out_shape=(jax.ShapeDtypeStruct((nb, SSM_WIDTH), F32),
                   jax.ShapeDtypeStruct((nb, SSM_GROUPS * D_STATE), F32),
                   jax.ShapeDtypeStruct((nb, SSM_GROUPS * D_STATE), F32),
                   jax.ShapeDtypeStruct((SSM_WIDTH, nb), F32),
                   jax.ShapeDtypeStruct((nb, DT_PAD), F32),
                   jax.ShapeDtypeStruct((CONV_W - 1, nb, CONV_DIM), F32)),
        grid=(1,),
        in_specs=[pl.BlockSpec((nb, CONV_DIM), lambda i: (0, COL_XBC // CONV_DIM)),
                  pl.BlockSpec((nb, DT_PAD), lambda i: (0, COL_DT // DT_PAD)),
                  _const_spec((CONV_W - 1, nb, CONV_DIM)),
                  _const_spec((CONV_W, CONV_DIM)),
                  _const_spec((1, CONV_DIM)),
                  _const_spec((1, DT_PAD)),
                  _const_spec((1, DT_PAD))],
        out_specs=(_const_spec((nb, SSM_WIDTH)),
                   _const_spec((nb, SSM_GROUPS * D_STATE)),
                   _const_spec((nb, SSM_GROUPS * D_STATE)),
                   _const_spec((SSM_WIDTH, nb)),
                   _const_spec((nb, DT_PAD)),
                   _const_spec((CONV_W - 1, nb, CONV_DIM))),
        compiler_params=pltpu.CompilerParams(dimension_semantics=("arbitrary",),
                                             vmem_limit_bytes=VMEM_LIMIT),
        name="ssm_sample_prep",
    )(proj, proj, conv_state_t, conv_w, conv_b, dtb, alog)


def _ssm_sample_state_kernel(dec_ref, xdt_t_ref, bm_ref, cm_ref, xs_ref, z_ref, dsk_ref, nw_ref, st_ref,
                             nst_ref, s_ref, yt_ref):
    i = pl.program_id(0)
    last = pl.num_programs(0) - 1
    rows = SSM_REP * SSM_HEAD_DIM

    @pl.when(i == 0)
    def _():
        yt_ref[...] = jnp.zeros_like(yt_ref)

    lane = lax.broadcasted_iota(jnp.int32, (rows, 128), 1)
    for bb in range(SAMPLE_BT):
        b = i * SAMPLE_BT + bb
        mine = lane == b
        for g in range(SSM_GROUPS):
            bg = bm_ref[:, g * D_STATE:(g + 1) * D_STATE].astype(BF16)
            cg = cm_ref[:, g * D_STATE:(g + 1) * D_STATE].astype(BF16)
            xsel = jnp.where(mine, xdt_t_ref[g * rows:(g + 1) * rows, :], 0.0).astype(BF16)
            outer = jnp.dot(xsel, bg, preferred_element_type=F32)
            new = []
            for r in range(SSM_REP):
                h = g * SSM_REP + r
                hn = st_ref[bb, h] * dec_ref[b * SSM_HEADS + h] + outer[r * SSM_HEAD_DIM:(r + 1) * SSM_HEAD_DIM, :]
                nst_ref[bb, h] = hn
                new.append(hn)
            hcat = jnp.concatenate(new, axis=0).astype(BF16)
            res = lax.dot_general(hcat, cg, _NT, preferred_element_type=F32)
            yt_ref[g * rows:(g + 1) * rows, :] += jnp.where(mine, res, 0.0)

    @pl.when(i == last)
    def _():
        y = jnp.concatenate([yt_ref[k * 128:(k + 1) * 128, :].T for k in range(SSM_WIDTH // 128)], axis=1)
        y = y + xs_ref[...] * dsk_ref[...]
        s_ref[...] = _gated_group_norm(y, z_ref[...], nw_ref[...])


def _ssm_sample_state(dec_flat, xdt_t, bm, cm, xs, proj, dsk, nw, state):
    nb = state.shape[0]
    bt = SAMPLE_BT
    return pl.pallas_call(
        _ssm_sample_state_kernel,
        out_shape=(jax.ShapeDtypeStruct(state.shape, F32),
                   jax.ShapeDtypeStruct((nb, SSM_WIDTH), F32)),
        grid=(nb // bt,),
        in_specs=[pl.BlockSpec(memory_space=pltpu.SMEM),
                  _const_spec((SSM_WIDTH, nb)),
                  _const_spec((nb, SSM_GROUPS * D_STATE)),
                  _const_spec((nb, SSM_GROUPS * D_STATE)),
                  _const_spec((nb, SSM_WIDTH)),
                  pl.BlockSpec((nb, SSM_WIDTH), lambda i: (0, COL_Z // SSM_WIDTH)),
                  _const_spec((1, SSM_WIDTH)),
                  _const_spec((1, SSM_WIDTH)),
                  pl.BlockSpec((bt, SSM_HEADS, SSM_HEAD_DIM, D_STATE), lambda i: (i, 0, 0, 0))],
        out_specs=(pl.BlockSpec((bt, SSM_HEADS, SSM_HEAD_DIM, D_STATE), lambda i: (i, 0, 0, 0)),
                   pl.BlockSpec((nb, SSM_WIDTH), lambda i: (0, 0))),
        scratch_shapes=[pltpu.VMEM((SSM_WIDTH, nb), F32)],
        compiler_params=pltpu.CompilerParams(dimension_semantics=("arbitrary",),
                                             vmem_limit_bytes=VMEM_LIMIT),
        name="ssm_sample_state",
    )(dec_flat, xdt_t, bm, cm, xs, proj, dsk, nw, state)


def _pad_lanes(v, width):
    return jnp.pad(v.reshape(1, -1), ((0, 0), (0, width - v.shape[-1])))


def kernel(x_prompt, x_sample, cache_k, cache_v, state_conv, state_ssm, rel_bias, norm1_w, w_in, attn_sinks,
           conv_w, conv_b, dt_bias, A_log, D_skip, ssm_norm_w, w_out, norm2_w, w_gate, w_up, w_down, final_norm_w):
    depth = norm1_w.shape[0]
    assert depth == 1, "single-layer trunk"
    batch, seq, _ = x_prompt.shape
    nb = x_sample.shape[0]
    assert x_sample.shape[1] == 1 and nb == 128

    wi = w_in[0]
    q_c, k_c, v_c, z_c, xbc_c, dt_c = (wi[:, 0:512], wi[:, 512:640], wi[:, 640:768], wi[:, 768:1280],
                                       wi[:, 1280:2304], wi[:, 2304:2312])
    w_perm = jnp.concatenate([q_c, z_c, xbc_c, k_c, v_c, jnp.pad(dt_c, ((0, 0), (0, DT_PAD - SSM_HEADS)))],
                             axis=1).astype(BF16)
    wo = w_out[0].astype(BF16)
    wg = w_gate[0].reshape(D_MODEL, N_FF_CHUNKS, FF_CHUNK).transpose(1, 0, 2).astype(BF16)
    wu = w_up[0].reshape(D_MODEL, N_FF_CHUNKS, FF_CHUNK).transpose(1, 0, 2).astype(BF16)
    wd = w_down[0].reshape(N_FF_CHUNKS, FF_CHUNK, D_MODEL).astype(BF16)
    n1 = norm1_w[0].reshape(1, D_MODEL)
    n2 = norm2_w[0].reshape(1, D_MODEL)
    fn = final_norm_w.reshape(1, D_MODEL)
    cw = conv_w[0]
    cb = conv_b[0].reshape(1, CONV_DIM)
    dtb = _pad_lanes(dt_bias[0], DT_PAD)
    alog = _pad_lanes(A_log[0], DT_PAD)
    dsk = jnp.repeat(D_skip[0], SSM_HEAD_DIM).reshape(1, SSM_WIDTH)
    nw = ssm_norm_w[0].reshape(1, SSM_WIDTH)
    sinks = attn_sinks[0]
    sink_b = jnp.broadcast_to(sinks[:, None], (N_HEADS, WINDOW))

    tab_p, tab_s, tab_n = _bias_tables(rel_bias)

    xp2 = x_prompt.reshape(batch * seq, D_MODEL)
    proj_p = _inproj(xp2, n1, w_perm, 512)
    a_p = _attn_prompt(proj_p, sinks, tab_p, batch, seq)
    s_p, conv_p, ssm_p = _ssd_prompt(proj_p, cw, cb, dtb, alog, dsk, nw, batch, seq)
    y_p = _tail(xp2, a_p, s_p, wo, n2, wg, wu, wd, fn, 256)
    kv_p = proj_p[:, COL_K:COL_V + N_KV_HEADS * HEAD_DIM].reshape(batch, seq, 2, N_KV_HEADS, HEAD_DIM)[:, -WINDOW:]

    xs2 = x_sample.reshape(nb, D_MODEL)
    proj_s = _inproj(xs2, n1, w_perm, nb)
    q3 = proj_s[:, COL_Q:COL_Q + ATTN_WIDTH].reshape(nb, N_HEADS, HEAD_DIM)
    ck = cache_k[0].reshape(nb, WINDOW, N_KV_HEADS * HEAD_DIM)
    cv = cache_v[0].reshape(nb, WINDOW, N_KV_HEADS * HEAD_DIM)
    a_s3, nk, nv = _attn_sample(q3, proj_s, ck, cv, tab_s, tab_n, sink_b)
    conv_t = jnp.transpose(state_conv[0], (1, 0, 2))
    xs_s, bm_s, cm_s, xdt_t, dec, nconv_t = _ssm_sample_prep(proj_s, conv_t, cw, cb, dtb, alog)
    dec_flat = dec[:, :SSM_HEADS].reshape(nb * SSM_HEADS)
    nssm, s_s = _ssm_sample_state(dec_flat, xdt_t, bm_s, cm_s, xs_s, proj_s, dsk, nw, state_ssm[0])
    y_s = _tail(xs2, a_s3.reshape(nb, ATTN_WIDTH), s_s, wo, n2, wg, wu, wd, fn, nb)

    return (y_p.reshape(batch, seq, D_MODEL),
            y_s.reshape(nb, 1, D_MODEL),
            kv_p[:, :, 0][None],
            kv_p[:, :, 1][None],
            conv_p[None],
            ssm_p[None],
            nk.reshape(1, nb, WINDOW, N_KV_HEADS, HEAD_DIM),
            nv.reshape(1, nb, WINDOW, N_KV_HEADS, HEAD_DIM),
            jnp.transpose(nconv_t, (1, 0, 2))[None],
            nssm[None])
```

```python
import functools
import math

import numpy as np
import jax
import jax.numpy as jnp
from jax import lax
from jax.experimental import pallas as pl
from jax.experimental.pallas import tpu as pltpu

F32 = jnp.float32
BF16 = jnp.bfloat16

D_MODEL = 1024
HEAD_DIM = 64
N_HEADS = 8
N_KV_HEADS = 2
KV_REP = 4
WINDOW = 128
ATTN_WIDTH = 512
ATTN_SCALE = HEAD_DIM ** -0.5
N_BUCKETS = 32
MAX_DISTANCE = 128
SSM_WIDTH = 512
SSM_HEADS = 8
SSM_GROUPS = 2
SSM_REP = 4
SSM_HEAD_DIM = 64
D_STATE = 128
CONV_W = 4
CONV_DIM = 1024
CHUNK = 128
D_FF = 2816
EPS = 1e-6

COL_Q, COL_Z, COL_XBC, COL_K, COL_V, COL_DT = 0, 512, 1024, 2048, 2176, 2304
PROJ_COLS = 2432
DT_PAD = 128

FF_CHUNK = 256
N_FF_CHUNKS = D_FF // FF_CHUNK

VMEM_LIMIT = 56 * 1024 * 1024

_NT = (((1,), (1,)), ((), ()))


def _bucket_table(dist):
    n = np.maximum(dist, 0)
    exact = N_BUCKETS // 2
    nf = np.maximum(n, 1).astype(np.float32)
    large = exact + (np.log(nf / exact) / math.log(MAX_DISTANCE / exact) * (N_BUCKETS - exact)).astype(np.int32)
    return np.where(n < exact, n, np.minimum(large, N_BUCKETS - 1)).astype(np.int32)


def _prompt_buckets():
    dist = np.arange(WINDOW)[:, None] + WINDOW - np.arange(2 * WINDOW)[None, :]
    band = (dist >= 0) & (dist < WINDOW)
    return np.where(band, _bucket_table(dist), -1).astype(np.int32)


def _sample_buckets():
    dist = WINDOW - np.arange(WINDOW)[None, :]
    band = (dist >= 0) & (dist < WINDOW)
    row = np.where(band, _bucket_table(dist), -1).astype(np.int32)
    return np.tile(row, (N_HEADS, 1))


def _rms(x, w):
    return x * lax.rsqrt(jnp.mean(x * x, axis=-1, keepdims=True) + EPS) * w


def _silu(x):
    return x * jax.nn.sigmoid(x)


def _softplus(x):
    return jnp.maximum(x, 0.0) + jnp.log1p(jnp.exp(-jnp.abs(x)))


def _const_spec(shape):
    nd = len(shape)
    return pl.BlockSpec(shape, lambda *_: (0,) * nd, pipeline_mode=pl.Buffered(1))


def _bias_kernel(rb_ref, bp_ref, bs_ref, tp_ref, ts_ref, tn_ref):
    bp = bp_ref[...]
    bs = bs_ref[...]
    rowid = lax.broadcasted_iota(jnp.int32, (N_HEADS, WINDOW), 0)
    ts = jnp.zeros((N_HEADS, WINDOW), F32)
    tn = jnp.zeros((N_HEADS, WINDOW), F32)
    for h in range(N_HEADS):
        tp = jnp.zeros((WINDOW, 2 * WINDOW), F32)
        for bk in range(N_BUCKETS):
            v = rb_ref[bk, h]
            tp = jnp.where(bp == bk, v, tp)
            ts = jnp.where((bs == bk) & (rowid == h), v, ts)
        tp_ref[h] = jnp.where(bp < 0, -jnp.inf, tp)
        tn = jnp.where(rowid == h, rb_ref[0, h], tn)
    ts_ref[...] = jnp.where(bs < 0, -jnp.inf, ts)
    tn_ref[...] = tn


def _bias_tables(rel_bias):
    return pl.pallas_call(
        _bias_kernel,
        out_shape=(jax.ShapeDtypeStruct((N_HEADS, WINDOW, 2 * WINDOW), F32),
                   jax.ShapeDtypeStruct((N_HEADS, WINDOW), F32),
                   jax.ShapeDtypeStruct((N_HEADS, WINDOW), F32)),
        in_specs=[pl.BlockSpec(memory_space=pltpu.SMEM),
                  pl.BlockSpec(memory_space=pltpu.VMEM),
                  pl.BlockSpec(memory_space=pltpu.VMEM)],
        name="bias_tables",
    )(rel_bias, jnp.asarray(_prompt_buckets()), jnp.asarray(_sample_buckets()))


def _inproj_kernel(x_ref, nw_ref, w_ref, o_ref):
    h = _rms(x_ref[...], nw_ref[...]).astype(BF16)
    o_ref[...] = jnp.dot(h, w_ref[...], preferred_element_type=F32)


def _inproj(x2d, norm_w, w_perm, tm):
    n = x2d.shape[0]
    return pl.pallas_call(
        _inproj_kernel,
        out_shape=jax.ShapeDtypeStruct((n, PROJ_COLS), F32),
        grid=(n // tm,),
        in_specs=[pl.BlockSpec((tm, D_MODEL), lambda i: (i, 0)),
                  _const_spec((1, D_MODEL)),
                  _const_spec((D_MODEL, PROJ_COLS))],
        out_specs=pl.BlockSpec((tm, PROJ_COLS), lambda i: (i, 0)),
        compiler_params=pltpu.CompilerParams(dimension_semantics=("arbitrary",),
                                             vmem_limit_bytes=VMEM_LIMIT),
        name="inproj",
    )(x2d, norm_w, w_perm)


def _attn_prompt_kernel(sink_ref, q_ref, kp_ref, kc_ref, vp_ref, vc_ref, tab_ref, o_ref):
    n = pl.program_id(1)
    q = q_ref[...].astype(BF16)
    k = jnp.concatenate([kp_ref[...], kc_ref[...]], axis=0).astype(BF16)
    v = jnp.concatenate([vp_ref[...], vc_ref[...]], axis=0).astype(BF16)
    col = lax.broadcasted_iota(jnp.int32, (WINDOW, 2 * WINDOW), 1)
    first_valid = jnp.where(n > 0, 0, WINDOW)
    ok = col >= first_valid
    outs = []
    for hh in range(N_HEADS):
        g = hh // KV_REP
        s = lax.dot_general(q[:, hh * HEAD_DIM:(hh + 1) * HEAD_DIM], k[:, g * HEAD_DIM:(g + 1) * HEAD_DIM],
                            _NT, preferred_element_type=F32)
        s = s * ATTN_SCALE + tab_ref[hh]
        s = jnp.where(ok, s, -jnp.inf)
        sk = sink_ref[hh]
        m = jnp.maximum(jnp.max(s, axis=-1, keepdims=True), sk)
        e = jnp.exp(s - m)
        den = jnp.sum(e, axis=-1, keepdims=True) + jnp.exp(sk - m)
        p = (e / den).astype(BF16)
        outs.append(jnp.dot(p, v[:, g * HEAD_DIM:(g + 1) * HEAD_DIM], preferred_element_type=F32))
    o_ref[...] = jnp.concatenate(outs, axis=-1)


def _attn_prompt(proj, sinks, tab, batch, seq):
    nb = seq // WINDOW
    kcol, vcol = COL_K // WINDOW, COL_V // WINDOW

    def cur(c):
        return lambda b, n: (b * nb + n, c)

    def prev(c):
        return lambda b, n: (b * nb + jnp.maximum(n - 1, 0), c)

    return pl.pallas_call(
        _attn_prompt_kernel,
        out_shape=jax.ShapeDtypeStruct((batch * seq, ATTN_WIDTH), F32),
        grid=(batch, nb),
        in_specs=[pl.BlockSpec(memory_space=pltpu.SMEM),
                  pl.BlockSpec((WINDOW, ATTN_WIDTH), lambda b, n: (b * nb + n, 0)),
                  pl.BlockSpec((WINDOW, WINDOW), prev(kcol)),
                  pl.BlockSpec((WINDOW, WINDOW), cur(kcol)),
                  pl.BlockSpec((WINDOW, WINDOW), prev(vcol)),
                  pl.BlockSpec((WINDOW, WINDOW), cur(vcol)),
                  _const_spec((N_HEADS, WINDOW, 2 * WINDOW))],
        out_specs=pl.BlockSpec((WINDOW, ATTN_WIDTH), lambda b, n: (b * nb + n, 0)),
        compiler_params=pltpu.CompilerParams(dimension_semantics=("arbitrary", "arbitrary"),
                                             vmem_limit_bytes=VMEM_LIMIT),
        name="attn_prompt",
    )(sinks, proj, proj, proj, proj, proj, tab)


def _gated_group_norm(y, z, nw):
    y = y * _silu(z)
    half = SSM_WIDTH // SSM_GROUPS
    parts = []
    for g in range(SSM_GROUPS):
        yg = y[:, g * half:(g + 1) * half]
        parts.append(yg * lax.rsqrt(jnp.mean(yg * yg, axis=-1, keepdims=True) + EPS))
    return jnp.concatenate(parts, axis=-1) * nw


def _ssd_prompt_kernel(z_ref, xbc_ref, dt_ref, cw_ref, cb_ref, dtb_ref, alog_ref, dsk_ref, nw_ref,
                       s_ref, conv_ref, st_ref, xp_ref):
    c = pl.program_id(1)
    last = pl.num_programs(1) - 1

    @pl.when(c == 0)
    def _():
        xp_ref[0:8, :] = jnp.zeros((8, CONV_DIM), F32)
        st_ref[...] = jnp.zeros_like(st_ref)

    xp_ref[8:8 + CHUNK, :] = xbc_ref[...]
    tot = xp_ref[5:5 + CHUNK, :] * cw_ref[0:1, :]
    for j in range(1, CONV_W):
        tot = tot + xp_ref[5 + j:5 + j + CHUNK, :] * cw_ref[j:j + 1, :]
    xc = _silu(cb_ref[...] + tot)
    tail = xp_ref[5 + CHUNK:8 + CHUNK, :]

    @pl.when(c == last)
    def _():
        conv_ref[0] = tail

    xp_ref[5:8, :] = tail

    xs = xc[:, :SSM_WIDTH]
    bm = xc[:, SSM_WIDTH:SSM_WIDTH + SSM_GROUPS * D_STATE]
    cm = xc[:, SSM_WIDTH + SSM_GROUPS * D_STATE:]

    dt = _softplus(dt_ref[...] + dtb_ref[...])
    a = dt * (-jnp.exp(alog_ref[...]))
    row = lax.broadcasted_iota(jnp.int32, (CHUNK, CHUNK), 0)
    col = lax.broadcasted_iota(jnp.int32, (CHUNK, CHUNK), 1)
    causal = row >= col
    cs = jnp.dot(causal.astype(F32), a, precision=lax.Precision.HIGHEST, preferred_element_type=F32)
    cs_t = cs.T
    dt_t = dt.T

    ys = []
    for g in range(SSM_GROUPS):
        bg = bm[:, g * D_STATE:(g + 1) * D_STATE].astype(BF16)
        cg = cm[:, g * D_STATE:(g + 1) * D_STATE].astype(BF16)
        cb = lax.dot_general(cg, bg, _NT, preferred_element_type=F32)
        xt = [xs[:, (2 * g + k) * 128:(2 * g + k + 1) * 128].T for k in range(2)]
        for r in range(SSM_REP):
            h = g * SSM_REP + r
            cs_col = cs[:, h:h + 1]
            cs_row = cs_t[h:h + 1, :]
            dt_row = dt_t[h:h + 1, :]
            decay = jnp.exp(jnp.where(causal, cs_col - cs_row, -jnp.inf))
            m = (cb * decay * dt_row).astype(BF16)
            xh = xs[:, h * SSM_HEAD_DIM:(h + 1) * SSM_HEAD_DIM]
            y_diag = jnp.dot(m, xh.astype(BF16), preferred_element_type=F32)
            h_prev = st_ref[0, h]
            y_off = lax.dot_general(cg, h_prev.astype(BF16), _NT, preferred_element_type=F32) * jnp.exp(cs_col)
            cs_end = cs_t[h:h + 1, CHUNK - 1:CHUNK]
            w_row = jnp.exp(cs_end - cs_row) * dt_row
            xt_h = xt[r // 2][(r % 2) * SSM_HEAD_DIM:(r % 2 + 1) * SSM_HEAD_DIM, :]
            st = jnp.dot((xt_h * w_row).astype(BF16), bg, preferred_element_type=F32)
            st_ref[0, h] = h_prev * jnp.exp(cs_end) + st
            ys.append(y_diag + y_off)
    y = jnp.concatenate(ys, axis=-1) + xs * dsk_ref[...]
    s_ref[...] = _gated_group_norm(y, z_ref[...], nw_ref[...])


def _ssd_prompt(proj, conv_w, conv_b, dtb, alog, dsk, nw, batch, seq):
    nc = seq // CHUNK
    return pl.pallas_call(
        _ssd_prompt_kernel,
        out_shape=(jax.ShapeDtypeStruct((batch * seq, SSM_WIDTH), F32),
                   jax.ShapeDtypeStruct((batch, CONV_W - 1, CONV_DIM), F32),
                   jax.ShapeDtypeStruct((batch, SSM_HEADS, SSM_HEAD_DIM, D_STATE), F32)),
        grid=(batch, nc),
        in_specs=[pl.BlockSpec((CHUNK, SSM_WIDTH), lambda b, c: (b * nc + c, COL_Z // SSM_WIDTH)),
                  pl.BlockSpec((CHUNK, CONV_DIM), lambda b, c: (b * nc + c, COL_XBC // CONV_DIM)),
                  pl.BlockSpec((CHUNK, DT_PAD), lambda b, c: (b * nc + c, COL_DT // DT_PAD)),
                  _const_spec((CONV_W, CONV_DIM)),
                  _const_spec((1, CONV_DIM)),
                  _const_spec((1, DT_PAD)),
                  _const_spec((1, DT_PAD)),
                  _const_spec((1, SSM_WIDTH)),
                  _const_spec((1, SSM_WIDTH))],
        out_specs=(pl.BlockSpec((CHUNK, SSM_WIDTH), lambda b, c: (b * nc + c, 0)),
                   pl.BlockSpec((1, CONV_W - 1, CONV_DIM), lambda b, c: (b, 0, 0)),
                   pl.BlockSpec((1, SSM_HEADS, SSM_HEAD_DIM, D_STATE), lambda b, c: (b, 0, 0, 0))),
        scratch_shapes=[pltpu.VMEM((8 + CHUNK, CONV_DIM), F32)],
        compiler_params=pltpu.CompilerParams(dimension_semantics=("arbitrary", "arbitrary"),
                                             vmem_limit_bytes=VMEM_LIMIT),
        name="ssd_prompt",
    )(proj, proj, proj, conv_w, conv_b, dtb, alog, dsk, nw)


def _tail_kernel(x_ref, a_ref, s_ref, wo_ref, n2_ref, wg_ref, wu_ref, wd_ref, fn_ref, o_ref):
    x1 = (x_ref[...]
          + jnp.dot(a_ref[...].astype(BF16), wo_ref[0:ATTN_WIDTH, :], preferred_element_type=F32)
          + jnp.dot(s_ref[...].astype(BF16), wo_ref[ATTN_WIDTH:, :], preferred_element_type=F32))
    h2 = _rms(x1, n2_ref[...]).astype(BF16)
    acc = None
    for j in range(N_FF_CHUNKS):
        gate = jnp.dot(h2, wg_ref[j], preferred_element_type=F32)
        up = jnp.dot(h2, wu_ref[j], preferred_element_type=F32)
        act = (_silu(gate) * up).astype(BF16)
        part = jnp.dot(act, wd_ref[j], preferred_element_type=F32)
        acc = part if acc is None else acc + part
    o_ref[...] = _rms(x1 + acc, fn_ref[...])


def _tail(x2d, a, s, wo, n2, wg, wu, wd, fn, tm):
    n = x2d.shape[0]
    return pl.pallas_call(
        _tail_kernel,
        out_shape=jax.ShapeDtypeStruct((n, D_MODEL), F32),
        grid=(n // tm,),
        in_specs=[pl.BlockSpec((tm, D_MODEL), lambda i: (i, 0)),
                  pl.BlockSpec((tm, ATTN_WIDTH), lambda i: (i, 0)),
                  pl.BlockSpec((tm, SSM_WIDTH), lambda i: (i, 0)),
                  _const_spec((D_MODEL, D_MODEL)),
                  _const_spec((1, D_MODEL)),
                  _const_spec((N_FF_CHUNKS, D_MODEL, FF_CHUNK)),
                  _const_spec((N_FF_CHUNKS, D_MODEL, FF_CHUNK)),
                  _const_spec((N_FF_CHUNKS, FF_CHUNK, D_MODEL)),
                  _const_spec((1, D_MODEL))],
        out_specs=pl.BlockSpec((tm, D_MODEL), lambda i: (i, 0)),
        compiler_params=pltpu.CompilerParams(dimension_semantics=("arbitrary",),
                                             vmem_limit_bytes=VMEM_LIMIT),
        name="outproj_ffn",
    )(x2d, a, s, wo, n2, wg, wu, wd, fn)


SAMPLE_BT = 8


def _attn_sample_kernel(q_ref, kn_ref, vn_ref, ck_ref, cv_ref, ts_ref, tn_ref, sink_ref,
                        a_ref, nk_ref, nv_ref):
    rowid = lax.broadcasted_iota(jnp.int32, (N_HEADS, HEAD_DIM), 0)
    ts = ts_ref[...]
    tn = tn_ref[:, 0:1]
    sink = sink_ref[:, 0:1]
    for bb in range(SAMPLE_BT):
        qb = q_ref[bb]
        qbd = jnp.concatenate([jnp.where(rowid < KV_REP, qb, 0.0), jnp.where(rowid >= KV_REP, qb, 0.0)], axis=1)
        kn = kn_ref[bb:bb + 1, :]
        vn = vn_ref[bb:bb + 1, :]
        s_c = lax.dot_general(qbd.astype(BF16), ck_ref[bb].astype(BF16), _NT,
                              preferred_element_type=F32) * ATTN_SCALE + ts
        s_n = jnp.sum(qbd * kn, axis=-1, keepdims=True) * ATTN_SCALE + tn
        m = jnp.maximum(jnp.maximum(jnp.max(s_c, axis=-1, keepdims=True), s_n), sink)
        e_c = jnp.exp(s_c - m)
        e_n = jnp.exp(s_n - m)
        den = jnp.sum(e_c, axis=-1, keepdims=True) + e_n + jnp.exp(sink - m)
        o = jnp.dot((e_c / den).astype(BF16), cv_ref[bb].astype(BF16), preferred_element_type=F32) + (e_n / den) * vn
        a_ref[bb] = jnp.where(rowid < KV_REP, o[:, :HEAD_DIM], o[:, HEAD_DIM:])
        nk_ref[bb, 0:WINDOW - 1, :] = ck_ref[bb, 1:WINDOW, :]
        nk_ref[bb, WINDOW - 1:WINDOW, :] = kn
        nv_ref[bb, 0:WINDOW - 1, :] = cv_ref[bb, 1:WINDOW, :]
        nv_ref[bb, WINDOW - 1:WINDOW, :] = vn


def _attn_sample(q3, proj, ck, cv, ts, tn, sink_b):
    nb = q3.shape[0]
    bt = SAMPLE_BT
    kv = N_KV_HEADS * HEAD_DIM
    return pl.pallas_call(
        _attn_sample_kernel,
        out_shape=(jax.ShapeDtypeStruct((nb, N_HEADS, HEAD_DIM), F32),
                   jax.ShapeDtypeStruct((nb, WINDOW, kv), F32),
                   jax.ShapeDtypeStruct((nb, WINDOW, kv), F32)),
        grid=(nb // bt,),
        in_specs=[pl.BlockSpec((bt, N_HEADS, HEAD_DIM), lambda i: (i, 0, 0)),
                  pl.BlockSpec((bt, kv), lambda i: (i, COL_K // kv)),
                  pl.BlockSpec((bt, kv), lambda i: (i, COL_V // kv)),
                  pl.BlockSpec((bt, WINDOW, kv), lambda i: (i, 0, 0)),
                  pl.BlockSpec((bt, WINDOW, kv), lambda i: (i, 0, 0)),
                  _const_spec((N_HEADS, WINDOW)),
                  _const_spec((N_HEADS, WINDOW)),
                  _const_spec((N_HEADS, WINDOW))],
        out_specs=(pl.BlockSpec((bt, N_HEADS, HEAD_DIM), lambda i: (i, 0, 0)),
                   pl.BlockSpec((bt, WINDOW, kv), lambda i: (i, 0, 0)),
                   pl.BlockSpec((bt, WINDOW, kv), lambda i: (i, 0, 0))),
        compiler_params=pltpu.CompilerParams(dimension_semantics=("arbitrary",),
                                             vmem_limit_bytes=VMEM_LIMIT),
        name="attn_sample",
    )(q3, proj, proj, ck, cv, ts, tn, sink_b)


def _ssm_sample_prep_kernel(xbc_ref, dt_ref, cst_ref, cw_ref, cb_ref, dtb_ref, alog_ref,
                            xs_ref, bm_ref, cm_ref, xdt_t_ref, dec_ref, nc_ref):
    xbc = xbc_ref[...]
    tot = cst_ref[0] * cw_ref[0:1, :]
    tot = tot + cst_ref[1] * cw_ref[1:2, :]
    tot = tot + cst_ref[2] * cw_ref[2:3, :]
    tot = tot + xbc * cw_ref[3:4, :]
    xc = _silu(cb_ref[...] + tot)
    nc_ref[0] = cst_ref[1]
    nc_ref[1] = cst_ref[2]
    nc_ref[2] = xbc
    xs = xc[:, :SSM_WIDTH]
    xs_ref[...] = xs
    bm_ref[...] = xc[:, SSM_WIDTH:SSM_WIDTH + SSM_GROUPS * D_STATE]
    cm_ref[...] = xc[:, SSM_WIDTH + SSM_GROUPS * D_STATE:]
    dt = _softplus(dt_ref[...] + dtb_ref[...])
    dec_ref[...] = jnp.exp(dt * (-jnp.exp(alog_ref[...])))
    dt_t = dt.T
    for k in range(SSM_WIDTH // 128):
        xt = xs[:, k * 128:(k + 1) * 128].T
        for half in range(2):
            h = 2 * k + half
            lo = h * SSM_HEAD_DIM
            xdt_t_ref[lo:lo + SSM_HEAD_DIM, :] = xt[half * SSM_HEAD_DIM:(half + 1) * SSM_HEAD_DIM, :] * dt_t[h:h + 1, :]


def _ssm_sample_prep(proj, conv_state_t, conv_w, conv_b, dtb, alog):
    nb = proj.shape[0]
    return pl.pallas_call(
        _ssm_sample_prep_kernel,
        out_shape=(jax.ShapeDtypeStruct((nb, SSM_WIDTH), F32),
                   jax.ShapeDtypeStruct((nb, SSM_GROUPS * D_STATE), F32),
                   jax.ShapeDtypeStruct((nb, SSM_GROUPS * D_STATE), F32),
                   jax.ShapeDtypeStruct((SSM_WIDTH, nb), F32),
                   jax.ShapeDtypeStruct((nb, DT_PAD), F32),
                   jax.ShapeDtypeStruct((CONV_W - 1, nb, CONV_DIM), F32)),
        grid=(1,),
        in_specs=[pl.BlockSpec((nb, CONV_DIM), lambda i: (0, COL_XBC // CONV_DIM)),
                  pl.BlockSpec((nb, DT_PAD), lambda i: (0, COL_DT // DT_PAD)),
                  _const_spec((CONV_W - 1, nb, CONV_DIM)),
                  _const_spec((CONV_W, CONV_DIM)),
                  _const_spec((1, CONV_DIM)),
                  _const_spec((1, DT_PAD)),
                  _const_spec((1, DT_PAD))],
        out_specs=(_const_spec((nb, SSM_WIDTH)),
                   _const_spec((nb, SSM_GROUPS * D_STATE)),
                   _const_spec((nb, SSM_GROUPS * D_STATE)),
                   _const_spec((SSM_WIDTH, nb)),
                   _const_spec((nb, DT_PAD)),
                   _const_spec((CONV_W - 1, nb, CONV_DIM))),
        compiler_params=pltpu.CompilerParams(dimension_semantics=("arbitrary",),
                                             vmem_limit_bytes=VMEM_LIMIT),
        name="ssm_sample_prep",
    )(proj, proj, conv_state_t, conv_w, conv_b, dtb, alog)


def _ssm_sample_state_kernel(dec_ref, xdt_t_ref, bm_ref, cm_ref, xs_ref, z_ref, dsk_ref, nw_ref, st_ref,
                             nst_ref, s_ref, yt_ref):
    i = pl.program_id(0)
    last = pl.num_programs(0) - 1
    rows = SSM_REP * SSM_HEAD_DIM

    @pl.when(i == 0)
    def _():
        yt_ref[...] = jnp.zeros_like(yt_ref)

    lane = lax.broadcasted_iota(jnp.int32, (rows, 128), 1)
    for bb in range(SAMPLE_BT):
        b = i * SAMPLE_BT + bb
        mine = lane == b
        for g in range(SSM_GROUPS):
            bg = bm_ref[:, g * D_STATE:(g + 1) * D_STATE].astype(BF16)
            cg = cm_ref[:, g * D_STATE:(g + 1) * D_STATE].astype(BF16)
            xsel = jnp.where(mine, xdt_t_ref[g * rows:(g + 1) * rows, :], 0.0).astype(BF16)
            outer = jnp.dot(xsel, bg, preferred_element_type=F32)
            new = []
            for r in range(SSM_REP):
                h = g * SSM_REP + r
                hn = st_ref[bb, h] * dec_ref[b * SSM_HEADS + h] + outer[r * SSM_HEAD_DIM:(r + 1) * SSM_HEAD_DIM, :]
                nst_ref[bb, h] = hn
                new.append(hn)
            hcat = jnp.concatenate(new, axis=0).astype(BF16)
            res = lax.dot_general(hcat, cg, _NT, preferred_element_type=F32)
            yt_ref[g * rows:(g + 1) * rows, :] += jnp.where(mine, res, 0.0)

    @pl.when(i == last)
    def _():
        y = jnp.concatenate([yt_ref[k * 128:(k + 1) * 128, :].T for k in range(SSM_WIDTH // 128)], axis=1)
        y = y + xs_ref[...] * dsk_ref[...]
        s_ref[...] = _gated_group_norm(y, z_ref[...], nw_ref[...])


def _ssm_sample_state(dec_flat, xdt_t, bm, cm, xs, proj, dsk, nw, state):
    nb = state.shape[0]
    bt = SAMPLE_BT
    return pl.pallas_call(
        _ssm_sample_state_kernel,
        out_shape=(jax.ShapeDtypeStruct(state.shape, F32),
                   jax.ShapeDtypeStruct((nb, SSM_WIDTH), F32)),
        grid=(nb // bt,),
        in_specs=[pl.BlockSpec(memory_space=pltpu.SMEM),
                  _const_spec((SSM_WIDTH, nb)),
                  _const_spec((nb, SSM_GROUPS * D_STATE)),
                  _const_spec((nb, SSM_GROUPS * D_STATE)),
                  _const_spec((nb, SSM_WIDTH)),
                  pl.BlockSpec((nb, SSM_WIDTH), lambda i: (0, COL_Z // SSM_WIDTH)),
                  _const_spec((1, SSM_WIDTH)),
                  _const_spec((1, SSM_WIDTH)),
                  pl.BlockSpec((bt, SSM_HEADS, SSM_HEAD_DIM, D_STATE), lambda i: (i, 0, 0, 0))],
        out_specs=(pl.BlockSpec((bt, SSM_HEADS, SSM_HEAD_DIM, D_STATE), lambda i: (i, 0, 0, 0)),
                   pl.BlockSpec((nb, SSM_WIDTH), lambda i: (0, 0))),
        scratch_shapes=[pltpu.VMEM((SSM_WIDTH, nb), F32)],
        compiler_params=pltpu.CompilerParams(dimension_semantics=("arbitrary",),
                                             vmem_limit_bytes=VMEM_LIMIT),
        name="ssm_sample_state",
    )(dec_flat, xdt_t, bm, cm, xs, proj, dsk, nw, state)


def _pad_lanes(v, width):
    return jnp.pad(v.reshape(1, -1), ((0, 0), (0, width - v.shape[-1])))


def kernel(x_prompt, x_sample, cache_k, cache_v, state_conv, state_ssm, rel_bias, norm1_w, w_in, attn_sinks,
           conv_w, conv_b, dt_bias, A_log, D_skip, ssm_norm_w, w_out, norm2_w, w_gate, w_up, w_down, final_norm_w):
    depth = norm1_w.shape[0]
    assert depth == 1, "single-layer trunk"
    batch, seq, _ = x_prompt.shape
    nb = x_sample.shape[0]
    assert x_sample.shape[1] == 1 and nb == 128

    wi = w_in[0]
    q_c, k_c, v_c, z_c, xbc_c, dt_c = (wi[:, 0:512], wi[:, 512:640], wi[:, 640:768], wi[:, 768:1280],
                                       wi[:, 1280:2304], wi[:, 2304:2312])
    w_perm = jnp.concatenate([q_c, z_c, xbc_c, k_c, v_c, jnp.pad(dt_c, ((0, 0), (0, DT_PAD - SSM_HEADS)))],
                             axis=1).astype(BF16)
    wo = w_out[0].astype(BF16)
    wg = w_gate[0].reshape(D_MODEL, N_FF_CHUNKS, FF_CHUNK).transpose(1, 0, 2).astype(BF16)
    wu = w_up[0].reshape(D_MODEL, N_FF_CHUNKS, FF_CHUNK).transpose(1, 0, 2).astype(BF16)
    wd = w_down[0].reshape(N_FF_CHUNKS, FF_CHUNK, D_MODEL).astype(BF16)
    n1 = norm1_w[0].reshape(1, D_MODEL)
    n2 = norm2_w[0].reshape(1, D_MODEL)
    fn = final_norm_w.reshape(1, D_MODEL)
    cw = conv_w[0]
    cb = conv_b[0].reshape(1, CONV_DIM)
    dtb = _pad_lanes(dt_bias[0], DT_PAD)
    alog = _pad_lanes(A_log[0], DT_PAD)
    dsk = jnp.repeat(D_skip[0], SSM_HEAD_DIM).reshape(1, SSM_WIDTH)
    nw = ssm_norm_w[0].reshape(1, SSM_WIDTH)
    sinks = attn_sinks[0]
    sink_b = jnp.broadcast_to(sinks[:, None], (N_HEADS, WINDOW))

    tab_p, tab_s, tab_n = _bias_tables(rel_bias)

    xp2 = x_prompt.reshape(batch * seq, D_MODEL)
    proj_p = _inproj(xp2, n1, w_perm, 1024)
    a_p = _attn_prompt(proj_p, sinks, tab_p, batch, seq)
    s_p, conv_p, ssm_p = _ssd_prompt(proj_p, cw, cb, dtb, alog, dsk, nw, batch, seq)
    y_p = _tail(xp2, a_p, s_p, wo, n2, wg, wu, wd, fn, 512)
    kv_p = proj_p[:, COL_K:COL_V + N_KV_HEADS * HEAD_DIM].reshape(batch, seq, 2, N_KV_HEADS, HEAD_DIM)[:, -WINDOW:]

    xs2 = x_sample.reshape(nb, D_MODEL)
    proj_s = _inproj(xs2, n1, w_perm, nb)
    q3 = proj_s[:, COL_Q:COL_Q + ATTN_WIDTH].reshape(nb, N_HEADS, HEAD_DIM)
    ck = cache_k[0].reshape(nb, WINDOW, N_KV_HEADS * HEAD_DIM)
    cv = cache_v[0].reshape(nb, WINDOW, N_KV_HEADS * HEAD_DIM)
    a_s3, nk, nv = _attn_sample(q3, proj_s, ck, cv, tab_s, tab_n, sink_b)
    conv_t = jnp.transpose(state_conv[0], (1, 0, 2))
    xs_s, bm_s, cm_s, xdt_t, dec, nconv_t = _ssm_sample_prep(proj_s, conv_t, cw, cb, dtb, alog)
    dec_flat = dec[:, :SSM_HEADS].reshape(nb * SSM_HEADS)
    nssm, s_s = _ssm_sample_state(dec_flat, xdt_t, bm_s, cm_s, xs_s, proj_s, dsk, nw, state_ssm[0])
    y_s = _tail(xs2, a_s3.reshape(nb, ATTN_WIDTH), s_s, wo, n2, wg, wu, wd, fn, nb)

    return (y_p.reshape(batch, seq, D_MODEL),
            y_s.reshape(nb, 1, D_MODEL),
            kv_p[:, :, 0][None],
            kv_p[:, :, 1][None],
            conv_p[None],
            ssm_p[None],
            nk.reshape(1, nb, WINDOW, N_KV_HEADS, HEAD_DIM),
            nv.reshape(1, nb, WINDOW, N_KV_HEADS, HEAD_DIM),
            jnp.transpose(nconv_t, (1, 0, 2))[None],
            nssm[None])
```

```python
import functools
import math

import numpy as np
import jax
import jax.numpy as jnp
from jax import lax
from jax.experimental import pallas as pl
from jax.experimental.pallas import tpu as pltpu

F32 = jnp.float32
BF16 = jnp.bfloat16

D_MODEL = 1024
HEAD_DIM = 64
N_HEADS = 8
N_KV_HEADS = 2
KV_REP = 4
WINDOW = 128
ATTN_WIDTH = 512
ATTN_SCALE = HEAD_DIM ** -0.5
N_BUCKETS = 32
MAX_DISTANCE = 128
SSM_WIDTH = 512
SSM_HEADS = 8
SSM_GROUPS = 2
SSM_REP = 4
SSM_HEAD_DIM = 64
D_STATE = 128
CONV_W = 4
CONV_DIM = 1024
CHUNK = 128
D_FF = 2816
EPS = 1e-6

COL_Q, COL_Z, COL_XBC, COL_K, COL_V, COL_DT = 0, 512, 1024, 2048, 2176, 2304
PROJ_COLS = 2432
DT_PAD = 128

FF_CHUNK = 256
N_FF_CHUNKS = D_FF // FF_CHUNK

VMEM_LIMIT = 56 * 1024 * 1024

_NT = (((1,), (1,)), ((), ()))


def _bucket_table(dist):
    n = np.maximum(dist, 0)
    exact = N_BUCKETS // 2
    nf = np.maximum(n, 1).astype(np.float32)
    large = exact + (np.log(nf / exact) / math.log(MAX_DISTANCE / exact) * (N_BUCKETS - exact)).astype(np.int32)
    return np.where(n < exact, n, np.minimum(large, N_BUCKETS - 1)).astype(np.int32)


def _prompt_buckets():
    i = np.arange(WINDOW)[:, None]
    j = np.arange(WINDOW)[None, :]
    dist = np.where(j > i, i + WINDOW - j, i - j)
    return _bucket_table(dist)


def _sample_buckets():
    dist = WINDOW - np.arange(WINDOW)[None, :]
    band = (dist >= 0) & (dist < WINDOW)
    row = np.where(band, _bucket_table(dist), -1).astype(np.int32)
    return np.tile(row, (N_HEADS, 1))


def _rms(x, w):
    return x * lax.rsqrt(jnp.mean(x * x, axis=-1, keepdims=True) + EPS) * w


def _silu(x):
    return x * (0.5 + 0.5 * jnp.tanh(0.5 * x))


def _softplus(x):
    return jnp.maximum(x, 0.0) + jnp.log1p(jnp.exp(-jnp.abs(x)))


def _const_spec(shape):
    nd = len(shape)
    return pl.BlockSpec(shape, lambda *_: (0,) * nd, pipeline_mode=pl.Buffered(1))


def _bias_kernel(rb_ref, bp_ref, bs_ref, tp_ref, ts_ref, tn_ref):
    bp = bp_ref[...]
    bs = bs_ref[...]
    rowid = lax.broadcasted_iota(jnp.int32, (N_HEADS, WINDOW), 0)
    ts = jnp.zeros((N_HEADS, WINDOW), F32)
    tn = jnp.zeros((N_HEADS, WINDOW), F32)
    for h in range(N_HEADS):
        tp = jnp.zeros((WINDOW, WINDOW), F32)
        for bk in range(N_BUCKETS):
            v = rb_ref[bk, h]
            tp = jnp.where(bp == bk, v, tp)
            ts = jnp.where((bs == bk) & (rowid == h), v, ts)
        tp_ref[h] = tp
        tn = jnp.where(rowid == h, rb_ref[0, h], tn)
    ts_ref[...] = jnp.where(bs < 0, -jnp.inf, ts)
    tn_ref[...] = tn


def _bias_tables(rel_bias):
    return pl.pallas_call(
        _bias_kernel,
        out_shape=(jax.ShapeDtypeStruct((N_HEADS, WINDOW, WINDOW), F32),
                   jax.ShapeDtypeStruct((N_HEADS, WINDOW), F32),
                   jax.ShapeDtypeStruct((N_HEADS, WINDOW), F32)),
        in_specs=[pl.BlockSpec(memory_space=pltpu.SMEM),
                  pl.BlockSpec(memory_space=pltpu.VMEM),
                  pl.BlockSpec(memory_space=pltpu.VMEM)],
        name="bias_tables",
    )(rel_bias, jnp.asarray(_prompt_buckets()), jnp.asarray(_sample_buckets()))


def _inproj_kernel(x_ref, nw_ref, w_ref, o_ref):
    h = _rms(x_ref[...], nw_ref[...]).astype(BF16)
    o_ref[...] = jnp.dot(h, w_ref[...], preferred_element_type=F32)


def _inproj(x2d, norm_w, w_perm, tm):
    n = x2d.shape[0]
    return pl.pallas_call(
        _inproj_kernel,
        out_shape=jax.ShapeDtypeStruct((n, PROJ_COLS), F32),
        grid=(n // tm,),
        in_specs=[pl.BlockSpec((tm, D_MODEL), lambda i: (i, 0)),
                  _const_spec((1, D_MODEL)),
                  _const_spec((D_MODEL, PROJ_COLS))],
        out_specs=pl.BlockSpec((tm, PROJ_COLS), lambda i: (i, 0)),
        compiler_params=pltpu.CompilerParams(dimension_semantics=("arbitrary",),
                                             vmem_limit_bytes=VMEM_LIMIT),
        name="inproj",
    )(x2d, norm_w, w_perm)


def _head_variants(x, low):
    xr = pltpu.roll(x, HEAD_DIM, axis=1)
    zero = jnp.zeros_like(x)
    return ((jnp.where(low, x, zero).astype(BF16), jnp.where(low, zero, xr).astype(BF16)),
            (jnp.where(low, xr, zero).astype(BF16), jnp.where(low, zero, x).astype(BF16)))


def _attn_prompt_kernel(sink_ref, q_ref, k_ref, v_ref, tab_ref, o_ref, kst_ref, vst_ref):
    n = pl.program_id(1)
    w = WINDOW
    row = lax.broadcasted_iota(jnp.int32, (w, w), 0)
    lane = lax.broadcasted_iota(jnp.int32, (w, w), 1)
    low = lane < HEAD_DIM
    upper = lane > row
    no_prev = (lane - row) > jnp.where(n == 0, 0, w)

    @pl.when(n == 0)
    def _():
        kst_ref[...] = jnp.zeros_like(kst_ref)
        vst_ref[...] = jnp.zeros_like(vst_ref)

    @pl.when(n > 0)
    def _():
        for st in (kst_ref, vst_ref):
            for g in range(N_KV_HEADS):
                st[g, 0:w, :] = st[g, w:2 * w, :]
                st[g, 2 * w:3 * w, :] = st[g, 3 * w:4 * w, :]

    for st, ref in ((kst_ref, k_ref), (vst_ref, v_ref)):
        var = _head_variants(ref[...], low)
        for g in range(N_KV_HEADS):
            st[g, w:2 * w, :] = var[g][0]
            st[g, 3 * w:4 * w, :] = var[g][1]

    scores = []
    for pp in range(N_HEADS // 2):
        q2 = q_ref[:, pp * w:(pp + 1) * w].astype(BF16)
        scores.append(lax.dot_general(q2, kst_ref[pp // (KV_REP // 2)], _NT, preferred_element_type=F32))
    p4s = []
    for pp in range(N_HEADS // 2):
        s4 = scores[pp]
        probs = []
        for half in range(2):
            hh = 2 * pp + half
            sp = s4[:, 2 * w * half:2 * w * half + w]
            sc = s4[:, 2 * w * half + w:2 * w * (half + 1)]
            s = jnp.where(upper, sp, sc) * ATTN_SCALE + tab_ref[hh]
            s = jnp.where(no_prev, -jnp.inf, s)
            sk = sink_ref[hh]
            m = jnp.maximum(jnp.max(s, axis=-1, keepdims=True), sk)
            e = jnp.exp(s - m)
            den = jnp.sum(e, axis=-1, keepdims=True) + jnp.exp(sk - m)
            p = e / den
            zero = jnp.zeros_like(p)
            probs.append(jnp.where(upper, p, zero).astype(BF16))
            probs.append(jnp.where(upper, zero, p).astype(BF16))
        p4s.append(jnp.concatenate(probs, axis=1))
    for pp in range(N_HEADS // 2):
        o_ref[:, pp * w:(pp + 1) * w] = jnp.dot(p4s[pp], vst_ref[pp // (KV_REP // 2)], preferred_element_type=F32)


def _attn_prompt(proj, sinks, tab, batch, seq):
    nb = seq // WINDOW
    kcol, vcol = COL_K // WINDOW, COL_V // WINDOW
    return pl.pallas_call(
        _attn_prompt_kernel,
        out_shape=jax.ShapeDtypeStruct((batch * seq, ATTN_WIDTH), F32),
        grid=(batch, nb),
        in_specs=[pl.BlockSpec(memory_space=pltpu.SMEM),
                  pl.BlockSpec((WINDOW, ATTN_WIDTH), lambda b, n: (b * nb + n, 0)),
                  pl.BlockSpec((WINDOW, WINDOW), lambda b, n: (b * nb + n, kcol)),
                  pl.BlockSpec((WINDOW, WINDOW), lambda b, n: (b * nb + n, vcol)),
                  _const_spec((N_HEADS, WINDOW, WINDOW))],
        out_specs=pl.BlockSpec((WINDOW, ATTN_WIDTH), lambda b, n: (b * nb + n, 0)),
        scratch_shapes=[pltpu.VMEM((N_KV_HEADS, 4 * WINDOW, WINDOW), BF16),
                        pltpu.VMEM((N_KV_HEADS, 4 * WINDOW, WINDOW), BF16)],
        compiler_params=pltpu.CompilerParams(dimension_semantics=("arbitrary", "arbitrary"),
                                             vmem_limit_bytes=VMEM_LIMIT),
        name="attn_prompt",
    )(sinks, proj, proj, proj, tab)


def _gated_group_norm(y, z, nw):
    y = y * _silu(z)
    half = SSM_WIDTH // SSM_GROUPS
    parts = []
    for g in range(SSM_GROUPS):
        yg = y[:, g * half:(g + 1) * half]
        parts.append(yg * lax.rsqrt(jnp.mean(yg * yg, axis=-1, keepdims=True) + EPS))
    return jnp.concatenate(parts, axis=-1) * nw


def _ssd_prompt_kernel(z_ref, xbc_ref, dt_ref, cw_ref, cb_ref, dtb_ref, alog_ref, dsk_ref, nw_ref,
                       s_ref, conv_ref, st_ref, hist_ref):
    c = pl.program_id(1)
    last = pl.num_programs(1) - 1
    hd = SSM_HEAD_DIM

    @pl.when(c == 0)
    def _():
        hist_ref[...] = jnp.zeros_like(hist_ref)
        st_ref[...] = jnp.zeros_like(st_ref)

    @pl.when(c == last)
    def _():
        conv_ref[0] = xbc_ref[CHUNK - (CONV_W - 1):, :]

    row = lax.broadcasted_iota(jnp.int32, (CHUNK, CHUNK), 0)
    lane = lax.broadcasted_iota(jnp.int32, (CHUNK, CHUNK), 1)
    causal = row >= lane
    low = lane < hd
    dt_t = _softplus(dt_ref[...].T[0:SSM_HEADS, :] + dtb_ref[...])
    a_t = dt_t * (-jnp.exp(alog_ref[...]))
    cs_t = jnp.dot(a_t, (row <= lane).astype(F32), precision=lax.Precision.HIGHEST,
                   preferred_element_type=F32)
    cs_end = cs_t[:, CHUNK - 1:CHUNK]
    w_t = jnp.exp(cs_end - cs_t) * dt_t
    dec_end = jnp.exp(cs_end)
    col = jnp.concatenate([cs_t, jnp.exp(cs_t), jnp.zeros((CHUNK - 2 * SSM_HEADS, CHUNK), F32)], axis=0).T

    x = xbc_ref[...]
    xx = jnp.concatenate([hist_ref[...], x], axis=0)
    tot = pltpu.roll(xx, CONV_W - 1, axis=0)[8:, :] * cw_ref[0:1, :]
    for j in range(1, CONV_W - 1):
        tot = tot + pltpu.roll(xx, CONV_W - 1 - j, axis=0)[8:, :] * cw_ref[j:j + 1, :]
    tot = tot + x * cw_ref[CONV_W - 1:CONV_W, :]
    xc = _silu(cb_ref[...] + tot)
    hist_ref[...] = x[CHUNK - 8:, :]

    xs = xc[:, :SSM_WIDTH]
    bm = xc[:, SSM_WIDTH:SSM_WIDTH + SSM_GROUPS * D_STATE]
    cm = xc[:, SSM_WIDTH + SSM_GROUPS * D_STATE:]

    groups = range(SSM_GROUPS)
    bgs = [bm[:, g * D_STATE:(g + 1) * D_STATE].astype(BF16) for g in groups]
    cgs = [cm[:, g * D_STATE:(g + 1) * D_STATE].astype(BF16) for g in groups]
    h_prevs = [st_ref[0, g * SSM_REP:(g + 1) * SSM_REP].reshape(SSM_REP * hd, D_STATE) for g in groups]
    cbs = [lax.dot_general(cgs[g], bgs[g], _NT, preferred_element_type=F32) for g in groups]
    y_offs = [lax.dot_general(cgs[g], h_prevs[g].astype(BF16), _NT, preferred_element_type=F32) for g in groups]

    for g in groups:
        xt, wts, decs = [], [], []
        for pr in range(SSM_REP // 2):
            h0 = g * SSM_REP + 2 * pr
            xt.append(xs[:, h0 * hd:(h0 + 2) * hd].T)
            for h in (h0, h0 + 1):
                wts.append(jnp.broadcast_to(w_t[h:h + 1, :], (hd, CHUNK)))
                decs.append(jnp.broadcast_to(dec_end[h:h + 1, :], (hd, D_STATE)))
        xw = (jnp.concatenate(xt, axis=0) * jnp.concatenate(wts, axis=0)).astype(BF16)
        st = jnp.dot(xw, bgs[g], preferred_element_type=F32)
        h_new = h_prevs[g] * jnp.concatenate(decs, axis=0) + st
        st_ref[0, g * SSM_REP:(g + 1) * SSM_REP] = h_new.reshape(SSM_REP, hd, D_STATE)

    ys = []
    for g in groups:
        for pr in range(SSM_REP // 2):
            h0 = g * SSM_REP + 2 * pr
            xpair = xs[:, h0 * hd:(h0 + 2) * hd]
            zero = jnp.zeros_like(xpair)
            m = []
            for h in (h0, h0 + 1):
                decay = jnp.exp(jnp.where(causal, col[:, h:h + 1] - cs_t[h:h + 1, :], -jnp.inf))
                m.append((cbs[g] * decay * dt_t[h:h + 1, :]).astype(BF16))
            y_diag = (jnp.dot(m[0], jnp.where(low, xpair, zero).astype(BF16), preferred_element_type=F32)
                      + jnp.dot(m[1], jnp.where(low, zero, xpair).astype(BF16), preferred_element_type=F32))
            e0 = jnp.broadcast_to(col[:, SSM_HEADS + h0:SSM_HEADS + h0 + 1], (CHUNK, CHUNK))
            e1 = jnp.broadcast_to(col[:, SSM_HEADS + h0 + 1:SSM_HEADS + h0 + 2], (CHUNK, CHUNK))
            ys.append(y_diag + y_offs[g][:, 2 * pr * hd:(2 * pr + 2) * hd] * jnp.where(low, e0, e1))
    y = jnp.concatenate(ys, axis=-1) + xs * dsk_ref[...]
    s_ref[...] = _gated_group_norm(y, z_ref[...], nw_ref[...])


def _ssd_prompt(proj, conv_w, conv_b, dtb, alog, dsk, nw, batch, seq):
    nc = seq // CHUNK
    return pl.pallas_call(
        _ssd_prompt_kernel,
        out_shape=(jax.ShapeDtypeStruct((batch * seq, SSM_WIDTH), F32),
                   jax.ShapeDtypeStruct((batch, CONV_W - 1, CONV_DIM), F32),
                   jax.ShapeDtypeStruct((batch, SSM_HEADS, SSM_HEAD_DIM, D_STATE), F32)),
        grid=(batch, nc),
        in_specs=[pl.BlockSpec((CHUNK, SSM_WIDTH), lambda b, c: (b * nc + c, COL_Z // SSM_WIDTH)),
                  pl.BlockSpec((CHUNK, CONV_DIM), lambda b, c: (b * nc + c, COL_XBC // CONV_DIM)),
                  pl.BlockSpec((CHUNK, DT_PAD), lambda b, c: (b * nc + c, COL_DT // DT_PAD)),
                  _const_spec((CONV_W, CONV_DIM)),
                  _const_spec((1, CONV_DIM)),
                  _const_spec((SSM_HEADS, CHUNK)),
                  _const_spec((SSM_HEADS, CHUNK)),
                  _const_spec((1, SSM_WIDTH)),
                  _const_spec((1, SSM_WIDTH))],
        out_specs=(pl.BlockSpec((CHUNK, SSM_WIDTH), lambda b, c: (b * nc + c, 0)),
                   pl.BlockSpec((1, CONV_W - 1, CONV_DIM), lambda b, c: (b, 0, 0)),
                   pl.BlockSpec((1, SSM_HEADS, SSM_HEAD_DIM, D_STATE), lambda b, c: (b, 0, 0, 0))),
        scratch_shapes=[pltpu.VMEM((8, CONV_DIM), F32)],
        compiler_params=pltpu.CompilerParams(dimension_semantics=("arbitrary", "arbitrary"),
                                             vmem_limit_bytes=VMEM_LIMIT),
        name="ssd_prompt",
    )(proj, proj, proj, conv_w, conv_b, dtb, alog, dsk, nw)


def _tail_kernel(x_ref, a_ref, s_ref, wo_ref, n2_ref, wg_ref, wu_ref, wd_ref, fn_ref, o_ref):
    x1 = (x_ref[...]
          + jnp.dot(a_ref[...].astype(BF16), wo_ref[0:ATTN_WIDTH, :], preferred_element_type=F32)
          + jnp.dot(s_ref[...].astype(BF16), wo_ref[ATTN_WIDTH:, :], preferred_element_type=F32))
    h2 = _rms(x1, n2_ref[...]).astype(BF16)
    acc = None
    for j in range(N_FF_CHUNKS):
        gate = jnp.dot(h2, wg_ref[j], preferred_element_type=F32)
        up = jnp.dot(h2, wu_ref[j], preferred_element_type=F32)
        act = (_silu(gate) * up).astype(BF16)
        part = jnp.dot(act, wd_ref[j], preferred_element_type=F32)
        acc = part if acc is None else acc + part
    o_ref[...] = _rms(x1 + acc, fn_ref[...])


def _tail(x2d, a, s, wo, n2, wg, wu, wd, fn, tm):
    n = x2d.shape[0]
    return pl.pallas_call(
        _tail_kernel,
        out_shape=jax.ShapeDtypeStruct((n, D_MODEL), F32),
        grid=(n // tm,),
        in_specs=[pl.BlockSpec((tm, D_MODEL), lambda i: (i, 0)),
                  pl.BlockSpec((tm, ATTN_WIDTH), lambda i: (i, 0)),
                  pl.BlockSpec((tm, SSM_WIDTH), lambda i: (i, 0)),
                  _const_spec((D_MODEL, D_MODEL)),
                  _const_spec((1, D_MODEL)),
                  _const_spec((N_FF_CHUNKS, D_MODEL, FF_CHUNK)),
                  _const_spec((N_FF_CHUNKS, D_MODEL, FF_CHUNK)),
                  _const_spec((N_FF_CHUNKS, FF_CHUNK, D_MODEL)),
                  _const_spec((1, D_MODEL))],
        out_specs=pl.BlockSpec((tm, D_MODEL), lambda i: (i, 0)),
        compiler_params=pltpu.CompilerParams(dimension_semantics=("arbitrary",),
                                             vmem_limit_bytes=VMEM_LIMIT),
        name="outproj_ffn",
    )(x2d, a, s, wo, n2, wg, wu, wd, fn)


SAMPLE_BT = 8


def _attn_sample_kernel(q_ref, kn_ref, vn_ref, ck_ref, cv_ref, ts_ref, tn_ref, sink_ref,
                        a_ref, nk_ref, nv_ref):
    rowid = lax.broadcasted_iota(jnp.int32, (N_HEADS, HEAD_DIM), 0)
    ts = ts_ref[...]
    tn = tn_ref[:, 0:1]
    sink = sink_ref[:, 0:1]
    for bb in range(SAMPLE_BT):
        qb = q_ref[bb]
        qbd = jnp.concatenate([jnp.where(rowid < KV_REP, qb, 0.0), jnp.where(rowid >= KV_REP, qb, 0.0)], axis=1)
        kn = kn_ref[bb:bb + 1, :]
        vn = vn_ref[bb:bb + 1, :]
        s_c = lax.dot_general(qbd.astype(BF16), ck_ref[bb].astype(BF16), _NT,
                              preferred_element_type=F32) * ATTN_SCALE + ts
        s_n = jnp.sum(qbd * kn, axis=-1, keepdims=True) * ATTN_SCALE + tn
        m = jnp.maximum(jnp.maximum(jnp.max(s_c, axis=-1, keepdims=True), s_n), sink)
        e_c = jnp.exp(s_c - m)
        e_n = jnp.exp(s_n - m)
        den = jnp.sum(e_c, axis=-1, keepdims=True) + e_n + jnp.exp(sink - m)
        o = jnp.dot((e_c / den).astype(BF16), cv_ref[bb].astype(BF16), preferred_element_type=F32) + (e_n / den) * vn
        a_ref[bb] = jnp.where(rowid < KV_REP, o[:, :HEAD_DIM], o[:, HEAD_DIM:])
        nk_ref[bb, 0:WINDOW - 1, :] = ck_ref[bb, 1:WINDOW, :]
        nk_ref[bb, WINDOW - 1:WINDOW, :] = kn
        nv_ref[bb, 0:WINDOW - 1, :] = cv_ref[bb, 1:WINDOW, :]
        nv_ref[bb, WINDOW - 1:WINDOW, :] = vn


def _attn_sample(q3, proj, ck, cv, ts, tn, sink_b):
    nb = q3.shape[0]
    bt = SAMPLE_BT
    kv = N_KV_HEADS * HEAD_DIM
    return pl.pallas_call(
        _attn_sample_kernel,
        out_shape=(jax.ShapeDtypeStruct((nb, N_HEADS, HEAD_DIM), F32),
                   jax.ShapeDtypeStruct((nb, WINDOW, kv), F32),
                   jax.ShapeDtypeStruct((nb, WINDOW, kv), F32)),
        grid=(nb // bt,),
        in_specs=[pl.BlockSpec((bt, N_HEADS, HEAD_DIM), lambda i: (i, 0, 0)),
                  pl.BlockSpec((bt, kv), lambda i: (i, COL_K // kv)),
                  pl.BlockSpec((bt, kv), lambda i: (i, COL_V // kv)),
                  pl.BlockSpec((bt, WINDOW, kv), lambda i: (i, 0, 0)),
                  pl.BlockSpec((bt, WINDOW, kv), lambda i: (i, 0, 0)),
                  _const_spec((N_HEADS, WINDOW)),
                  _const_spec((N_HEADS, WINDOW)),
                  _const_spec((N_HEADS, WINDOW))],
        out_specs=(pl.BlockSpec((bt, N_HEADS, HEAD_DIM), lambda i: (i, 0, 0)),
                   pl.BlockSpec((bt, WINDOW, kv), lambda i: (i, 0, 0)),
                   pl.BlockSpec((bt, WINDOW, kv), lambda i: (i, 0, 0))),
        compiler_params=pltpu.CompilerParams(dimension_semantics=("arbitrary",),
                                             vmem_limit_bytes=VMEM_LIMIT),
        name="attn_sample",
    )(q3, proj, proj, ck, cv, ts, tn, sink_b)


def _ssm_sample_prep_kernel(xbc_ref, dt_ref, cst_ref, cw_ref, cb_ref, dtb_ref, alog_ref,
                            xs_ref, bm_ref, cm_ref, xdt_t_ref, dec_ref, nc_ref):
    xbc = xbc_ref[...]
    tot = cst_ref[0] * cw_ref[0:1, :]
    tot = tot + cst_ref[1] * cw_ref[1:2, :]
    tot = tot + cst_ref[2] * cw_ref[2:3, :]
    tot = tot + xbc * cw_ref[3:4, :]
    xc = _silu(cb_ref[...] + tot)
    nc_ref[0] = cst_ref[1]
    nc_ref[1] = cst_ref[2]
    nc_ref[2] = xbc
    xs = xc[:, :SSM_WIDTH]
    xs_ref[...] = xs
    bm_ref[...] = xc[:, SSM_WIDTH:SSM_WIDTH + SSM_GROUPS * D_STATE]
    cm_ref[...] = xc[:, SSM_WIDTH + SSM_GROUPS * D_STATE:]
    dt = _softplus(dt_ref[...] + dtb_ref[...])
    dec_ref[...] = jnp.exp(dt * (-jnp.exp(alog_ref[...])))
    dt_t = dt.T
    for k in range(SSM_WIDTH // 128):
        xt = xs[:, k * 128:(k + 1) * 128].T
        for half in range(2):
            h = 2 * k + half
            lo = h * SSM_HEAD_DIM
            xdt_t_ref[lo:lo + SSM_HEAD_DIM, :] = xt[half * SSM_HEAD_DIM:(half + 1) * SSM_HEAD_DIM, :] * dt_t[h:h + 1, :]


def _ssm_sample_prep(proj, conv_state_t, conv_w, conv_b, dtb, alog):
    nb = proj.shape[0]
    return pl.pallas_call(
        _ssm_sample_prep_kernel,
        out_shape=(jax.ShapeDtypeStruct((nb, SSM_WIDTH), F32),
                   jax.ShapeDtypeStruct((nb, SSM_GROUPS * D_STATE), F32),
                   jax.ShapeDtypeStruct((nb, SSM_GROUPS * D_STATE), F32),
                   jax.ShapeDtypeStruct((SSM_WIDTH, nb), F32),
                   jax.ShapeDtypeStruct((nb, DT_PAD), F32),
                   jax.ShapeDtypeStruct((CONV_W - 1, nb, CONV_DIM), F32)),
        grid=(1,),
        in_specs=[pl.BlockSpec((nb, CONV_DIM), lambda i: (0, COL_XBC // CONV_DIM)),
                  pl.BlockSpec((nb, DT_PAD), lambda i: (0, COL_DT // DT_PAD)),
                  _const_spec((CONV_W - 1, nb, CONV_DIM)),
                  _const_spec((CONV_W, CONV_DIM)),
                  _const_spec((1, CONV_DIM)),
                  _const_spec((1, DT_PAD)),
                  _const_spec((1, DT_PAD))],
        out_specs=(_const_spec((nb, SSM_WIDTH)),
                   _const_spec((nb, SSM_GROUPS * D_STATE)),
                   _const_spec((nb, SSM_GROUPS * D_STATE)),
                   _const_spec((SSM_WIDTH, nb)),
                   _const_spec((nb, DT_PAD)),
                   _const_spec((CONV_W - 1, nb, CONV_DIM))),
        compiler_params=pltpu.CompilerParams(dimension_semantics=("arbitrary",),
                                             vmem_limit_bytes=VMEM_LIMIT),
        name="ssm_sample_prep",
    )(proj, proj, conv_state_t, conv_w, conv_b, dtb, alog)


def _ssm_sample_state_kernel(dec_ref, xdt_t_ref, bm_ref, cm_ref, xs_ref, z_ref, dsk_ref, nw_ref, st_ref,
                             nst_ref, s_ref, yt_ref):
    i = pl.program_id(0)
    last = pl.num_programs(0) - 1
    rows = SSM_REP * SSM_HEAD_DIM

    @pl.when(i == 0)
    def _():
        yt_ref[...] = jnp.zeros_like(yt_ref)

    lane = lax.broadcasted_iota(jnp.int32, (rows, 128), 1)
    for bb in range(SAMPLE_BT):
        b = i * SAMPLE_BT + bb
        mine = lane == b
        for g in range(SSM_GROUPS):
            bg = bm_ref[:, g * D_STATE:(g + 1) * D_STATE].astype(BF16)
            cg = cm_ref[:, g * D_STATE:(g + 1) * D_STATE].astype(BF16)
            xsel = jnp.where(mine, xdt_t_ref[g * rows:(g + 1) * rows, :], 0.0).astype(BF16)
            outer = jnp.dot(xsel, bg, preferred_element_type=F32)
            new = []
            for r in range(SSM_REP):
                h = g * SSM_REP + r
                hn = st_ref[bb, h] * dec_ref[b * SSM_HEADS + h] + outer[r * SSM_HEAD_DIM:(r + 1) * SSM_HEAD_DIM, :]
                nst_ref[bb, h] = hn
                new.append(hn)
            hcat = jnp.concatenate(new, axis=0).astype(BF16)
            res = lax.dot_general(hcat, cg, _NT, preferred_element_type=F32)
            yt_ref[g * rows:(g + 1) * rows, :] += jnp.where(mine, res, 0.0)

    @pl.when(i == last)
    def _():
        y = jnp.concatenate([yt_ref[k * 128:(k + 1) * 128, :].T for k in range(SSM_WIDTH // 128)], axis=1)
        y = y + xs_ref[...] * dsk_ref[...]
        s_ref[...] = _gated_group_norm(y, z_ref[...], nw_ref[...])


def _ssm_sample_state(dec_flat, xdt_t, bm, cm, xs, proj, dsk, nw, state):
    nb = state.shape[0]
    bt = SAMPLE_BT
    return pl.pallas_call(
        _ssm_sample_state_kernel,
        out_shape=(jax.ShapeDtypeStruct(state.shape, F32),
                   jax.ShapeDtypeStruct((nb, SSM_WIDTH), F32)),
        grid=(nb // bt,),
        in_specs=[pl.BlockSpec(memory_space=pltpu.SMEM),
                  _const_spec((SSM_WIDTH, nb)),
                  _const_spec((nb, SSM_GROUPS * D_STATE)),
                  _const_spec((nb, SSM_GROUPS * D_STATE)),
                  _const_spec((nb, SSM_WIDTH)),
                  pl.BlockSpec((nb, SSM_WIDTH), lambda i: (0, COL_Z // SSM_WIDTH)),
                  _const_spec((1, SSM_WIDTH)),
                  _const_spec((1, SSM_WIDTH)),
                  pl.BlockSpec((bt, SSM_HEADS, SSM_HEAD_DIM, D_STATE), lambda i: (i, 0, 0, 0))],
        out_specs=(pl.BlockSpec((bt, SSM_HEADS, SSM_HEAD_DIM, D_STATE), lambda i: (i, 0, 0, 0)),
                   pl.BlockSpec((nb, SSM_WIDTH), lambda i: (0, 0))),
        scratch_shapes=[pltpu.VMEM((SSM_WIDTH, nb), F32)],
        compiler_params=pltpu.CompilerParams(dimension_semantics=("arbitrary",),
                                             vmem_limit_bytes=VMEM_LIMIT),
        name="ssm_sample_state",
    )(dec_flat, xdt_t, bm, cm, xs, proj, dsk, nw, state)


def _pad_lanes(v, width):
    return jnp.pad(v.reshape(1, -1), ((0, 0), (0, width - v.shape[-1])))


def kernel(x_prompt, x_sample, cache_k, cache_v, state_conv, state_ssm, rel_bias, norm1_w, w_in, attn_sinks,
           conv_w, conv_b, dt_bias, A_log, D_skip, ssm_norm_w, w_out, norm2_w, w_gate, w_up, w_down, final_norm_w):
    depth = norm1_w.shape[0]
    assert depth == 1, "single-layer trunk"
    batch, seq, _ = x_prompt.shape
    nb = x_sample.shape[0]
    assert x_sample.shape[1] == 1 and nb == 128

    wi = w_in[0]
    q_c, k_c, v_c, z_c, xbc_c, dt_c = (wi[:, 0:512], wi[:, 512:640], wi[:, 640:768], wi[:, 768:1280],
                                       wi[:, 1280:2304], wi[:, 2304:2312])
    w_perm = jnp.concatenate([q_c, z_c, xbc_c, k_c, v_c, jnp.pad(dt_c, ((0, 0), (0, DT_PAD - SSM_HEADS)))],
                             axis=1).astype(BF16)
    wo = w_out[0].astype(BF16)
    wg = w_gate[0].reshape(D_MODEL, N_FF_CHUNKS, FF_CHUNK).transpose(1, 0, 2).astype(BF16)
    wu = w_up[0].reshape(D_MODEL, N_FF_CHUNKS, FF_CHUNK).transpose(1, 0, 2).astype(BF16)
    wd = w_down[0].reshape(N_FF_CHUNKS, FF_CHUNK, D_MODEL).astype(BF16)
    n1 = norm1_w[0].reshape(1, D_MODEL)
    n2 = norm2_w[0].reshape(1, D_MODEL)
    fn = final_norm_w.reshape(1, D_MODEL)
    cw = conv_w[0]
    cb = conv_b[0].reshape(1, CONV_DIM)
    dtb = _pad_lanes(dt_bias[0], DT_PAD)
    alog = _pad_lanes(A_log[0], DT_PAD)
    dtb_t = jnp.broadcast_to(dt_bias[0][:, None], (SSM_HEADS, CHUNK))
    alog_t = jnp.broadcast_to(A_log[0][:, None], (SSM_HEADS, CHUNK))
    dsk = jnp.repeat(D_skip[0], SSM_HEAD_DIM).reshape(1, SSM_WIDTH)
    nw = ssm_norm_w[0].reshape(1, SSM_WIDTH)
    sinks = attn_sinks[0]
    sink_b = jnp.broadcast_to(sinks[:, None], (N_HEADS, WINDOW))

    tab_p, tab_s, tab_n = _bias_tables(rel_bias)

    xp2 = x_prompt.reshape(batch * seq, D_MODEL)
    proj_p = _inproj(xp2, n1, w_perm, 1024)
    a_p = _attn_prompt(proj_p, sinks, tab_p, batch, seq)
    s_p, conv_p, ssm_p = _ssd_prompt(proj_p, cw, cb, dtb_t, alog_t, dsk, nw, batch, seq)
    y_p = _tail(xp2, a_p, s_p, wo, n2, wg, wu, wd, fn, 512)
    kv_p = proj_p[:, COL_K:COL_V + N_KV_HEADS * HEAD_DIM].reshape(batch, seq, 2, N_KV_HEADS, HEAD_DIM)[:, -WINDOW:]

    xs2 = x_sample.reshape(nb, D_MODEL)
    proj_s = _inproj(xs2, n1, w_perm, nb)
    q3 = proj_s[:, COL_Q:COL_Q + ATTN_WIDTH].reshape(nb, N_HEADS, HEAD_DIM)
    ck = cache_k[0].reshape(nb, WINDOW, N_KV_HEADS * HEAD_DIM)
    cv = cache_v[0].reshape(nb, WINDOW, N_KV_HEADS * HEAD_DIM)
    a_s3, nk, nv = _attn_sample(q3, proj_s, ck, cv, tab_s, tab_n, sink_b)
    conv_t = jnp.transpose(state_conv[0], (1, 0, 2))
    xs_s, bm_s, cm_s, xdt_t, dec, nconv_t = _ssm_sample_prep(proj_s, conv_t, cw, cb, dtb, alog)
    dec_flat = dec[:, :SSM_HEADS].reshape(nb * SSM_HEADS)
    nssm, s_s = _ssm_sample_state(dec_flat, xdt_t, bm_s, cm_s, xs_s, proj_s, dsk, nw, state_ssm[0])
    y_s = _tail(xs2, a_s3.reshape(nb, ATTN_WIDTH), s_s, wo, n2, wg, wu, wd, fn, nb)

    return (y_p.reshape(batch, seq, D_MODEL),
            y_s.reshape(nb, 1, D_MODEL),
            kv_p[:, :, 0][None],
            kv_p[:, :, 1][None],
            conv_p[None],
            ssm_p[None],
            nk.reshape(1, nb, WINDOW, N_KV_HEADS, HEAD_DIM),
            nv.reshape(1, nb, WINDOW, N_KV_HEADS, HEAD_DIM),
            jnp.transpose(nconv_t, (1, 0, 2))[None],
            nssm[None])
```

```python
import functools
import math

import numpy as np
import jax
import jax.numpy as jnp
from jax import lax
from jax.experimental import pallas as pl
from jax.experimental.pallas import tpu as pltpu

F32 = jnp.float32
BF16 = jnp.bfloat16

D_MODEL = 1024
HEAD_DIM = 64
N_HEADS = 8
N_KV_HEADS = 2
KV_REP = 4
WINDOW = 128
ATTN_WIDTH = 512
ATTN_SCALE = HEAD_DIM ** -0.5
N_BUCKETS = 32
MAX_DISTANCE = 128
SSM_WIDTH = 512
SSM_HEADS = 8
SSM_GROUPS = 2
SSM_REP = 4
SSM_HEAD_DIM = 64
D_STATE = 128
CONV_W = 4
CONV_DIM = 1024
CHUNK = 128
D_FF = 2816
EPS = 1e-6

COL_Q, COL_Z, COL_XBC, COL_K, COL_V, COL_DT = 0, 512, 1024, 2048, 2176, 2304
PROJ_COLS = 2432
DT_PAD = 128
KV_COLS = 2 * N_KV_HEADS * HEAD_DIM

FF_CHUNK = 256
N_FF_CHUNKS = D_FF // FF_CHUNK

MIX_TILE = 512
MIX_BLOCKS = MIX_TILE // CHUNK
PROJ_PIECES = ((0, 512), (512, 1024), (1024, 1536), (1536, 2048), (2048, PROJ_COLS))

TAIL_TILE = 512

VMEM_LIMIT = 56 * 1024 * 1024

_NT = (((1,), (1,)), ((), ()))


def _bucket_table(dist):
    n = np.maximum(dist, 0)
    exact = N_BUCKETS // 2
    nf = np.maximum(n, 1).astype(np.float32)
    large = exact + (np.log(nf / exact) / math.log(MAX_DISTANCE / exact) * (N_BUCKETS - exact)).astype(np.int32)
    return np.where(n < exact, n, np.minimum(large, N_BUCKETS - 1)).astype(np.int32)


def _prompt_buckets():
    i = np.arange(WINDOW)[:, None]
    j = np.arange(WINDOW)[None, :]
    dist = np.where(j > i, i + WINDOW - j, i - j)
    return _bucket_table(dist)


def _sample_buckets():
    dist = WINDOW - np.arange(WINDOW)[None, :]
    band = (dist >= 0) & (dist < WINDOW)
    row = np.where(band, _bucket_table(dist), -1).astype(np.int32)
    return np.tile(row, (N_HEADS, 1))


def _rms(x, w):
    return x * lax.rsqrt(jnp.mean(x * x, axis=-1, keepdims=True) + EPS) * w


def _silu(x):
    return x * (0.5 + 0.5 * jnp.tanh(0.5 * x))


def _softplus(x):
    return jnp.maximum(x, 0.0) + jnp.log1p(jnp.exp(-jnp.abs(x)))


def _const_spec(shape):
    nd = len(shape)
    return pl.BlockSpec(shape, lambda *_: (0,) * nd, pipeline_mode=pl.Buffered(1))


def _bias_kernel(rb_ref, bp_ref, bs_ref, tp_ref, ts_ref, tn_ref):
    bp = bp_ref[...]
    bs = bs_ref[...]
    rowid = lax.broadcasted_iota(jnp.int32, (N_HEADS, WINDOW), 0)
    ts = jnp.zeros((N_HEADS, WINDOW), F32)
    tn = jnp.zeros((N_HEADS, WINDOW), F32)
    for h in range(N_HEADS):
        tp = jnp.zeros((WINDOW, WINDOW), F32)
        for bk in range(N_BUCKETS):
            v = rb_ref[bk, h]
            tp = jnp.where(bp == bk, v, tp)
            ts = jnp.where((bs == bk) & (rowid == h), v, ts)
        tp_ref[h] = tp
        tn = jnp.where(rowid == h, rb_ref[0, h], tn)
    ts_ref[...] = jnp.where(bs < 0, -jnp.inf, ts)
    tn_ref[...] = tn


def _bias_tables(rel_bias):
    return pl.pallas_call(
        _bias_kernel,
        out_shape=(jax.ShapeDtypeStruct((N_HEADS, WINDOW, WINDOW), F32),
                   jax.ShapeDtypeStruct((N_HEADS, WINDOW), F32),
                   jax.ShapeDtypeStruct((N_HEADS, WINDOW), F32)),
        in_specs=[pl.BlockSpec(memory_space=pltpu.SMEM),
                  pl.BlockSpec(memory_space=pltpu.VMEM),
                  pl.BlockSpec(memory_space=pltpu.VMEM)],
        name="bias_tables",
    )(rel_bias, jnp.asarray(_prompt_buckets()), jnp.asarray(_sample_buckets()))


def _inproj_kernel(x_ref, nw_ref, w_ref, o_ref):
    h = _rms(x_ref[...], nw_ref[...]).astype(BF16)
    o_ref[...] = jnp.dot(h, w_ref[...], preferred_element_type=F32)


def _inproj(x2d, norm_w, w_perm, tm):
    n = x2d.shape[0]
    return pl.pallas_call(
        _inproj_kernel,
        out_shape=jax.ShapeDtypeStruct((n, PROJ_COLS), F32),
        grid=(n // tm,),
        in_specs=[pl.BlockSpec((tm, D_MODEL), lambda i: (i, 0)),
                  _const_spec((1, D_MODEL)),
                  _const_spec((D_MODEL, PROJ_COLS))],
        out_specs=pl.BlockSpec((tm, PROJ_COLS), lambda i: (i, 0)),
        compiler_params=pltpu.CompilerParams(dimension_semantics=("arbitrary",),
                                             vmem_limit_bytes=VMEM_LIMIT),
        name="inproj",
    )(x2d, norm_w, w_perm)


def _head_variants(x, low):
    xr = pltpu.roll(x, HEAD_DIM, axis=1)
    zero = jnp.zeros_like(x)
    return ((jnp.where(low, x, zero).astype(BF16), jnp.where(low, zero, xr).astype(BF16)),
            (jnp.where(low, xr, zero).astype(BF16), jnp.where(low, zero, x).astype(BF16)))


def _gated_group_norm(y, z, nw):
    y = y * _silu(z)
    half = SSM_WIDTH // SSM_GROUPS
    parts = []
    for g in range(SSM_GROUPS):
        yg = y[:, g * half:(g + 1) * half]
        parts.append(yg * lax.rsqrt(jnp.mean(yg * yg, axis=-1, keepdims=True) + EPS))
    return jnp.concatenate(parts, axis=-1) * nw


def _attn_scores(pr, r0, kst_ref, vst_ref):
    w = WINDOW
    rows = slice(r0, r0 + w)
    low = lax.broadcasted_iota(jnp.int32, (w, w), 1) < HEAD_DIM

    for st, c0 in ((kst_ref, COL_K), (vst_ref, COL_V)):
        for g in range(N_KV_HEADS):
            st[g, 0:w, :] = st[g, w:2 * w, :]
            st[g, 2 * w:3 * w, :] = st[g, 3 * w:4 * w, :]
        var = _head_variants(pr[rows, c0:c0 + w], low)
        for g in range(N_KV_HEADS):
            st[g, w:2 * w, :] = var[g][0]
            st[g, 3 * w:4 * w, :] = var[g][1]

    scores = []
    for pp in range(N_HEADS // 2):
        q2 = pr[rows, COL_Q + pp * w:COL_Q + (pp + 1) * w].astype(BF16)
        scores.append(lax.dot_general(q2, kst_ref[pp // (KV_REP // 2)], _NT, preferred_element_type=F32))
    return scores


def _attn_softmax(scores, seq_start, sink_ref, tab_ref):
    w = WINDOW
    row = lax.broadcasted_iota(jnp.int32, (w, w), 0)
    lane = lax.broadcasted_iota(jnp.int32, (w, w), 1)
    upper = lane > row
    if seq_start is not None:
        no_prev = (lane - row) > jnp.where(seq_start, 0, w)
    p4s = []
    for pp in range(N_HEADS // 2):
        s4 = scores[pp]
        probs = []
        for half in range(2):
            hh = 2 * pp + half
            sp = s4[:, 2 * w * half:2 * w * half + w]
            sc = s4[:, 2 * w * half + w:2 * w * (half + 1)]
            s = jnp.where(upper, sp, sc) * ATTN_SCALE + tab_ref[hh]
            if seq_start is not None:
                s = jnp.where(no_prev, -jnp.inf, s)
            sk = sink_ref[hh]
            m = jnp.maximum(jnp.max(s, axis=-1, keepdims=True), sk)
            e = jnp.exp(s - m)
            den = jnp.sum(e, axis=-1, keepdims=True) + jnp.exp(sk - m)
            p = e / den
            zero = jnp.zeros_like(p)
            probs.append(jnp.where(upper, p, zero).astype(BF16))
            probs.append(jnp.where(upper, zero, p).astype(BF16))
        p4s.append(jnp.concatenate(probs, axis=1))
    return p4s


def _attn_values(p4s, r0, vst_ref, out_ref):
    w = WINDOW
    for pp in range(N_HEADS // 2):
        out_ref[r0:r0 + w, pp * w:(pp + 1) * w] = jnp.dot(p4s[pp], vst_ref[pp // (KV_REP // 2)],
                                                          preferred_element_type=F32)


def _ssd_decay(pr, r0, dtb_ref, alog_ref):
    rows = slice(r0, r0 + CHUNK)
    row = lax.broadcasted_iota(jnp.int32, (CHUNK, CHUNK), 0)
    lane = lax.broadcasted_iota(jnp.int32, (CHUNK, CHUNK), 1)
    dt_t = _softplus(pr[rows, COL_DT:COL_DT + DT_PAD].T[0:SSM_HEADS, :] + dtb_ref[...])
    a_t = dt_t * (-jnp.exp(alog_ref[...]))
    cs_t = jnp.dot(a_t, (row <= lane).astype(F32), precision=lax.Precision.HIGHEST,
                   preferred_element_type=F32)
    cs_end = cs_t[:, CHUNK - 1:CHUNK]
    w_t = jnp.exp(cs_end - cs_t) * dt_t
    dec_end = jnp.exp(cs_end)
    col = jnp.concatenate([cs_t, jnp.exp(cs_t), jnp.zeros((CHUNK - 2 * SSM_HEADS, CHUNK), F32)], axis=0).T
    return dt_t, cs_t, w_t, dec_end, col


def _ssd_conv(pr, r0, seq_start, cw_ref, cb_ref, hist_ref):
    x = pr[r0:r0 + CHUNK, COL_XBC:COL_XBC + CONV_DIM]
    hist = hist_ref[...]
    if seq_start is not None:
        hist = jnp.where(seq_start, 0.0, hist)
    xx = jnp.concatenate([hist, x], axis=0)
    tot = pltpu.roll(xx, CONV_W - 1, axis=0)[8:, :] * cw_ref[0:1, :]
    for j in range(1, CONV_W - 1):
        tot = tot + pltpu.roll(xx, CONV_W - 1 - j, axis=0)[8:, :] * cw_ref[j:j + 1, :]
    tot = tot + x * cw_ref[CONV_W - 1:CONV_W, :]
    hist_ref[...] = x[CHUNK - 8:, :]
    return _silu(cb_ref[...] + tot)


def _ssd_matmuls(pr, r0, seq_start, decay_terms, xc, dsk_ref, nw_ref, st_ref, out_ref):
    hd = SSM_HEAD_DIM
    rows = slice(r0, r0 + CHUNK)
    dt_t, cs_t, w_t, dec_end, col = decay_terms
    row = lax.broadcasted_iota(jnp.int32, (CHUNK, CHUNK), 0)
    lane = lax.broadcasted_iota(jnp.int32, (CHUNK, CHUNK), 1)
    causal = row >= lane
    low = lane < hd
    xs = xc[:, :SSM_WIDTH]
    bm = xc[:, SSM_WIDTH:SSM_WIDTH + SSM_GROUPS * D_STATE]
    cm = xc[:, SSM_WIDTH + SSM_GROUPS * D_STATE:]

    groups = range(SSM_GROUPS)
    bgs = [bm[:, g * D_STATE:(g + 1) * D_STATE].astype(BF16) for g in groups]
    cgs = [cm[:, g * D_STATE:(g + 1) * D_STATE].astype(BF16) for g in groups]
    h_prevs = []
    for g in groups:
        hp = st_ref[g * SSM_REP:(g + 1) * SSM_REP].reshape(SSM_REP * hd, D_STATE)
        if seq_start is not None:
            hp = jnp.where(seq_start, 0.0, hp)
        h_prevs.append(hp)
    cbs = [lax.dot_general(cgs[g], bgs[g], _NT, preferred_element_type=F32) for g in groups]
    y_offs = [lax.dot_general(cgs[g], h_prevs[g].astype(BF16), _NT, preferred_element_type=F32) for g in groups]

    for g in groups:
        xt, wts, decs = [], [], []
        for pr2 in range(SSM_REP // 2):
            h0 = g * SSM_REP + 2 * pr2
            xt.append(xs[:, h0 * hd:(h0 + 2) * hd].T)
            for h in (h0, h0 + 1):
                wts.append(jnp.broadcast_to(w_t[h:h + 1, :], (hd, CHUNK)))
                decs.append(jnp.broadcast_to(dec_end[h:h + 1, :], (hd, D_STATE)))
        xw = (jnp.concatenate(xt, axis=0) * jnp.concatenate(wts, axis=0)).astype(BF16)
        st = jnp.dot(xw, bgs[g], preferred_element_type=F32)
        h_new = h_prevs[g] * jnp.concatenate(decs, axis=0) + st
        st_ref[g * SSM_REP:(g + 1) * SSM_REP] = h_new.reshape(SSM_REP, hd, D_STATE)

    ys = []
    for g in groups:
        for pr2 in range(SSM_REP // 2):
            h0 = g * SSM_REP + 2 * pr2
            xpair = xs[:, h0 * hd:(h0 + 2) * hd]
            zero = jnp.zeros_like(xpair)
            m = []
            for h in (h0, h0 + 1):
                decay = jnp.exp(jnp.where(causal, col[:, h:h + 1] - cs_t[h:h + 1, :], -jnp.inf))
                m.append((cbs[g] * decay * dt_t[h:h + 1, :]).astype(BF16))
            y_diag = (jnp.dot(m[0], jnp.where(low, xpair, zero).astype(BF16), preferred_element_type=F32)
                      + jnp.dot(m[1], jnp.where(low, zero, xpair).astype(BF16), preferred_element_type=F32))
            e0 = jnp.broadcast_to(col[:, SSM_HEADS + h0:SSM_HEADS + h0 + 1], (CHUNK, CHUNK))
            e1 = jnp.broadcast_to(col[:, SSM_HEADS + h0 + 1:SSM_HEADS + h0 + 2], (CHUNK, CHUNK))
            ys.append(y_diag + y_offs[g][:, 2 * pr2 * hd:(2 * pr2 + 2) * hd] * jnp.where(low, e0, e1))
    y = jnp.concatenate(ys, axis=-1) + xs * dsk_ref[...]
    out_ref[rows, ATTN_WIDTH:ATTN_WIDTH + SSM_WIDTH] = _gated_group_norm(
        y, pr[rows, COL_Z:COL_Z + SSM_WIDTH], nw_ref[...])


def _prompt_mixer_kernel(tiles_per_seq, sink_ref, x_ref, n1_ref, w_ref, tab_ref, cw_ref, cb_ref, dtb_ref,
                         alog_ref, dsk_ref, nw_ref, mix_ref, kv_ref, conv_ref, ssm_ref,
                         proj_ref, kst_ref, vst_ref, hist_ref, st_ref):
    t = pl.program_id(0)

    @pl.when(t == 0)
    def _():
        proj_ref[1] = jnp.zeros((MIX_TILE, PROJ_COLS), F32)
        kst_ref[...] = jnp.zeros_like(kst_ref)
        vst_ref[...] = jnp.zeros_like(vst_ref)
        hist_ref[...] = jnp.zeros_like(hist_ref)
        st_ref[...] = jnp.zeros_like(st_ref)

    slot_w = lax.rem(t, 2)
    pw = proj_ref.at[slot_w]
    pr = proj_ref.at[1 - slot_w]
    seq_start = lax.rem(t + tiles_per_seq - 1, tiles_per_seq) == 0

    h = _rms(x_ref[...], n1_ref[...]).astype(BF16)

    def project(piece):
        lo, hi = PROJ_PIECES[piece]
        pw[:, lo:hi] = jnp.dot(h, w_ref[:, lo:hi], preferred_element_type=F32)

    project(0)
    for j in range(MIX_BLOCKS):
        r0 = j * CHUNK
        start = seq_start if j == 0 else None
        decay_terms = _ssd_decay(pr, r0, dtb_ref, alog_ref)
        scores = _attn_scores(pr, r0, kst_ref, vst_ref)
        project(j + 1)
        xc = _ssd_conv(pr, r0, start, cw_ref, cb_ref, hist_ref)
        p4s = _attn_softmax(scores, start, sink_ref, tab_ref)
        _attn_values(p4s, r0, vst_ref, mix_ref)
        _ssd_matmuls(pr, r0, start, decay_terms, xc, dsk_ref, nw_ref, st_ref, mix_ref)

    kv_ref[0] = pr[MIX_TILE - WINDOW:, COL_K:COL_K + KV_COLS]
    conv_ref[0] = pr[MIX_TILE - (CONV_W - 1):, COL_XBC:COL_XBC + CONV_DIM]
    ssm_ref[0] = st_ref[...]


def _prompt_mixer(x2d, n1, w_perm, sinks, tab, conv_w, conv_b, dtb_t, alog_t, dsk, nw, batch, seq):
    assert seq % MIX_TILE == 0 and len(PROJ_PIECES) == MIX_BLOCKS + 1
    tiles_per_seq = seq // MIX_TILE
    nt = batch * tiles_per_seq

    def prev_tile(t):
        return jnp.maximum(t - 1, 0)

    return pl.pallas_call(
        functools.partial(_prompt_mixer_kernel, tiles_per_seq),
        out_shape=(jax.ShapeDtypeStruct((batch * seq, ATTN_WIDTH + SSM_WIDTH), F32),
                   jax.ShapeDtypeStruct((batch, WINDOW, KV_COLS), F32),
                   jax.ShapeDtypeStruct((batch, CONV_W - 1, CONV_DIM), F32),
                   jax.ShapeDtypeStruct((batch, SSM_HEADS, SSM_HEAD_DIM, D_STATE), F32)),
        grid=(nt + 1,),
        in_specs=[pl.BlockSpec(memory_space=pltpu.SMEM),
                  pl.BlockSpec((MIX_TILE, D_MODEL), lambda t: (jnp.minimum(t, nt - 1), 0)),
                  _const_spec((1, D_MODEL)),
                  _const_spec((D_MODEL, PROJ_COLS)),
                  _const_spec((N_HEADS, WINDOW, WINDOW)),
                  _const_spec((CONV_W, CONV_DIM)),
                  _const_spec((1, CONV_DIM)),
                  _const_spec((SSM_HEADS, CHUNK)),
                  _const_spec((SSM_HEADS, CHUNK)),
                  _const_spec((1, SSM_WIDTH)),
                  _const_spec((1, SSM_WIDTH))],
        out_specs=(pl.BlockSpec((MIX_TILE, ATTN_WIDTH + SSM_WIDTH), lambda t: (prev_tile(t), 0)),
                   pl.BlockSpec((1, WINDOW, KV_COLS), lambda t: (prev_tile(t) // tiles_per_seq, 0, 0)),
                   pl.BlockSpec((1, CONV_W - 1, CONV_DIM), lambda t: (prev_tile(t) // tiles_per_seq, 0, 0)),
                   pl.BlockSpec((1, SSM_HEADS, SSM_HEAD_DIM, D_STATE),
                                lambda t: (prev_tile(t) // tiles_per_seq, 0, 0, 0))),
        scratch_shapes=[pltpu.VMEM((2, MIX_TILE, PROJ_COLS), F32),
                        pltpu.VMEM((N_KV_HEADS, 4 * WINDOW, WINDOW), BF16),
                        pltpu.VMEM((N_KV_HEADS, 4 * WINDOW, WINDOW), BF16),
                        pltpu.VMEM((8, CONV_DIM), F32),
                        pltpu.VMEM((SSM_HEADS, SSM_HEAD_DIM, D_STATE), F32)],
        compiler_params=pltpu.CompilerParams(dimension_semantics=("arbitrary",),
                                             vmem_limit_bytes=VMEM_LIMIT),
        name="prompt_mixer",
    )(sinks, x2d, n1, w_perm, tab, conv_w, conv_b, dtb_t, alog_t, dsk, nw)


def _tail_kernel(x_ref, mix_ref, wo_ref, n2_ref, wg_ref, wu_ref, wd_ref, fn_ref, o_ref):
    x1 = x_ref[...] + jnp.dot(mix_ref[...].astype(BF16), wo_ref[...], preferred_element_type=F32)
    h2 = _rms(x1, n2_ref[...]).astype(BF16)
    acc = None
    for j in range(N_FF_CHUNKS):
        cols = slice(j * FF_CHUNK, (j + 1) * FF_CHUNK)
        gate = jnp.dot(h2, wg_ref[:, cols], preferred_element_type=F32)
        up = jnp.dot(h2, wu_ref[:, cols], preferred_element_type=F32)
        act = (_silu(gate) * up).astype(BF16)
        part = jnp.dot(act, wd_ref[cols, :], preferred_element_type=F32)
        acc = part if acc is None else acc + part
    o_ref[...] = _rms(x1 + acc, fn_ref[...])


def _tail(x2d, mix, wo, n2, wg, wu, wd, fn, tm):
    n = x2d.shape[0]
    return pl.pallas_call(
        _tail_kernel,
        out_shape=jax.ShapeDtypeStruct((n, D_MODEL), F32),
        grid=(n // tm,),
        in_specs=[pl.BlockSpec((tm, D_MODEL), lambda i: (i, 0)),
                  pl.BlockSpec((tm, ATTN_WIDTH + SSM_WIDTH), lambda i: (i, 0)),
                  _const_spec((D_MODEL, D_MODEL)),
                  _const_spec((1, D_MODEL)),
                  _const_spec((D_MODEL, D_FF)),
                  _const_spec((D_MODEL, D_FF)),
                  _const_spec((D_FF, D_MODEL)),
                  _const_spec((1, D_MODEL))],
        out_specs=pl.BlockSpec((tm, D_MODEL), lambda i: (i, 0)),
        compiler_params=pltpu.CompilerParams(dimension_semantics=("arbitrary",),
                                             vmem_limit_bytes=VMEM_LIMIT),
        name="outproj_ffn",
    )(x2d, mix, wo, n2, wg, wu, wd, fn)


SAMPLE_BT = 8


def _attn_sample_kernel(q_ref, kn_ref, vn_ref, ck_ref, cv_ref, ts_ref, tn_ref, sink_ref,
                        a_ref, nk_ref, nv_ref):
    rowid = lax.broadcasted_iota(jnp.int32, (N_HEADS, HEAD_DIM), 0)
    ts = ts_ref[...]
    tn = tn_ref[:, 0:1]
    sink = sink_ref[:, 0:1]
    for bb in range(SAMPLE_BT):
        qb = q_ref[bb]
        qbd = jnp.concatenate([jnp.where(rowid < KV_REP, qb, 0.0), jnp.where(rowid >= KV_REP, qb, 0.0)], axis=1)
        kn = kn_ref[bb:bb + 1, :]
        vn = vn_ref[bb:bb + 1, :]
        s_c = lax.dot_general(qbd.astype(BF16), ck_ref[bb].astype(BF16), _NT,
                              preferred_element_type=F32) * ATTN_SCALE + ts
        s_n = jnp.sum(qbd * kn, axis=-1, keepdims=True) * ATTN_SCALE + tn
        m = jnp.maximum(jnp.maximum(jnp.max(s_c, axis=-1, keepdims=True), s_n), sink)
        e_c = jnp.exp(s_c - m)
        e_n = jnp.exp(s_n - m)
        den = jnp.sum(e_c, axis=-1, keepdims=True) + e_n + jnp.exp(sink - m)
        o = jnp.dot((e_c / den).astype(BF16), cv_ref[bb].astype(BF16), preferred_element_type=F32) + (e_n / den) * vn
        a_ref[bb] = jnp.where(rowid < KV_REP, o[:, :HEAD_DIM], o[:, HEAD_DIM:])
        nk_ref[bb, 0:WINDOW - 1, :] = ck_ref[bb, 1:WINDOW, :]
        nk_ref[bb, WINDOW - 1:WINDOW, :] = kn
        nv_ref[bb, 0:WINDOW - 1, :] = cv_ref[bb, 1:WINDOW, :]
        nv_ref[bb, WINDOW - 1:WINDOW, :] = vn


def _attn_sample(q3, proj, ck, cv, ts, tn, sink_b):
    nb = q3.shape[0]
    bt = SAMPLE_BT
    kv = N_KV_HEADS * HEAD_DIM
    return pl.pallas_call(
        _attn_sample_kernel,
        out_shape=(jax.ShapeDtypeStruct((nb, N_HEADS, HEAD_DIM), F32),
                   jax.ShapeDtypeStruct((nb, WINDOW, kv), F32),
                   jax.ShapeDtypeStruct((nb, WINDOW, kv), F32)),
        grid=(nb // bt,),
        in_specs=[pl.BlockSpec((bt, N_HEADS, HEAD_DIM), lambda i: (i, 0, 0)),
                  pl.BlockSpec((bt, kv), lambda i: (i, COL_K // kv)),
                  pl.BlockSpec((bt, kv), lambda i: (i, COL_V // kv)),
                  pl.BlockSpec((bt, WINDOW, kv), lambda i: (i, 0, 0)),
                  pl.BlockSpec((bt, WINDOW, kv), lambda i: (i, 0, 0)),
                  _const_spec((N_HEADS, WINDOW)),
                  _const_spec((N_HEADS, WINDOW)),
                  _const_spec((N_HEADS, WINDOW))],
        out_specs=(pl.BlockSpec((bt, N_HEADS, HEAD_DIM), lambda i: (i, 0, 0)),
                   pl.BlockSpec((bt, WINDOW, kv), lambda i: (i, 0, 0)),
                   pl.BlockSpec((bt, WINDOW, kv), lambda i: (i, 0, 0))),
        compiler_params=pltpu.CompilerParams(dimension_semantics=("arbitrary",),
                                             vmem_limit_bytes=VMEM_LIMIT),
        name="attn_sample",
    )(q3, proj, proj, ck, cv, ts, tn, sink_b)


def _ssm_sample_prep_kernel(xbc_ref, dt_ref, cst_ref, cw_ref, cb_ref, dtb_ref, alog_ref,
                            xs_ref, bm_ref, cm_ref, xdt_t_ref, dec_ref, nc_ref):
    xbc = xbc_ref[...]
    tot = cst_ref[0] * cw_ref[0:1, :]
    tot = tot + cst_ref[1] * cw_ref[1:2, :]
    tot = tot + cst_ref[2] * cw_ref[2:3, :]
    tot = tot + xbc * cw_ref[3:4, :]
    xc = _silu(cb_ref[...] + tot)
    nc_ref[0] = cst_ref[1]
    nc_ref[1] = cst_ref[2]
    nc_ref[2] = xbc
    xs = xc[:, :SSM_WIDTH]
    xs_ref[...] = xs
    bm_ref[...] = xc[:, SSM_WIDTH:SSM_WIDTH + SSM_GROUPS * D_STATE]
    cm_ref[...] = xc[:, SSM_WIDTH + SSM_GROUPS * D_STATE:]
    dt = _softplus(dt_ref[...] + dtb_ref[...])
    dec_ref[...] = jnp.exp(dt * (-jnp.exp(alog_ref[...])))
    dt_t = dt.T
    for k in range(SSM_WIDTH // 128):
        xt = xs[:, k * 128:(k + 1) * 128].T
        for half in range(2):
            h = 2 * k + half
            lo = h * SSM_HEAD_DIM
            xdt_t_ref[lo:lo + SSM_HEAD_DIM, :] = xt[half * SSM_HEAD_DIM:(half + 1) * SSM_HEAD_DIM, :] * dt_t[h:h + 1, :]


def _ssm_sample_prep(proj, conv_state_t, conv_w, conv_b, dtb, alog):
    nb = proj.shape[0]
    return pl.pallas_call(
        _ssm_sample_prep_kernel,
        out_shape=(jax.ShapeDtypeStruct((nb, SSM_WIDTH), F32),
                   jax.ShapeDtypeStruct((nb, SSM_GROUPS * D_STATE), F32),
                   jax.ShapeDtypeStruct((nb, SSM_GROUPS * D_STATE), F32),
                   jax.ShapeDtypeStruct((SSM_WIDTH, nb), F32),
                   jax.ShapeDtypeStruct((nb, DT_PAD), F32),
                   jax.ShapeDtypeStruct((CONV_W - 1, nb, CONV_DIM), F32)),
        grid=(1,),
        in_specs=[pl.BlockSpec((nb, CONV_DIM), lambda i: (0, COL_XBC // CONV_DIM)),
                  pl.BlockSpec((nb, DT_PAD), lambda i: (0, COL_DT // DT_PAD)),
                  _const_spec((CONV_W - 1, nb, CONV_DIM)),
                  _const_spec((CONV_W, CONV_DIM)),
                  _const_spec((1, CONV_DIM)),
                  _const_spec((1, DT_PAD)),
                  _const_spec((1, DT_PAD))],
        out_specs=(_const_spec((nb, SSM_WIDTH)),
                   _const_spec((nb, SSM_GROUPS * D_STATE)),
                   _const_spec((nb, SSM_GROUPS * D_STATE)),
                   _const_spec((SSM_WIDTH, nb)),
                   _const_spec((nb, DT_PAD)),
                   _const_spec((CONV_W - 1, nb, CONV_DIM))),
        compiler_params=pltpu.CompilerParams(dimension_semantics=("arbitrary",),
                                             vmem_limit_bytes=VMEM_LIMIT),
        name="ssm_sample_prep",
    )(proj, proj, conv_state_t, conv_w, conv_b, dtb, alog)


def _ssm_sample_state_kernel(dec_ref, xdt_t_ref, bm_ref, cm_ref, xs_ref, z_ref, dsk_ref, nw_ref, st_ref,
                             nst_ref, s_ref, yt_ref):
    i = pl.program_id(0)
    last = pl.num_programs(0) - 1
    rows = SSM_REP * SSM_HEAD_DIM

    @pl.when(i == 0)
    def _():
        yt_ref[...] = jnp.zeros_like(yt_ref)

    lane = lax.broadcasted_iota(jnp.int32, (rows, 128), 1)
    for bb in range(SAMPLE_BT):
        b = i * SAMPLE_BT + bb
        mine = lane == b
        for g in range(SSM_GROUPS):
            bg = bm_ref[:, g * D_STATE:(g + 1) * D_STATE].astype(BF16)
            cg = cm_ref[:, g * D_STATE:(g + 1) * D_STATE].astype(BF16)
            xsel = jnp.where(mine, xdt_t_ref[g * rows:(g + 1) * rows, :], 0.0).astype(BF16)
            outer = jnp.dot(xsel, bg, preferred_element_type=F32)
            new = []
            for r in range(SSM_REP):
                h = g * SSM_REP + r
                hn = st_ref[bb, h] * dec_ref[b * SSM_HEADS + h] + outer[r * SSM_HEAD_DIM:(r + 1) * SSM_HEAD_DIM, :]
                nst_ref[bb, h] = hn
                new.append(hn)
            hcat = jnp.concatenate(new, axis=0).astype(BF16)
            res = lax.dot_general(hcat, cg, _NT, preferred_element_type=F32)
            yt_ref[g * rows:(g + 1) * rows, :] += jnp.where(mine, res, 0.0)

    @pl.when(i == last)
    def _():
        y = jnp.concatenate([yt_ref[k * 128:(k + 1) * 128, :].T for k in range(SSM_WIDTH // 128)], axis=1)
        y = y + xs_ref[...] * dsk_ref[...]
        s_ref[...] = _gated_group_norm(y, z_ref[...], nw_ref[...])


def _ssm_sample_state(dec_flat, xdt_t, bm, cm, xs, proj, dsk, nw, state):
    nb = state.shape[0]
    bt = SAMPLE_BT
    return pl.pallas_call(
        _ssm_sample_state_kernel,
        out_shape=(jax.ShapeDtypeStruct(state.shape, F32),
                   jax.ShapeDtypeStruct((nb, SSM_WIDTH), F32)),
        grid=(nb // bt,),
        in_specs=[pl.BlockSpec(memory_space=pltpu.SMEM),
                  _const_spec((SSM_WIDTH, nb)),
                  _const_spec((nb, SSM_GROUPS * D_STATE)),
                  _const_spec((nb, SSM_GROUPS * D_STATE)),
                  _const_spec((nb, SSM_WIDTH)),
                  pl.BlockSpec((nb, SSM_WIDTH), lambda i: (0, COL_Z // SSM_WIDTH)),
                  _const_spec((1, SSM_WIDTH)),
                  _const_spec((1, SSM_WIDTH)),
                  pl.BlockSpec((bt, SSM_HEADS, SSM_HEAD_DIM, D_STATE), lambda i: (i, 0, 0, 0))],
        out_specs=(pl.BlockSpec((bt, SSM_HEADS, SSM_HEAD_DIM, D_STATE), lambda i: (i, 0, 0, 0)),
                   pl.BlockSpec((nb, SSM_WIDTH), lambda i: (0, 0))),
        scratch_shapes=[pltpu.VMEM((SSM_WIDTH, nb), F32)],
        compiler_params=pltpu.CompilerParams(dimension_semantics=("arbitrary",),
                                             vmem_limit_bytes=VMEM_LIMIT),
        name="ssm_sample_state",
    )(dec_flat, xdt_t, bm, cm, xs, proj, dsk, nw, state)


def _pad_lanes(v, width):
    return jnp.pad(v.reshape(1, -1), ((0, 0), (0, width - v.shape[-1])))


def kernel(x_prompt, x_sample, cache_k, cache_v, state_conv, state_ssm, rel_bias, norm1_w, w_in, attn_sinks,
           conv_w, conv_b, dt_bias, A_log, D_skip, ssm_norm_w, w_out, norm2_w, w_gate, w_up, w_down, final_norm_w):
    depth = norm1_w.shape[0]
    assert depth == 1, "single-layer trunk"
    batch, seq, _ = x_prompt.shape
    nb = x_sample.shape[0]
    assert x_sample.shape[1] == 1 and nb == 128

    wi = w_in[0]
    q_c, k_c, v_c, z_c, xbc_c, dt_c = (wi[:, 0:512], wi[:, 512:640], wi[:, 640:768], wi[:, 768:1280],
                                       wi[:, 1280:2304], wi[:, 2304:2312])
    w_perm = jnp.concatenate([q_c, z_c, xbc_c, k_c, v_c, jnp.pad(dt_c, ((0, 0), (0, DT_PAD - SSM_HEADS)))],
                             axis=1).astype(BF16)
    wo = w_out[0].astype(BF16)
    wg = w_gate[0].astype(BF16)
    wu = w_up[0].astype(BF16)
    wd = w_down[0].astype(BF16)
    n1 = norm1_w[0].reshape(1, D_MODEL)
    n2 = norm2_w[0].reshape(1, D_MODEL)
    fn = final_norm_w.reshape(1, D_MODEL)
    cw = conv_w[0]
    cb = conv_b[0].reshape(1, CONV_DIM)
    dtb = _pad_lanes(dt_bias[0], DT_PAD)
    alog = _pad_lanes(A_log[0], DT_PAD)
    dtb_t = jnp.broadcast_to(dt_bias[0][:, None], (SSM_HEADS, CHUNK))
    alog_t = jnp.broadcast_to(A_log[0][:, None], (SSM_HEADS, CHUNK))
    dsk = jnp.repeat(D_skip[0], SSM_HEAD_DIM).reshape(1, SSM_WIDTH)
    nw = ssm_norm_w[0].reshape(1, SSM_WIDTH)
    sinks = attn_sinks[0]
    sink_b = jnp.broadcast_to(sinks[:, None], (N_HEADS, WINDOW))

    tab_p, tab_s, tab_n = _bias_tables(rel_bias)

    xp2 = x_prompt.reshape(batch * seq, D_MODEL)
    mix_p, kv_p, conv_p, ssm_p = _prompt_mixer(xp2, n1, w_perm, sinks, tab_p, cw, cb, dtb_t, alog_t, dsk, nw,
                                                batch, seq)
    y_p = _tail(xp2, mix_p, wo, n2, wg, wu, wd, fn, TAIL_TILE)
    kv_p = kv_p.reshape(batch, WINDOW, 2, N_KV_HEADS, HEAD_DIM)

    xs2 = x_sample.reshape(nb, D_MODEL)
    proj_s = _inproj(xs2, n1, w_perm, nb)
    q3 = proj_s[:, COL_Q:COL_Q + ATTN_WIDTH].reshape(nb, N_HEADS, HEAD_DIM)
    ck = cache_k[0].reshape(nb, WINDOW, N_KV_HEADS * HEAD_DIM)
    cv = cache_v[0].reshape(nb, WINDOW, N_KV_HEADS * HEAD_DIM)
    a_s3, nk, nv = _attn_sample(q3, proj_s, ck, cv, tab_s, tab_n, sink_b)
    conv_t = jnp.transpose(state_conv[0], (1, 0, 2))
    xs_s, bm_s, cm_s, xdt_t, dec, nconv_t = _ssm_sample_prep(proj_s, conv_t, cw, cb, dtb, alog)
    dec_flat = dec[:, :SSM_HEADS].reshape(nb * SSM_HEADS)
    nssm, s_s = _ssm_sample_state(dec_flat, xdt_t, bm_s, cm_s, xs_s, proj_s, dsk, nw, state_ssm[0])
    mix_s = jnp.concatenate([a_s3.reshape(nb, ATTN_WIDTH), s_s], axis=1)
    y_s = _tail(xs2, mix_s, wo, n2, wg, wu, wd, fn, nb)

    return (y_p.reshape(batch, seq, D_MODEL),
            y_s.reshape(nb, 1, D_MODEL),
            kv_p[:, :, 0][None],
            kv_p[:, :, 1][None],
            conv_p[None],
            ssm_p[None],
            nk.reshape(1, nb, WINDOW, N_KV_HEADS, HEAD_DIM),
            nv.reshape(1, nb, WINDOW, N_KV_HEADS, HEAD_DIM),
            jnp.transpose(nconv_t, (1, 0, 2))[None],
            nssm[None])
```

```python
import functools
import math

import numpy as np
import jax
import jax.numpy as jnp
from jax import lax
from jax.experimental import pallas as pl
from jax.experimental.pallas import tpu as pltpu

F32 = jnp.float32
BF16 = jnp.bfloat16

D_MODEL = 1024
HEAD_DIM = 64
N_HEADS = 8
N_KV_HEADS = 2
KV_REP = 4
WINDOW = 128
ATTN_WIDTH = 512
ATTN_SCALE = HEAD_DIM ** -0.5
N_BUCKETS = 32
MAX_DISTANCE = 128
SSM_WIDTH = 512
SSM_HEADS = 8
SSM_GROUPS = 2
SSM_REP = 4
SSM_HEAD_DIM = 64
D_STATE = 128
CONV_W = 4
CONV_DIM = 1024
CHUNK = 128
D_FF = 2816
EPS = 1e-6

COL_Q, COL_Z, COL_XBC, COL_K, COL_V, COL_DT = 0, 512, 1024, 2048, 2176, 2304
PROJ_COLS = 2432
DT_PAD = 128
KV_COLS = 2 * N_KV_HEADS * HEAD_DIM

FF_CHUNK = 256
N_FF_CHUNKS = D_FF // FF_CHUNK

MIX_TILE = 512
MIX_BLOCKS = MIX_TILE // CHUNK
PROJ_PIECES = ((0, 512), (512, 1024), (1024, 1536), (1536, 2048), (2048, PROJ_COLS))

TAIL_TILE = 512

VMEM_LIMIT = 56 * 1024 * 1024

_NT = (((1,), (1,)), ((), ()))


def _bucket_table(dist):
    n = np.maximum(dist, 0)
    exact = N_BUCKETS // 2
    nf = np.maximum(n, 1).astype(np.float32)
    large = exact + (np.log(nf / exact) / math.log(MAX_DISTANCE / exact) * (N_BUCKETS - exact)).astype(np.int32)
    return np.where(n < exact, n, np.minimum(large, N_BUCKETS - 1)).astype(np.int32)


def _prompt_buckets():
    i = np.arange(WINDOW)[:, None]
    j = np.arange(WINDOW)[None, :]
    dist = np.where(j > i, i + WINDOW - j, i - j)
    return _bucket_table(dist)


def _sample_buckets():
    dist = WINDOW - np.arange(WINDOW)[None, :]
    band = (dist >= 0) & (dist < WINDOW)
    row = np.where(band, _bucket_table(dist), -1).astype(np.int32)
    return np.tile(row, (N_HEADS, 1))


def _rms(x, w):
    return x * lax.rsqrt(jnp.mean(x * x, axis=-1, keepdims=True) + EPS) * w


def _silu(x):
    return x * (0.5 + 0.5 * jnp.tanh(0.5 * x))


def _softplus(x):
    return jnp.maximum(x, 0.0) + jnp.log1p(jnp.exp(-jnp.abs(x)))


def _const_spec(shape):
    nd = len(shape)
    return pl.BlockSpec(shape, lambda *_: (0,) * nd, pipeline_mode=pl.Buffered(1))


def _bias_kernel(rb_ref, bp_ref, bs_ref, tp_ref, ts_ref, tn_ref):
    bp = bp_ref[...]
    bs = bs_ref[...]
    rowid = lax.broadcasted_iota(jnp.int32, (N_HEADS, WINDOW), 0)
    ts = jnp.zeros((N_HEADS, WINDOW), F32)
    tn = jnp.zeros((N_HEADS, WINDOW), F32)
    for h in range(N_HEADS):
        tp = jnp.zeros((WINDOW, WINDOW), F32)
        for bk in range(N_BUCKETS):
            v = rb_ref[bk, h]
            tp = jnp.where(bp == bk, v, tp)
            ts = jnp.where((bs == bk) & (rowid == h), v, ts)
        tp_ref[h] = tp
        tn = jnp.where(rowid == h, rb_ref[0, h], tn)
    ts_ref[...] = jnp.where(bs < 0, -jnp.inf, ts)
    tn_ref[...] = tn


def _bias_tables(rel_bias):
    return pl.pallas_call(
        _bias_kernel,
        out_shape=(jax.ShapeDtypeStruct((N_HEADS, WINDOW, WINDOW), F32),
                   jax.ShapeDtypeStruct((N_HEADS, WINDOW), F32),
                   jax.ShapeDtypeStruct((N_HEADS, WINDOW), F32)),
        in_specs=[pl.BlockSpec(memory_space=pltpu.SMEM),
                  pl.BlockSpec(memory_space=pltpu.VMEM),
                  pl.BlockSpec(memory_space=pltpu.VMEM)],
        name="bias_tables",
    )(rel_bias, jnp.asarray(_prompt_buckets()), jnp.asarray(_sample_buckets()))


def _inproj_kernel(x_ref, nw_ref, w_ref, o_ref):
    h = _rms(x_ref[...], nw_ref[...]).astype(BF16)
    o_ref[...] = jnp.dot(h, w_ref[...], preferred_element_type=F32)


def _inproj(x2d, norm_w, w_perm, tm):
    n = x2d.shape[0]
    return pl.pallas_call(
        _inproj_kernel,
        out_shape=jax.ShapeDtypeStruct((n, PROJ_COLS), F32),
        grid=(n // tm,),
        in_specs=[pl.BlockSpec((tm, D_MODEL), lambda i: (i, 0)),
                  _const_spec((1, D_MODEL)),
                  _const_spec((D_MODEL, PROJ_COLS))],
        out_specs=pl.BlockSpec((tm, PROJ_COLS), lambda i: (i, 0)),
        compiler_params=pltpu.CompilerParams(dimension_semantics=("arbitrary",),
                                             vmem_limit_bytes=VMEM_LIMIT),
        name="inproj",
    )(x2d, norm_w, w_perm)


def _head_variants(x, low):
    xr = pltpu.roll(x, HEAD_DIM, axis=1)
    zero = jnp.zeros_like(x)
    return ((jnp.where(low, x, zero).astype(BF16), jnp.where(low, zero, xr).astype(BF16)),
            (jnp.where(low, xr, zero).astype(BF16), jnp.where(low, zero, x).astype(BF16)))


def _gated_group_norm(y, z, nw):
    y = y * _silu(z)
    half = SSM_WIDTH // SSM_GROUPS
    parts = []
    for g in range(SSM_GROUPS):
        yg = y[:, g * half:(g + 1) * half]
        parts.append(yg * lax.rsqrt(jnp.mean(yg * yg, axis=-1, keepdims=True) + EPS))
    return jnp.concatenate(parts, axis=-1) * nw


def _attn_scores(pr, r0, kst_ref, vst_ref):
    w = WINDOW
    rows = slice(r0, r0 + w)
    low = lax.broadcasted_iota(jnp.int32, (w, w), 1) < HEAD_DIM

    for st, c0 in ((kst_ref, COL_K), (vst_ref, COL_V)):
        for g in range(N_KV_HEADS):
            st[g, 0:w, :] = st[g, w:2 * w, :]
            st[g, 2 * w:3 * w, :] = st[g, 3 * w:4 * w, :]
        var = _head_variants(pr[rows, c0:c0 + w], low)
        for g in range(N_KV_HEADS):
            st[g, w:2 * w, :] = var[g][0]
            st[g, 3 * w:4 * w, :] = var[g][1]

    scores = []
    for pp in range(N_HEADS // 2):
        q2 = pr[rows, COL_Q + pp * w:COL_Q + (pp + 1) * w].astype(BF16)
        scores.append(lax.dot_general(q2, kst_ref[pp // (KV_REP // 2)], _NT, preferred_element_type=F32))
    return scores


def _attn_softmax(scores, seq_start, sink_ref, tab_ref):
    w = WINDOW
    row = lax.broadcasted_iota(jnp.int32, (w, w), 0)
    lane = lax.broadcasted_iota(jnp.int32, (w, w), 1)
    upper = lane > row
    if seq_start is not None:
        no_prev = (lane - row) > jnp.where(seq_start, 0, w)
    p4s = []
    for pp in range(N_HEADS // 2):
        s4 = scores[pp]
        probs = []
        for half in range(2):
            hh = 2 * pp + half
            sp = s4[:, 2 * w * half:2 * w * half + w]
            sc = s4[:, 2 * w * half + w:2 * w * (half + 1)]
            s = jnp.where(upper, sp, sc) * ATTN_SCALE + tab_ref[hh]
            if seq_start is not None:
                s = jnp.where(no_prev, -jnp.inf, s)
            sk = sink_ref[hh]
            m = jnp.maximum(jnp.max(s, axis=-1, keepdims=True), sk)
            e = jnp.exp(s - m)
            den = jnp.sum(e, axis=-1, keepdims=True) + jnp.exp(sk - m)
            p = e / den
            zero = jnp.zeros_like(p)
            probs.append(jnp.where(upper, p, zero).astype(BF16))
            probs.append(jnp.where(upper, zero, p).astype(BF16))
        p4s.append(jnp.concatenate(probs, axis=1))
    return p4s


def _attn_values(p4s, r0, vst_ref, out_ref):
    w = WINDOW
    for pp in range(N_HEADS // 2):
        out_ref[r0:r0 + w, pp * w:(pp + 1) * w] = jnp.dot(p4s[pp], vst_ref[pp // (KV_REP // 2)],
                                                          preferred_element_type=F32)


def _ssd_decay(pr, r0, dtb_ref, alog_ref):
    rows = slice(r0, r0 + CHUNK)
    row = lax.broadcasted_iota(jnp.int32, (CHUNK, CHUNK), 0)
    lane = lax.broadcasted_iota(jnp.int32, (CHUNK, CHUNK), 1)
    dt_t = _softplus(pr[rows, COL_DT:COL_DT + DT_PAD].T[0:SSM_HEADS, :] + dtb_ref[...])
    a_t = dt_t * (-jnp.exp(alog_ref[...]))
    cs_t = jnp.dot(a_t, (row <= lane).astype(F32), precision=lax.Precision.HIGHEST,
                   preferred_element_type=F32)
    cs_end = cs_t[:, CHUNK - 1:CHUNK]
    w_t = jnp.exp(cs_end - cs_t) * dt_t
    dec_end = jnp.exp(cs_end)
    col = jnp.concatenate([cs_t, jnp.exp(cs_t), jnp.zeros((CHUNK - 2 * SSM_HEADS, CHUNK), F32)], axis=0).T
    return dt_t, cs_t, w_t, dec_end, col


def _ssd_conv(pr, r0, seq_start, cw_ref, cb_ref, hist_ref):
    x = pr[r0:r0 + CHUNK, COL_XBC:COL_XBC + CONV_DIM]
    hist = hist_ref[...]
    if seq_start is not None:
        hist = jnp.where(seq_start, 0.0, hist)
    xx = jnp.concatenate([hist, x], axis=0)
    tot = pltpu.roll(xx, CONV_W - 1, axis=0)[8:, :] * cw_ref[0:1, :]
    for j in range(1, CONV_W - 1):
        tot = tot + pltpu.roll(xx, CONV_W - 1 - j, axis=0)[8:, :] * cw_ref[j:j + 1, :]
    tot = tot + x * cw_ref[CONV_W - 1:CONV_W, :]
    hist_ref[...] = x[CHUNK - 8:, :]
    return _silu(cb_ref[...] + tot)


def _ssd_matmuls(pr, r0, seq_start, decay_terms, xc, dsk_ref, nw_ref, st_ref, out_ref):
    hd = SSM_HEAD_DIM
    rows = slice(r0, r0 + CHUNK)
    dt_t, cs_t, w_t, dec_end, col = decay_terms
    row = lax.broadcasted_iota(jnp.int32, (CHUNK, CHUNK), 0)
    lane = lax.broadcasted_iota(jnp.int32, (CHUNK, CHUNK), 1)
    causal = row >= lane
    low = lane < hd
    xs = xc[:, :SSM_WIDTH]
    bm = xc[:, SSM_WIDTH:SSM_WIDTH + SSM_GROUPS * D_STATE]
    cm = xc[:, SSM_WIDTH + SSM_GROUPS * D_STATE:]

    groups = range(SSM_GROUPS)
    bgs = [bm[:, g * D_STATE:(g + 1) * D_STATE].astype(BF16) for g in groups]
    cgs = [cm[:, g * D_STATE:(g + 1) * D_STATE].astype(BF16) for g in groups]
    h_prevs = []
    for g in groups:
        hp = st_ref[g * SSM_REP:(g + 1) * SSM_REP].reshape(SSM_REP * hd, D_STATE)
        if seq_start is not None:
            hp = jnp.where(seq_start, 0.0, hp)
        h_prevs.append(hp)
    cbs = [lax.dot_general(cgs[g], bgs[g], _NT, preferred_element_type=F32) for g in groups]
    y_offs = [lax.dot_general(cgs[g], h_prevs[g].astype(BF16), _NT, preferred_element_type=F32) for g in groups]

    for g in groups:
        xt, wts, decs = [], [], []
        for pr2 in range(SSM_REP // 2):
            h0 = g * SSM_REP + 2 * pr2
            xt.append(xs[:, h0 * hd:(h0 + 2) * hd].T)
            for h in (h0, h0 + 1):
                wts.append(jnp.broadcast_to(w_t[h:h + 1, :], (hd, CHUNK)))
                decs.append(jnp.broadcast_to(dec_end[h:h + 1, :], (hd, D_STATE)))
        xw = (jnp.concatenate(xt, axis=0) * jnp.concatenate(wts, axis=0)).astype(BF16)
        st = jnp.dot(xw, bgs[g], preferred_element_type=F32)
        h_new = h_prevs[g] * jnp.concatenate(decs, axis=0) + st
        st_ref[g * SSM_REP:(g + 1) * SSM_REP] = h_new.reshape(SSM_REP, hd, D_STATE)

    ys = []
    for g in groups:
        for pr2 in range(SSM_REP // 2):
            h0 = g * SSM_REP + 2 * pr2
            xpair = xs[:, h0 * hd:(h0 + 2) * hd]
            zero = jnp.zeros_like(xpair)
            m = []
            for h in (h0, h0 + 1):
                decay = jnp.exp(jnp.where(causal, col[:, h:h + 1] - cs_t[h:h + 1, :], -jnp.inf))
                m.append((cbs[g] * decay * dt_t[h:h + 1, :]).astype(BF16))
            y_diag = (jnp.dot(m[0], jnp.where(low, xpair, zero).astype(BF16), preferred_element_type=F32)
                      + jnp.dot(m[1], jnp.where(low, zero, xpair).astype(BF16), preferred_element_type=F32))
            e0 = jnp.broadcast_to(col[:, SSM_HEADS + h0:SSM_HEADS + h0 + 1], (CHUNK, CHUNK))
            e1 = jnp.broadcast_to(col[:, SSM_HEADS + h0 + 1:SSM_HEADS + h0 + 2], (CHUNK, CHUNK))
            ys.append(y_diag + y_offs[g][:, 2 * pr2 * hd:(2 * pr2 + 2) * hd] * jnp.where(low, e0, e1))
    y = jnp.concatenate(ys, axis=-1) + xs * dsk_ref[...]
    out_ref[rows, ATTN_WIDTH:ATTN_WIDTH + SSM_WIDTH] = _gated_group_norm(
        y, pr[rows, COL_Z:COL_Z + SSM_WIDTH], nw_ref[...])


def _prompt_mixer_kernel(tiles_per_seq, sink_ref, x_ref, n1_ref, w_ref, tab_ref, cw_ref, cb_ref, dtb_ref,
                         alog_ref, dsk_ref, nw_ref, mix_ref, kv_ref, conv_ref, ssm_ref,
                         proj_ref, kst_ref, vst_ref, hist_ref, st_ref):
    t = pl.program_id(0)

    @pl.when(t == 0)
    def _():
        proj_ref[1] = jnp.zeros((MIX_TILE, PROJ_COLS), F32)
        kst_ref[...] = jnp.zeros_like(kst_ref)
        vst_ref[...] = jnp.zeros_like(vst_ref)
        hist_ref[...] = jnp.zeros_like(hist_ref)
        st_ref[...] = jnp.zeros_like(st_ref)

    slot_w = lax.rem(t, 2)
    pw = proj_ref.at[slot_w]
    pr = proj_ref.at[1 - slot_w]
    seq_start = lax.rem(t + tiles_per_seq - 1, tiles_per_seq) == 0

    h = _rms(x_ref[...], n1_ref[...]).astype(BF16)

    def project(piece):
        lo, hi = PROJ_PIECES[piece]
        pw[:, lo:hi] = jnp.dot(h, w_ref[:, lo:hi], preferred_element_type=F32)

    project(0)
    for j in range(MIX_BLOCKS):
        r0 = j * CHUNK
        start = seq_start if j == 0 else None
        decay_terms = _ssd_decay(pr, r0, dtb_ref, alog_ref)
        scores = _attn_scores(pr, r0, kst_ref, vst_ref)
        project(j + 1)
        xc = _ssd_conv(pr, r0, start, cw_ref, cb_ref, hist_ref)
        p4s = _attn_softmax(scores, start, sink_ref, tab_ref)
        _attn_values(p4s, r0, vst_ref, mix_ref)
        _ssd_matmuls(pr, r0, start, decay_terms, xc, dsk_ref, nw_ref, st_ref, mix_ref)

    kv_ref[0] = pr[MIX_TILE - WINDOW:, COL_K:COL_K + KV_COLS]
    conv_ref[0] = pr[MIX_TILE - (CONV_W - 1):, COL_XBC:COL_XBC + CONV_DIM]
    ssm_ref[0] = st_ref[...]


def _prompt_mixer(x2d, n1, w_perm, sinks, tab, conv_w, conv_b, dtb_t, alog_t, dsk, nw, batch, seq):
    assert seq % MIX_TILE == 0 and len(PROJ_PIECES) == MIX_BLOCKS + 1
    tiles_per_seq = seq // MIX_TILE
    nt = batch * tiles_per_seq

    def prev_tile(t):
        return jnp.maximum(t - 1, 0)

    return pl.pallas_call(
        functools.partial(_prompt_mixer_kernel, tiles_per_seq),
        out_shape=(jax.ShapeDtypeStruct((batch * seq, ATTN_WIDTH + SSM_WIDTH), F32),
                   jax.ShapeDtypeStruct((batch, WINDOW, KV_COLS), F32),
                   jax.ShapeDtypeStruct((batch, CONV_W - 1, CONV_DIM), F32),
                   jax.ShapeDtypeStruct((batch, SSM_HEADS, SSM_HEAD_DIM, D_STATE), F32)),
        grid=(nt + 1,),
        in_specs=[pl.BlockSpec(memory_space=pltpu.SMEM),
                  pl.BlockSpec((MIX_TILE, D_MODEL), lambda t: (jnp.minimum(t, nt - 1), 0)),
                  _const_spec((1, D_MODEL)),
                  _const_spec((D_MODEL, PROJ_COLS)),
                  _const_spec((N_HEADS, WINDOW, WINDOW)),
                  _const_spec((CONV_W, CONV_DIM)),
                  _const_spec((1, CONV_DIM)),
                  _const_spec((SSM_HEADS, CHUNK)),
                  _const_spec((SSM_HEADS, CHUNK)),
                  _const_spec((1, SSM_WIDTH)),
                  _const_spec((1, SSM_WIDTH))],
        out_specs=(pl.BlockSpec((MIX_TILE, ATTN_WIDTH + SSM_WIDTH), lambda t: (prev_tile(t), 0)),
                   pl.BlockSpec((1, WINDOW, KV_COLS), lambda t: (prev_tile(t) // tiles_per_seq, 0, 0)),
                   pl.BlockSpec((1, CONV_W - 1, CONV_DIM), lambda t: (prev_tile(t) // tiles_per_seq, 0, 0)),
                   pl.BlockSpec((1, SSM_HEADS, SSM_HEAD_DIM, D_STATE),
                                lambda t: (prev_tile(t) // tiles_per_seq, 0, 0, 0))),
        scratch_shapes=[pltpu.VMEM((2, MIX_TILE, PROJ_COLS), F32),
                        pltpu.VMEM((N_KV_HEADS, 4 * WINDOW, WINDOW), BF16),
                        pltpu.VMEM((N_KV_HEADS, 4 * WINDOW, WINDOW), BF16),
                        pltpu.VMEM((8, CONV_DIM), F32),
                        pltpu.VMEM((SSM_HEADS, SSM_HEAD_DIM, D_STATE), F32)],
        compiler_params=pltpu.CompilerParams(dimension_semantics=("arbitrary",),
                                             vmem_limit_bytes=VMEM_LIMIT),
        name="prompt_mixer",
    )(sinks, x2d, n1, w_perm, tab, conv_w, conv_b, dtb_t, alog_t, dsk, nw)


def _tail_kernel(x_ref, mix_ref, wo_ref, n2_ref, wg_ref, wu_ref, wd_ref, fn_ref, o_ref):
    x1 = x_ref[...] + jnp.dot(mix_ref[...].astype(BF16), wo_ref[...], preferred_element_type=F32)
    h2 = _rms(x1, n2_ref[...]).astype(BF16)
    acc = None
    for j in range(N_FF_CHUNKS):
        cols = slice(j * FF_CHUNK, (j + 1) * FF_CHUNK)
        gate = jnp.dot(h2, wg_ref[:, cols], preferred_element_type=F32)
        up = jnp.dot(h2, wu_ref[:, cols], preferred_element_type=F32)
        act = (_silu(gate) * up).astype(BF16)
        part = jnp.dot(act, wd_ref[cols, :], preferred_element_type=F32)
        acc = part if acc is None else acc + part
    o_ref[...] = _rms(x1 + acc, fn_ref[...])


def _tail(x2d, mix, wo, n2, wg, wu, wd, fn, tm):
    n = x2d.shape[0]
    return pl.pallas_call(
        _tail_kernel,
        out_shape=jax.ShapeDtypeStruct((n, D_MODEL), F32),
        grid=(n // tm,),
        in_specs=[pl.BlockSpec((tm, D_MODEL), lambda i: (i, 0)),
                  pl.BlockSpec((tm, ATTN_WIDTH + SSM_WIDTH), lambda i: (i, 0)),
                  _const_spec((D_MODEL, D_MODEL)),
                  _const_spec((1, D_MODEL)),
                  _const_spec((D_MODEL, D_FF)),
                  _const_spec((D_MODEL, D_FF)),
                  _const_spec((D_FF, D_MODEL)),
                  _const_spec((1, D_MODEL))],
        out_specs=pl.BlockSpec((tm, D_MODEL), lambda i: (i, 0)),
        compiler_params=pltpu.CompilerParams(dimension_semantics=("arbitrary",),
                                             vmem_limit_bytes=VMEM_LIMIT),
        name="outproj_ffn",
    )(x2d, mix, wo, n2, wg, wu, wd, fn)


SAMPLE_BT = 8
ATTN_SAMPLE_BT = 16


def _attn_sample_kernel(q_ref, kn_ref, vn_ref, ck_ref, cv_ref, ts_ref, tn_ref, sink_ref,
                        a_ref, nk_ref, nv_ref):
    rowid = lax.broadcasted_iota(jnp.int32, (N_HEADS, HEAD_DIM), 0)
    ts = ts_ref[...]
    tn = tn_ref[:, 0:1]
    sink = sink_ref[:, 0:1]
    pad = jnp.zeros((WINDOW - ATTN_SAMPLE_BT, WINDOW), F32)
    kn_t = jnp.concatenate([kn_ref[...], pad], axis=0).T
    vn_t = jnp.concatenate([vn_ref[...], pad], axis=0).T
    newest = lax.broadcasted_iota(jnp.int32, (WINDOW, WINDOW), 1) == WINDOW - 1

    def shifted(cache_ref, new_t, bb):
        return jnp.where(newest, new_t[:, bb:bb + 1], pltpu.roll(cache_ref[bb], WINDOW - 1, axis=1))

    qbds, scores = [], []
    for bb in range(ATTN_SAMPLE_BT):
        qb = q_ref[bb]
        qbd = jnp.concatenate([jnp.where(rowid < KV_REP, qb, 0.0), jnp.where(rowid >= KV_REP, qb, 0.0)], axis=1)
        qbds.append(qbd)
        scores.append(jnp.dot(qbd.astype(BF16), ck_ref[bb].astype(BF16), preferred_element_type=F32))
    probs, new_terms = [], []
    for bb in range(ATTN_SAMPLE_BT):
        s_c = scores[bb] * ATTN_SCALE + ts
        s_n = jnp.sum(qbds[bb] * kn_ref[bb:bb + 1, :], axis=-1, keepdims=True) * ATTN_SCALE + tn
        m = jnp.maximum(jnp.maximum(jnp.max(s_c, axis=-1, keepdims=True), s_n), sink)
        e_c = jnp.exp(s_c - m)
        e_n = jnp.exp(s_n - m)
        den = jnp.sum(e_c, axis=-1, keepdims=True) + e_n + jnp.exp(sink - m)
        probs.append((e_c / den).astype(BF16))
        new_terms.append((e_n / den) * vn_ref[bb:bb + 1, :])
    for bb in range(ATTN_SAMPLE_BT):
        o = lax.dot_general(probs[bb], cv_ref[bb].astype(BF16), _NT, preferred_element_type=F32) + new_terms[bb]
        a_ref[bb] = jnp.where(rowid < KV_REP, o[:, :HEAD_DIM], o[:, HEAD_DIM:])
    for bb in range(ATTN_SAMPLE_BT):
        nk_ref[bb] = shifted(ck_ref, kn_t, bb)
        nv_ref[bb] = shifted(cv_ref, vn_t, bb)


def _attn_sample(q3, proj, ck, cv, ts, tn, sink_b):
    nb = q3.shape[0]
    bt = ATTN_SAMPLE_BT
    kv = N_KV_HEADS * HEAD_DIM
    return pl.pallas_call(
        _attn_sample_kernel,
        out_shape=(jax.ShapeDtypeStruct((nb, N_HEADS, HEAD_DIM), F32),
                   jax.ShapeDtypeStruct((nb, WINDOW, kv), F32),
                   jax.ShapeDtypeStruct((nb, WINDOW, kv), F32)),
        grid=(nb // bt,),
        in_specs=[pl.BlockSpec((bt, N_HEADS, HEAD_DIM), lambda i: (i, 0, 0)),
                  pl.BlockSpec((bt, kv), lambda i: (i, COL_K // kv)),
                  pl.BlockSpec((bt, kv), lambda i: (i, COL_V // kv)),
                  pl.BlockSpec((bt, WINDOW, kv), lambda i: (i, 0, 0)),
                  pl.BlockSpec((bt, WINDOW, kv), lambda i: (i, 0, 0)),
                  _const_spec((N_HEADS, WINDOW)),
                  _const_spec((N_HEADS, WINDOW)),
                  _const_spec((N_HEADS, WINDOW))],
        out_specs=(pl.BlockSpec((bt, N_HEADS, HEAD_DIM), lambda i: (i, 0, 0)),
                   pl.BlockSpec((bt, WINDOW, kv), lambda i: (i, 0, 0)),
                   pl.BlockSpec((bt, WINDOW, kv), lambda i: (i, 0, 0))),
        compiler_params=pltpu.CompilerParams(dimension_semantics=("arbitrary",),
                                             vmem_limit_bytes=VMEM_LIMIT),
        name="attn_sample",
    )(q3, proj, proj, ck, cv, ts, tn, sink_b)


def _ssm_sample_prep_kernel(xbc_ref, dt_ref, cst_ref, cw_ref, cb_ref, dtb_ref, alog_ref,
                            xs_ref, bm_ref, cm_ref, xdt_t_ref, dec_ref, nc_ref):
    xbc = xbc_ref[...]
    tot = cst_ref[0] * cw_ref[0:1, :]
    tot = tot + cst_ref[1] * cw_ref[1:2, :]
    tot = tot + cst_ref[2] * cw_ref[2:3, :]
    tot = tot + xbc * cw_ref[3:4, :]
    xc = _silu(cb_ref[...] + tot)
    nc_ref[0] = cst_ref[1]
    nc_ref[1] = cst_ref[2]
    nc_ref[2] = xbc
    xs = xc[:, :SSM_WIDTH]
    xs_ref[...] = xs
    bm_ref[...] = xc[:, SSM_WIDTH:SSM_WIDTH + SSM_GROUPS * D_STATE]
    cm_ref[...] = xc[:, SSM_WIDTH + SSM_GROUPS * D_STATE:]
    dt = _softplus(dt_ref[...] + dtb_ref[...])
    dec_ref[...] = jnp.exp(dt * (-jnp.exp(alog_ref[...])))
    dt_t = dt.T
    for k in range(SSM_WIDTH // 128):
        xt = xs[:, k * 128:(k + 1) * 128].T
        for half in range(2):
            h = 2 * k + half
            lo = h * SSM_HEAD_DIM
            xdt_t_ref[lo:lo + SSM_HEAD_DIM, :] = xt[half * SSM_HEAD_DIM:(half + 1) * SSM_HEAD_DIM, :] * dt_t[h:h + 1, :]


def _ssm_sample_prep(proj, conv_state_t, conv_w, conv_b, dtb, alog):
    nb = proj.shape[0]
    return pl.pallas_call(
        _ssm_sample_prep_kernel,
        out_shape=(jax.ShapeDtypeStruct((nb, SSM_WIDTH), F32),
                   jax.ShapeDtypeStruct((nb, SSM_GROUPS * D_STATE), F32),
                   jax.ShapeDtypeStruct((nb, SSM_GROUPS * D_STATE), F32),
                   jax.ShapeDtypeStruct((SSM_WIDTH, nb), F32),
                   jax.ShapeDtypeStruct((nb, DT_PAD), F32),
                   jax.ShapeDtypeStruct((CONV_W - 1, nb, CONV_DIM), F32)),
        grid=(1,),
        in_specs=[pl.BlockSpec((nb, CONV_DIM), lambda i: (0, COL_XBC // CONV_DIM)),
                  pl.BlockSpec((nb, DT_PAD), lambda i: (0, COL_DT // DT_PAD)),
                  _const_spec((CONV_W - 1, nb, CONV_DIM)),
                  _const_spec((CONV_W, CONV_DIM)),
                  _const_spec((1, CONV_DIM)),
                  _const_spec((1, DT_PAD)),
                  _const_spec((1, DT_PAD))],
        out_specs=(_const_spec((nb, SSM_WIDTH)),
                   _const_spec((nb, SSM_GROUPS * D_STATE)),
                   _const_spec((nb, SSM_GROUPS * D_STATE)),
                   _const_spec((SSM_WIDTH, nb)),
                   _const_spec((nb, DT_PAD)),
                   _const_spec((CONV_W - 1, nb, CONV_DIM))),
        compiler_params=pltpu.CompilerParams(dimension_semantics=("arbitrary",),
                                             vmem_limit_bytes=VMEM_LIMIT),
        name="ssm_sample_prep",
    )(proj, proj, conv_state_t, conv_w, conv_b, dtb, alog)


def _ssm_sample_state_kernel(dec_ref, xdt_t_ref, bm_ref, cm_ref, xs_ref, z_ref, dsk_ref, nw_ref, st_ref,
                             nst_ref, s_ref, yt_ref):
    i = pl.program_id(0)
    last = pl.num_programs(0) - 1
    rows = SSM_REP * SSM_HEAD_DIM

    @pl.when(i == 0)
    def _():
        yt_ref[...] = jnp.zeros_like(yt_ref)

    lane = lax.broadcasted_iota(jnp.int32, (rows, 128), 1)
    for bb in range(SAMPLE_BT):
        b = i * SAMPLE_BT + bb
        mine = lane == b
        for g in range(SSM_GROUPS):
            bg = bm_ref[:, g * D_STATE:(g + 1) * D_STATE].astype(BF16)
            cg = cm_ref[:, g * D_STATE:(g + 1) * D_STATE].astype(BF16)
            xsel = jnp.where(mine, xdt_t_ref[g * rows:(g + 1) * rows, :], 0.0).astype(BF16)
            outer = jnp.dot(xsel, bg, preferred_element_type=F32)
            new = []
            for r in range(SSM_REP):
                h = g * SSM_REP + r
                hn = st_ref[bb, h] * dec_ref[b * SSM_HEADS + h] + outer[r * SSM_HEAD_DIM:(r + 1) * SSM_HEAD_DIM, :]
                nst_ref[bb, h] = hn
                new.append(hn)
            hcat = jnp.concatenate(new, axis=0).astype(BF16)
            res = lax.dot_general(hcat, cg, _NT, preferred_element_type=F32)
            yt_ref[g * rows:(g + 1) * rows, :] += jnp.where(mine, res, 0.0)

    @pl.when(i == last)
    def _():
        y = jnp.concatenate([yt_ref[k * 128:(k + 1) * 128, :].T for k in range(SSM_WIDTH // 128)], axis=1)
        y = y + xs_ref[...] * dsk_ref[...]
        s_ref[...] = _gated_group_norm(y, z_ref[...], nw_ref[...])


def _ssm_sample_state(dec_flat, xdt_t, bm, cm, xs, proj, dsk, nw, state):
    nb = state.shape[0]
    bt = SAMPLE_BT
    return pl.pallas_call(
        _ssm_sample_state_kernel,
        out_shape=(jax.ShapeDtypeStruct(state.shape, F32),
                   jax.ShapeDtypeStruct((nb, SSM_WIDTH), F32)),
        grid=(nb // bt,),
        in_specs=[pl.BlockSpec(memory_space=pltpu.SMEM),
                  _const_spec((SSM_WIDTH, nb)),
                  _const_spec((nb, SSM_GROUPS * D_STATE)),
                  _const_spec((nb, SSM_GROUPS * D_STATE)),
                  _const_spec((nb, SSM_WIDTH)),
                  pl.BlockSpec((nb, SSM_WIDTH), lambda i: (0, COL_Z // SSM_WIDTH)),
                  _const_spec((1, SSM_WIDTH)),
                  _const_spec((1, SSM_WIDTH)),
                  pl.BlockSpec((bt, SSM_HEADS, SSM_HEAD_DIM, D_STATE), lambda i: (i, 0, 0, 0))],
        out_specs=(pl.BlockSpec((bt, SSM_HEADS, SSM_HEAD_DIM, D_STATE), lambda i: (i, 0, 0, 0)),
                   pl.BlockSpec((nb, SSM_WIDTH), lambda i: (0, 0))),
        scratch_shapes=[pltpu.VMEM((SSM_WIDTH, nb), F32)],
        compiler_params=pltpu.CompilerParams(dimension_semantics=("arbitrary",),
                                             vmem_limit_bytes=VMEM_LIMIT),
        name="ssm_sample_state",
    )(dec_flat, xdt_t, bm, cm, xs, proj, dsk, nw, state)


def _pad_lanes(v, width):
    return jnp.pad(v.reshape(1, -1), ((0, 0), (0, width - v.shape[-1])))


def kernel(x_prompt, x_sample, cache_k, cache_v, state_conv, state_ssm, rel_bias, norm1_w, w_in, attn_sinks,
           conv_w, conv_b, dt_bias, A_log, D_skip, ssm_norm_w, w_out, norm2_w, w_gate, w_up, w_down, final_norm_w):
    depth = norm1_w.shape[0]
    assert depth == 1, "single-layer trunk"
    batch, seq, _ = x_prompt.shape
    nb = x_sample.shape[0]
    assert x_sample.shape[1] == 1 and nb == 128

    wi = w_in[0]
    q_c, k_c, v_c, z_c, xbc_c, dt_c = (wi[:, 0:512], wi[:, 512:640], wi[:, 640:768], wi[:, 768:1280],
                                       wi[:, 1280:2304], wi[:, 2304:2312])
    w_perm = jnp.concatenate([q_c, z_c, xbc_c, k_c, v_c, jnp.pad(dt_c, ((0, 0), (0, DT_PAD - SSM_HEADS)))],
                             axis=1).astype(BF16)
    wo = w_out[0].astype(BF16)
    wg = w_gate[0].astype(BF16)
    wu = w_up[0].astype(BF16)
    wd = w_down[0].astype(BF16)
    n1 = norm1_w[0].reshape(1, D_MODEL)
    n2 = norm2_w[0].reshape(1, D_MODEL)
    fn = final_norm_w.reshape(1, D_MODEL)
    cw = conv_w[0]
    cb = conv_b[0].reshape(1, CONV_DIM)
    dtb = _pad_lanes(dt_bias[0], DT_PAD)
    alog = _pad_lanes(A_log[0], DT_PAD)
    dtb_t = jnp.broadcast_to(dt_bias[0][:, None], (SSM_HEADS, CHUNK))
    alog_t = jnp.broadcast_to(A_log[0][:, None], (SSM_HEADS, CHUNK))
    dsk = jnp.repeat(D_skip[0], SSM_HEAD_DIM).reshape(1, SSM_WIDTH)
    nw = ssm_norm_w[0].reshape(1, SSM_WIDTH)
    sinks = attn_sinks[0]
    sink_b = jnp.broadcast_to(sinks[:, None], (N_HEADS, WINDOW))

    tab_p, tab_s, tab_n = _bias_tables(rel_bias)

    xp2 = x_prompt.reshape(batch * seq, D_MODEL)
    mix_p, kv_p, conv_p, ssm_p = _prompt_mixer(xp2, n1, w_perm, sinks, tab_p, cw, cb, dtb_t, alog_t, dsk, nw,
                                                batch, seq)
    y_p = _tail(xp2, mix_p, wo, n2, wg, wu, wd, fn, TAIL_TILE)
    kv_p = kv_p.reshape(batch, WINDOW, 2, N_KV_HEADS, HEAD_DIM)

    xs2 = x_sample.reshape(nb, D_MODEL)
    proj_s = _inproj(xs2, n1, w_perm, nb)
    q3 = proj_s[:, COL_Q:COL_Q + ATTN_WIDTH].reshape(nb, N_HEADS, HEAD_DIM)
    ck = jnp.transpose(cache_k[0], (0, 2, 3, 1)).reshape(nb, N_KV_HEADS * HEAD_DIM, WINDOW)
    cv = jnp.transpose(cache_v[0], (0, 2, 3, 1)).reshape(nb, N_KV_HEADS * HEAD_DIM, WINDOW)
    a_s3, nk, nv = _attn_sample(q3, proj_s, ck, cv, tab_s, tab_n, sink_b)
    conv_t = jnp.transpose(state_conv[0], (1, 0, 2))
    xs_s, bm_s, cm_s, xdt_t, dec, nconv_t = _ssm_sample_prep(proj_s, conv_t, cw, cb, dtb, alog)
    dec_flat = dec[:, :SSM_HEADS].reshape(nb * SSM_HEADS)
    nssm, s_s = _ssm_sample_state(dec_flat, xdt_t, bm_s, cm_s, xs_s, proj_s, dsk, nw, state_ssm[0])
    mix_s = jnp.concatenate([a_s3.reshape(nb, ATTN_WIDTH), s_s], axis=1)
    y_s = _tail(xs2, mix_s, wo, n2, wg, wu, wd, fn, nb)

    return (y_p.reshape(batch, seq, D_MODEL),
            y_s.reshape(nb, 1, D_MODEL),
            kv_p[:, :, 0][None],
            kv_p[:, :, 1][None],
            conv_p[None],
            ssm_p[None],
            jnp.transpose(nk.reshape(nb, N_KV_HEADS, HEAD_DIM, WINDOW), (0, 3, 1, 2))[None],
            jnp.transpose(nv.reshape(nb, N_KV_HEADS, HEAD_DIM, WINDOW), (0, 3, 1, 2))[None],
            jnp.transpose(nconv_t, (1, 0, 2))[None],
            nssm[None])
```

```python
import functools
import math

import numpy as np
import jax
import jax.numpy as jnp
from jax import lax
from jax.experimental import pallas as pl
from jax.experimental.pallas import tpu as pltpu

F32 = jnp.float32
BF16 = jnp.bfloat16

D_MODEL = 1024
HEAD_DIM = 64
N_HEADS = 8
N_KV_HEADS = 2
KV_REP = 4
WINDOW = 128
ATTN_WIDTH = 512
ATTN_SCALE = HEAD_DIM ** -0.5
N_BUCKETS = 32
MAX_DISTANCE = 128
SSM_WIDTH = 512
SSM_HEADS = 8
SSM_GROUPS = 2
SSM_REP = 4
SSM_HEAD_DIM = 64
D_STATE = 128
CONV_W = 4
CONV_DIM = 1024
CHUNK = 128
D_FF = 2816
EPS = 1e-6

COL_Q, COL_Z, COL_XBC, COL_K, COL_V, COL_DT = 0, 512, 1024, 2048, 2176, 2304
PROJ_COLS = 2432
DT_PAD = 128
KV_COLS = 2 * N_KV_HEADS * HEAD_DIM

FF_CHUNK = 256
N_FF_CHUNKS = D_FF // FF_CHUNK

MIX_TILE = 512
MIX_BLOCKS = MIX_TILE // CHUNK
PROJ_PIECES = ((0, 512), (512, 1024), (1024, 1536), (1536, 2048), (2048, PROJ_COLS))

TAIL_TILE = 512

VMEM_LIMIT = 56 * 1024 * 1024
LAYER_VMEM_LIMIT = 60 * 1024 * 1024

_NT = (((1,), (1,)), ((), ()))


def _bucket_table(dist):
    n = np.maximum(dist, 0)
    exact = N_BUCKETS // 2
    nf = np.maximum(n, 1).astype(np.float32)
    large = exact + (np.log(nf / exact) / math.log(MAX_DISTANCE / exact) * (N_BUCKETS - exact)).astype(np.int32)
    return np.where(n < exact, n, np.minimum(large, N_BUCKETS - 1)).astype(np.int32)


def _prompt_buckets():
    i = np.arange(WINDOW)[:, None]
    j = np.arange(WINDOW)[None, :]
    dist = np.where(j > i, i + WINDOW - j, i - j)
    return _bucket_table(dist)


def _sample_buckets():
    dist = WINDOW - np.arange(WINDOW)[None, :]
    band = (dist >= 0) & (dist < WINDOW)
    row = np.where(band, _bucket_table(dist), -1).astype(np.int32)
    return np.tile(row, (N_HEADS, 1))


def _rms(x, w):
    return x * lax.rsqrt(jnp.mean(x * x, axis=-1, keepdims=True) + EPS) * w


def _silu(x):
    return x * (0.5 + 0.5 * jnp.tanh(0.5 * x))


def _softplus(x):
    return jnp.maximum(x, 0.0) + jnp.log1p(jnp.exp(-jnp.abs(x)))


def _const_spec(shape):
    nd = len(shape)
    return pl.BlockSpec(shape, lambda *_: (0,) * nd, pipeline_mode=pl.Buffered(1))


def _bias_kernel(rb_ref, bp_ref, bs_ref, tp_ref, ts_ref, tn_ref):
    bp = bp_ref[...]
    bs = bs_ref[...]
    rowid = lax.broadcasted_iota(jnp.int32, (N_HEADS, WINDOW), 0)
    ts = jnp.zeros((N_HEADS, WINDOW), F32)
    tn = jnp.zeros((N_HEADS, WINDOW), F32)
    for h in range(N_HEADS):
        tp = jnp.zeros((WINDOW, WINDOW), F32)
        for bk in range(N_BUCKETS):
            v = rb_ref[bk, h]
            tp = jnp.where(bp == bk, v, tp)
            ts = jnp.where((bs == bk) & (rowid == h), v, ts)
        tp_ref[h] = tp
        tn = jnp.where(rowid == h, rb_ref[0, h], tn)
    ts_ref[...] = jnp.where(bs < 0, -jnp.inf, ts)
    tn_ref[...] = tn


def _bias_tables(rel_bias):
    return pl.pallas_call(
        _bias_kernel,
        out_shape=(jax.ShapeDtypeStruct((N_HEADS, WINDOW, WINDOW), F32),
                   jax.ShapeDtypeStruct((N_HEADS, WINDOW), F32),
                   jax.ShapeDtypeStruct((N_HEADS, WINDOW), F32)),
        in_specs=[pl.BlockSpec(memory_space=pltpu.SMEM),
                  pl.BlockSpec(memory_space=pltpu.VMEM),
                  pl.BlockSpec(memory_space=pltpu.VMEM)],
        name="bias_tables",
    )(rel_bias, jnp.asarray(_prompt_buckets()), jnp.asarray(_sample_buckets()))


def _inproj_kernel(x_ref, nw_ref, w_ref, o_ref):
    h = _rms(x_ref[...], nw_ref[...]).astype(BF16)
    o_ref[...] = jnp.dot(h, w_ref[...], preferred_element_type=F32)


def _inproj(x2d, norm_w, w_perm, tm):
    n = x2d.shape[0]
    return pl.pallas_call(
        _inproj_kernel,
        out_shape=jax.ShapeDtypeStruct((n, PROJ_COLS), F32),
        grid=(n // tm,),
        in_specs=[pl.BlockSpec((tm, D_MODEL), lambda i: (i, 0)),
                  _const_spec((1, D_MODEL)),
                  _const_spec((D_MODEL, PROJ_COLS))],
        out_specs=pl.BlockSpec((tm, PROJ_COLS), lambda i: (i, 0)),
        compiler_params=pltpu.CompilerParams(dimension_semantics=("arbitrary",),
                                             vmem_limit_bytes=VMEM_LIMIT),
        name="inproj",
    )(x2d, norm_w, w_perm)


def _head_variants(x, low):
    xr = pltpu.roll(x, HEAD_DIM, axis=1)
    zero = jnp.zeros_like(x)
    return ((jnp.where(low, x, zero).astype(BF16), jnp.where(low, zero, xr).astype(BF16)),
            (jnp.where(low, xr, zero).astype(BF16), jnp.where(low, zero, x).astype(BF16)))


def _gated_group_norm(y, z, nw):
    y = y * _silu(z)
    half = SSM_WIDTH // SSM_GROUPS
    parts = []
    for g in range(SSM_GROUPS):
        yg = y[:, g * half:(g + 1) * half]
        parts.append(yg * lax.rsqrt(jnp.mean(yg * yg, axis=-1, keepdims=True) + EPS))
    return jnp.concatenate(parts, axis=-1) * nw


def _attn_scores(pr, r0, kst_ref, vst_ref):
    w = WINDOW
    rows = slice(r0, r0 + w)
    low = lax.broadcasted_iota(jnp.int32, (w, w), 1) < HEAD_DIM

    for st, c0 in ((kst_ref, COL_K), (vst_ref, COL_V)):
        for g in range(N_KV_HEADS):
            st[g, 0:w, :] = st[g, w:2 * w, :]
            st[g, 2 * w:3 * w, :] = st[g, 3 * w:4 * w, :]
        var = _head_variants(pr[rows, c0:c0 + w], low)
        for g in range(N_KV_HEADS):
            st[g, w:2 * w, :] = var[g][0]
            st[g, 3 * w:4 * w, :] = var[g][1]

    scores = []
    for pp in range(N_HEADS // 2):
        q2 = pr[rows, COL_Q + pp * w:COL_Q + (pp + 1) * w].astype(BF16)
        scores.append(lax.dot_general(q2, kst_ref[pp // (KV_REP // 2)], _NT, preferred_element_type=F32))
    return scores


def _attn_softmax(scores, seq_start, sink_ref, tab_ref):
    w = WINDOW
    row = lax.broadcasted_iota(jnp.int32, (w, w), 0)
    lane = lax.broadcasted_iota(jnp.int32, (w, w), 1)
    upper = lane > row
    if seq_start is not None:
        no_prev = (lane - row) > jnp.where(seq_start, 0, w)
    p4s = []
    for pp in range(N_HEADS // 2):
        s4 = scores[pp]
        probs = []
        for half in range(2):
            hh = 2 * pp + half
            sp = s4[:, 2 * w * half:2 * w * half + w]
            sc = s4[:, 2 * w * half + w:2 * w * (half + 1)]
            s = jnp.where(upper, sp, sc) * ATTN_SCALE + tab_ref[hh]
            if seq_start is not None:
                s = jnp.where(no_prev, -jnp.inf, s)
            sk = sink_ref[hh]
            m = jnp.maximum(jnp.max(s, axis=-1, keepdims=True), sk)
            e = jnp.exp(s - m)
            den = jnp.sum(e, axis=-1, keepdims=True) + jnp.exp(sk - m)
            p = e / den
            zero = jnp.zeros_like(p)
            probs.append(jnp.where(upper, p, zero).astype(BF16))
            probs.append(jnp.where(upper, zero, p).astype(BF16))
        p4s.append(jnp.concatenate(probs, axis=1))
    return p4s


def _attn_values(p4s, r0, vst_ref, out_ref):
    w = WINDOW
    for pp in range(N_HEADS // 2):
        out_ref[r0:r0 + w, pp * w:(pp + 1) * w] = jnp.dot(p4s[pp], vst_ref[pp // (KV_REP // 2)],
                                                          preferred_element_type=F32).astype(out_ref.dtype)


def _ssd_decay(pr, r0, dtb_ref, alog_ref):
    rows = slice(r0, r0 + CHUNK)
    row = lax.broadcasted_iota(jnp.int32, (CHUNK, CHUNK), 0)
    lane = lax.broadcasted_iota(jnp.int32, (CHUNK, CHUNK), 1)
    dt_t = _softplus(pr[rows, COL_DT:COL_DT + DT_PAD].T[0:SSM_HEADS, :] + dtb_ref[...])
    a_t = dt_t * (-jnp.exp(alog_ref[...]))
    cs_t = jnp.dot(a_t, (row <= lane).astype(F32), precision=lax.Precision.HIGHEST,
                   preferred_element_type=F32)
    cs_end = cs_t[:, CHUNK - 1:CHUNK]
    w_t = jnp.exp(cs_end - cs_t) * dt_t
    dec_end = jnp.exp(cs_end)
    col = jnp.concatenate([cs_t, jnp.exp(cs_t), jnp.zeros((CHUNK - 2 * SSM_HEADS, CHUNK), F32)], axis=0).T
    return dt_t, cs_t, w_t, dec_end, col


def _ssd_conv(pr, r0, seq_start, cw_ref, cb_ref, hist_ref):
    x = pr[r0:r0 + CHUNK, COL_XBC:COL_XBC + CONV_DIM]
    hist = hist_ref[...]
    if seq_start is not None:
        hist = jnp.where(seq_start, 0.0, hist)
    xx = jnp.concatenate([hist, x], axis=0)
    tot = pltpu.roll(xx, CONV_W - 1, axis=0)[8:, :] * cw_ref[0:1, :]
    for j in range(1, CONV_W - 1):
        tot = tot + pltpu.roll(xx, CONV_W - 1 - j, axis=0)[8:, :] * cw_ref[j:j + 1, :]
    tot = tot + x * cw_ref[CONV_W - 1:CONV_W, :]
    hist_ref[...] = x[CHUNK - 8:, :]
    return _silu(cb_ref[...] + tot)


def _ssd_matmuls(pr, r0, seq_start, decay_terms, xc, dsk_ref, nw_ref, st_ref, out_ref):
    hd = SSM_HEAD_DIM
    rows = slice(r0, r0 + CHUNK)
    dt_t, cs_t, w_t, dec_end, col = decay_terms
    row = lax.broadcasted_iota(jnp.int32, (CHUNK, CHUNK), 0)
    lane = lax.broadcasted_iota(jnp.int32, (CHUNK, CHUNK), 1)
    causal = row >= lane
    low = lane < hd
    xs = xc[:, :SSM_WIDTH]
    bm = xc[:, SSM_WIDTH:SSM_WIDTH + SSM_GROUPS * D_STATE]
    cm = xc[:, SSM_WIDTH + SSM_GROUPS * D_STATE:]

    groups = range(SSM_GROUPS)
    bgs = [bm[:, g * D_STATE:(g + 1) * D_STATE].astype(BF16) for g in groups]
    cgs = [cm[:, g * D_STATE:(g + 1) * D_STATE].astype(BF16) for g in groups]
    h_prevs = []
    for g in groups:
        hp = st_ref[g * SSM_REP:(g + 1) * SSM_REP].reshape(SSM_REP * hd, D_STATE)
        if seq_start is not None:
            hp = jnp.where(seq_start, 0.0, hp)
        h_prevs.append(hp)
    cbs = [lax.dot_general(cgs[g], bgs[g], _NT, preferred_element_type=F32) for g in groups]
    y_offs = [lax.dot_general(cgs[g], h_prevs[g].astype(BF16), _NT, preferred_element_type=F32) for g in groups]

    for g in groups:
        xt, wts, decs = [], [], []
        for pr2 in range(SSM_REP // 2):
            h0 = g * SSM_REP + 2 * pr2
            xt.append(xs[:, h0 * hd:(h0 + 2) * hd].T)
            for h in (h0, h0 + 1):
                wts.append(jnp.broadcast_to(w_t[h:h + 1, :], (hd, CHUNK)))
                decs.append(jnp.broadcast_to(dec_end[h:h + 1, :], (hd, D_STATE)))
        xw = (jnp.concatenate(xt, axis=0) * jnp.concatenate(wts, axis=0)).astype(BF16)
        st = jnp.dot(xw, bgs[g], preferred_element_type=F32)
        h_new = h_prevs[g] * jnp.concatenate(decs, axis=0) + st
        st_ref[g * SSM_REP:(g + 1) * SSM_REP] = h_new.reshape(SSM_REP, hd, D_STATE)

    ys = []
    for g in groups:
        for pr2 in range(SSM_REP // 2):
            h0 = g * SSM_REP + 2 * pr2
            xpair = xs[:, h0 * hd:(h0 + 2) * hd]
            zero = jnp.zeros_like(xpair)
            m = []
            for h in (h0, h0 + 1):
                decay = jnp.exp(jnp.where(causal, col[:, h:h + 1] - cs_t[h:h + 1, :], -jnp.inf))
                m.append((cbs[g] * decay * dt_t[h:h + 1, :]).astype(BF16))
            y_diag = (jnp.dot(m[0], jnp.where(low, xpair, zero).astype(BF16), preferred_element_type=F32)
                      + jnp.dot(m[1], jnp.where(low, zero, xpair).astype(BF16), preferred_element_type=F32))
            e0 = jnp.broadcast_to(col[:, SSM_HEADS + h0:SSM_HEADS + h0 + 1], (CHUNK, CHUNK))
            e1 = jnp.broadcast_to(col[:, SSM_HEADS + h0 + 1:SSM_HEADS + h0 + 2], (CHUNK, CHUNK))
            ys.append(y_diag + y_offs[g][:, 2 * pr2 * hd:(2 * pr2 + 2) * hd] * jnp.where(low, e0, e1))
    y = jnp.concatenate(ys, axis=-1) + xs * dsk_ref[...]
    out_ref[rows, ATTN_WIDTH:ATTN_WIDTH + SSM_WIDTH] = _gated_group_norm(
        y, pr[rows, COL_Z:COL_Z + SSM_WIDTH], nw_ref[...]).astype(out_ref.dtype)


def _prompt_layer_kernel(tiles_per_seq, n_tiles, sink_ref, x_ref, xres_ref, n1_ref, w_ref, tab_ref, cw_ref,
                         cb_ref, dtb_ref, alog_ref, dsk_ref, nw_ref, wo_ref, n2_ref, wg_ref, wu_ref, wd_ref,
                         fn_ref, y_ref, kv_ref, conv_ref, ssm_ref,
                         proj_ref, mix_ref, kst_ref, vst_ref, hist_ref, st_ref):
    t = pl.program_id(0)

    @pl.when(t == 0)
    def _():
        kst_ref[...] = jnp.zeros_like(kst_ref)
        vst_ref[...] = jnp.zeros_like(vst_ref)
        hist_ref[...] = jnp.zeros_like(hist_ref)
        st_ref[...] = jnp.zeros_like(st_ref)

    slot = lax.rem(t, 2)
    pw = proj_ref.at[slot]
    pr = proj_ref.at[1 - slot]
    mw = mix_ref.at[1 - slot]
    mr = mix_ref.at[slot]
    seq_start = lax.rem(t + tiles_per_seq - 1, tiles_per_seq) == 0

    def normed_input():
        return _rms(x_ref[...], n1_ref[...]).astype(BF16)

    def project(h, piece):
        lo, hi = PROJ_PIECES[piece]
        pw[:, lo:hi] = jnp.dot(h, w_ref[:, lo:hi], preferred_element_type=F32)

    def ffn_input():
        x1 = xres_ref[...] + jnp.dot(mr[...], wo_ref[...], preferred_element_type=F32)
        return x1, _rms(x1, n2_ref[...]).astype(BF16)

    def ffn(h2, acc, chunk):
        cols = slice(chunk * FF_CHUNK, (chunk + 1) * FF_CHUNK)
        gate = jnp.dot(h2, wg_ref[:, cols], preferred_element_type=F32)
        up = jnp.dot(h2, wu_ref[:, cols], preferred_element_type=F32)
        act = (_silu(gate) * up).astype(BF16)
        part = jnp.dot(act, wd_ref[cols, :], preferred_element_type=F32)
        acc[0] = part if acc[0] is None else acc[0] + part

    def mix_block(j, between):
        r0 = j * CHUNK
        start = seq_start if j == 0 else None
        decay_terms = _ssd_decay(pr, r0, dtb_ref, alog_ref)
        scores = _attn_scores(pr, r0, kst_ref, vst_ref)
        between[0]()
        xc = _ssd_conv(pr, r0, start, cw_ref, cb_ref, hist_ref)
        between[1]()
        p4s = _attn_softmax(scores, start, sink_ref, tab_ref)
        _attn_values(p4s, r0, vst_ref, mw)
        between[2]()
        _ssd_matmuls(pr, r0, start, decay_terms, xc, dsk_ref, nw_ref, st_ref, mw)
        between[3]()

    def sequence_outputs():
        kv_ref[0] = pr[MIX_TILE - WINDOW:, COL_K:COL_K + KV_COLS]
        conv_ref[0] = pr[MIX_TILE - (CONV_W - 1):, COL_XBC:COL_XBC + CONV_DIM]
        ssm_ref[0] = st_ref[...]

    steady = (t >= 2) & (t <= n_tiles)

    @pl.when(steady)
    def _():
        h = normed_input()
        x1, h2 = ffn_input()
        acc = [None]
        pieces = ([functools.partial(project, h, p) for p in range(len(PROJ_PIECES))]
                  + [functools.partial(ffn, h2, acc, c) for c in range(N_FF_CHUNKS)])
        order = [0, 5, 6, 7, 1, 8, 9, 10, 2, 11, 12, 13, 3, 14, 15, 4]
        assert len(order) == 4 * MIX_BLOCKS
        for j in range(MIX_BLOCKS):
            mix_block(j, [pieces[i] for i in order[4 * j:4 * j + 4]])
        y_ref[...] = _rms(x1 + acc[0], fn_ref[...])
        sequence_outputs()

    @pl.when(jnp.logical_not(steady) & (t < n_tiles))
    def _():
        h = normed_input()
        for p in range(len(PROJ_PIECES)):
            project(h, p)

    @pl.when(jnp.logical_not(steady) & (t >= 1) & (t <= n_tiles))
    def _():
        for j in range(MIX_BLOCKS):
            mix_block(j, [lambda: None] * 4)
        sequence_outputs()

    @pl.when(jnp.logical_not(steady) & (t >= 2))
    def _():
        x1, h2 = ffn_input()
        acc = [None]
        for c in range(N_FF_CHUNKS):
            ffn(h2, acc, c)
        y_ref[...] = _rms(x1 + acc[0], fn_ref[...])


def _prompt_layer(x2d, n1, w_perm, sinks, tab, conv_w, conv_b, dtb_t, alog_t, dsk, nw, wo, n2, wg, wu, wd, fn,
                  batch, seq):
    assert seq % MIX_TILE == 0
    tiles_per_seq = seq // MIX_TILE
    nt = batch * tiles_per_seq

    def mixed_tile(t):
        return jnp.clip(t - 1, 0, nt - 1)

    def seq_of(t):
        return mixed_tile(t) // tiles_per_seq

    def ffn_tile(t):
        return jnp.maximum(t - 2, 0)

    return pl.pallas_call(
        functools.partial(_prompt_layer_kernel, tiles_per_seq, nt),
        out_shape=(jax.ShapeDtypeStruct((batch * seq, D_MODEL), F32),
                   jax.ShapeDtypeStruct((batch, WINDOW, KV_COLS), F32),
                   jax.ShapeDtypeStruct((batch, CONV_W - 1, CONV_DIM), F32),
                   jax.ShapeDtypeStruct((batch, SSM_HEADS, SSM_HEAD_DIM, D_STATE), F32)),
        grid=(nt + 2,),
        in_specs=[pl.BlockSpec(memory_space=pltpu.SMEM),
                  pl.BlockSpec((MIX_TILE, D_MODEL), lambda t: (jnp.minimum(t, nt - 1), 0)),
                  pl.BlockSpec((MIX_TILE, D_MODEL), lambda t: (ffn_tile(t), 0)),
                  _const_spec((1, D_MODEL)),
                  _const_spec((D_MODEL, PROJ_COLS)),
                  _const_spec((N_HEADS, WINDOW, WINDOW)),
                  _const_spec((CONV_W, CONV_DIM)),
                  _const_spec((1, CONV_DIM)),
                  _const_spec((SSM_HEADS, CHUNK)),
                  _const_spec((SSM_HEADS, CHUNK)),
                  _const_spec((1, SSM_WIDTH)),
                  _const_spec((1, SSM_WIDTH)),
                  _const_spec((D_MODEL, D_MODEL)),
                  _const_spec((1, D_MODEL)),
                  _const_spec((D_MODEL, D_FF)),
                  _const_spec((D_MODEL, D_FF)),
                  _const_spec((D_FF, D_MODEL)),
                  _const_spec((1, D_MODEL))],
        out_specs=(pl.BlockSpec((MIX_TILE, D_MODEL), lambda t: (ffn_tile(t), 0)),
                   pl.BlockSpec((1, WINDOW, KV_COLS), lambda t: (seq_of(t), 0, 0)),
                   pl.BlockSpec((1, CONV_W - 1, CONV_DIM), lambda t: (seq_of(t), 0, 0)),
                   pl.BlockSpec((1, SSM_HEADS, SSM_HEAD_DIM, D_STATE), lambda t: (seq_of(t), 0, 0, 0))),
        scratch_shapes=[pltpu.VMEM((2, MIX_TILE, PROJ_COLS), F32),
                        pltpu.VMEM((2, MIX_TILE, ATTN_WIDTH + SSM_WIDTH), BF16),
                        pltpu.VMEM((N_KV_HEADS, 4 * WINDOW, WINDOW), BF16),
                        pltpu.VMEM((N_KV_HEADS, 4 * WINDOW, WINDOW), BF16),
                        pltpu.VMEM((8, CONV_DIM), F32),
                        pltpu.VMEM((SSM_HEADS, SSM_HEAD_DIM, D_STATE), F32)],
        compiler_params=pltpu.CompilerParams(dimension_semantics=("arbitrary",),
                                             vmem_limit_bytes=LAYER_VMEM_LIMIT),
        name="prompt_layer",
    )(sinks, x2d, x2d, n1, w_perm, tab, conv_w, conv_b, dtb_t, alog_t, dsk, nw, wo, n2, wg, wu, wd, fn)


def _tail_kernel(x_ref, mix_ref, wo_ref, n2_ref, wg_ref, wu_ref, wd_ref, fn_ref, o_ref):
    x1 = x_ref[...] + jnp.dot(mix_ref[...].astype(BF16), wo_ref[...], preferred_element_type=F32)
    h2 = _rms(x1, n2_ref[...]).astype(BF16)
    acc = None
    for j in range(N_FF_CHUNKS):
        cols = slice(j * FF_CHUNK, (j + 1) * FF_CHUNK)
        gate = jnp.dot(h2, wg_ref[:, cols], preferred_element_type=F32)
        up = jnp.dot(h2, wu_ref[:, cols], preferred_element_type=F32)
        act = (_silu(gate) * up).astype(BF16)
        part = jnp.dot(act, wd_ref[cols, :], preferred_element_type=F32)
        acc = part if acc is None else acc + part
    o_ref[...] = _rms(x1 + acc, fn_ref[...])


def _tail(x2d, mix, wo, n2, wg, wu, wd, fn, tm):
    n = x2d.shape[0]
    return pl.pallas_call(
        _tail_kernel,
        out_shape=jax.ShapeDtypeStruct((n, D_MODEL), F32),
        grid=(n // tm,),
        in_specs=[pl.BlockSpec((tm, D_MODEL), lambda i: (i, 0)),
                  pl.BlockSpec((tm, ATTN_WIDTH + SSM_WIDTH), lambda i: (i, 0)),
                  _const_spec((D_MODEL, D_MODEL)),
                  _const_spec((1, D_MODEL)),
                  _const_spec((D_MODEL, D_FF)),
                  _const_spec((D_MODEL, D_FF)),
                  _const_spec((D_FF, D_MODEL)),
                  _const_spec((1, D_MODEL))],
        out_specs=pl.BlockSpec((tm, D_MODEL), lambda i: (i, 0)),
        compiler_params=pltpu.CompilerParams(dimension_semantics=("arbitrary",),
                                             vmem_limit_bytes=VMEM_LIMIT),
        name="outproj_ffn",
    )(x2d, mix, wo, n2, wg, wu, wd, fn)


SAMPLE_BT = 8
ATTN_SAMPLE_BT = 16


def _attn_sample_kernel(q_ref, kn_ref, vn_ref, ck_ref, cv_ref, ts_ref, tn_ref, sink_ref,
                        a_ref, nk_ref, nv_ref):
    rowid = lax.broadcasted_iota(jnp.int32, (N_HEADS, HEAD_DIM), 0)
    ts = ts_ref[...]
    tn = tn_ref[:, 0:1]
    sink = sink_ref[:, 0:1]
    pad = jnp.zeros((WINDOW - ATTN_SAMPLE_BT, WINDOW), F32)
    kn_t = jnp.concatenate([kn_ref[...], pad], axis=0).T
    vn_t = jnp.concatenate([vn_ref[...], pad], axis=0).T
    newest = lax.broadcasted_iota(jnp.int32, (WINDOW, WINDOW), 1) == WINDOW - 1

    def shifted(cache_ref, new_t, bb):
        return jnp.where(newest, new_t[:, bb:bb + 1], pltpu.roll(cache_ref[bb], WINDOW - 1, axis=1))

    qbds, scores = [], []
    for bb in range(ATTN_SAMPLE_BT):
        qb = q_ref[bb]
        qbd = jnp.concatenate([jnp.where(rowid < KV_REP, qb, 0.0), jnp.where(rowid >= KV_REP, qb, 0.0)], axis=1)
        qbds.append(qbd)
        scores.append(jnp.dot(qbd.astype(BF16), ck_ref[bb].astype(BF16), preferred_element_type=F32))
    probs, new_terms = [], []
    for bb in range(ATTN_SAMPLE_BT):
        s_c = scores[bb] * ATTN_SCALE + ts
        s_n = jnp.sum(qbds[bb] * kn_ref[bb:bb + 1, :], axis=-1, keepdims=True) * ATTN_SCALE + tn
        m = jnp.maximum(jnp.maximum(jnp.max(s_c, axis=-1, keepdims=True), s_n), sink)
        e_c = jnp.exp(s_c - m)
        e_n = jnp.exp(s_n - m)
        den = jnp.sum(e_c, axis=-1, keepdims=True) + e_n + jnp.exp(sink - m)
        probs.append((e_c / den).astype(BF16))
        new_terms.append((e_n / den) * vn_ref[bb:bb + 1, :])
    for bb in range(ATTN_SAMPLE_BT):
        o = lax.dot_general(probs[bb], cv_ref[bb].astype(BF16), _NT, preferred_element_type=F32) + new_terms[bb]
        a_ref[bb] = jnp.where(rowid < KV_REP, o[:, :HEAD_DIM], o[:, HEAD_DIM:])
    for bb in range(ATTN_SAMPLE_BT):
        nk_ref[bb] = shifted(ck_ref, kn_t, bb)
        nv_ref[bb] = shifted(cv_ref, vn_t, bb)


def _attn_sample(q3, proj, ck, cv, ts, tn, sink_b):
    nb = q3.shape[0]
    bt = ATTN_SAMPLE_BT
    kv = N_KV_HEADS * HEAD_DIM
    return pl.pallas_call(
        _attn_sample_kernel,
        out_shape=(jax.ShapeDtypeStruct((nb, N_HEADS, HEAD_DIM), F32),
                   jax.ShapeDtypeStruct((nb, WINDOW, kv), F32),
                   jax.ShapeDtypeStruct((nb, WINDOW, kv), F32)),
        grid=(nb // bt,),
        in_specs=[pl.BlockSpec((bt, N_HEADS, HEAD_DIM), lambda i: (i, 0, 0)),
                  pl.BlockSpec((bt, kv), lambda i: (i, COL_K // kv)),
                  pl.BlockSpec((bt, kv), lambda i: (i, COL_V // kv)),
                  pl.BlockSpec((bt, WINDOW, kv), lambda i: (i, 0, 0)),
                  pl.BlockSpec((bt, WINDOW, kv), lambda i: (i, 0, 0)),
                  _const_spec((N_HEADS, WINDOW)),
                  _const_spec((N_HEADS, WINDOW)),
                  _const_spec((N_HEADS, WINDOW))],
        out_specs=(pl.BlockSpec((bt, N_HEADS, HEAD_DIM), lambda i: (i, 0, 0)),
                   pl.BlockSpec((bt, WINDOW, kv), lambda i: (i, 0, 0)),
                   pl.BlockSpec((bt, WINDOW, kv), lambda i: (i, 0, 0))),
        compiler_params=pltpu.CompilerParams(dimension_semantics=("arbitrary",),
                                             vmem_limit_bytes=VMEM_LIMIT),
        name="attn_sample",
    )(q3, proj, proj, ck, cv, ts, tn, sink_b)


def _ssm_sample_prep_kernel(xbc_ref, dt_ref, cst_ref, cw_ref, cb_ref, dtb_ref, alog_ref,
                            xs_ref, bm_ref, cm_ref, xdt_t_ref, dec_ref, nc_ref):
    xbc = xbc_ref[...]
    tot = cst_ref[0] * cw_ref[0:1, :]
    tot = tot + cst_ref[1] * cw_ref[1:2, :]
    tot = tot + cst_ref[2] * cw_ref[2:3, :]
    tot = tot + xbc * cw_ref[3:4, :]
    xc = _silu(cb_ref[...] + tot)
    nc_ref[0] = cst_ref[1]
    nc_ref[1] = cst_ref[2]
    nc_ref[2] = xbc
    xs = xc[:, :SSM_WIDTH]
    xs_ref[...] = xs
    bm_ref[...] = xc[:, SSM_WIDTH:SSM_WIDTH + SSM_GROUPS * D_STATE]
    cm_ref[...] = xc[:, SSM_WIDTH + SSM_GROUPS * D_STATE:]
    dt = _softplus(dt_ref[...] + dtb_ref[...])
    dec_ref[...] = jnp.exp(dt * (-jnp.exp(alog_ref[...])))
    dt_t = dt.T
    for k in range(SSM_WIDTH // 128):
        xt = xs[:, k * 128:(k + 1) * 128].T
        for half in range(2):
            h = 2 * k + half
            lo = h * SSM_HEAD_DIM
            xdt_t_ref[lo:lo + SSM_HEAD_DIM, :] = xt[half * SSM_HEAD_DIM:(half + 1) * SSM_HEAD_DIM, :] * dt_t[h:h + 1, :]


def _ssm_sample_prep(proj, conv_state_t, conv_w, conv_b, dtb, alog):
    nb = proj.shape[0]
    return pl.pallas_call(
        _ssm_sample_prep_kernel,
        out_shape=(jax.ShapeDtypeStruct((nb, SSM_WIDTH), F32),
                   jax.ShapeDtypeStruct((nb, SSM_GROUPS * D_STATE), F32),
                   jax.ShapeDtypeStruct((nb, SSM_GROUPS * D_STATE), F32),
                   jax.ShapeDtypeStruct((SSM_WIDTH, nb), F32),
                   jax.ShapeDtypeStruct((nb, DT_PAD), F32),
                   jax.ShapeDtypeStruct((CONV_W - 1, nb, CONV_DIM), F32)),
        grid=(1,),
        in_specs=[pl.BlockSpec((nb, CONV_DIM), lambda i: (0, COL_XBC // CONV_DIM)),
                  pl.BlockSpec((nb, DT_PAD), lambda i: (0, COL_DT // DT_PAD)),
                  _const_spec((CONV_W - 1, nb, CONV_DIM)),
                  _const_spec((CONV_W, CONV_DIM)),
                  _const_spec((1, CONV_DIM)),
                  _const_spec((1, DT_PAD)),
                  _const_spec((1, DT_PAD))],
        out_specs=(_const_spec((nb, SSM_WIDTH)),
                   _const_spec((nb, SSM_GROUPS * D_STATE)),
                   _const_spec((nb, SSM_GROUPS * D_STATE)),
                   _const_spec((SSM_WIDTH, nb)),
                   _const_spec((nb, DT_PAD)),
                   _const_spec((CONV_W - 1, nb, CONV_DIM))),
        compiler_params=pltpu.CompilerParams(dimension_semantics=("arbitrary",),
                                             vmem_limit_bytes=VMEM_LIMIT),
        name="ssm_sample_prep",
    )(proj, proj, conv_state_t, conv_w, conv_b, dtb, alog)


def _ssm_sample_state_kernel(dec_ref, xdt_t_ref, bm_ref, cm_ref, xs_ref, z_ref, dsk_ref, nw_ref, st_ref,
                             nst_ref, s_ref, yt_ref):
    i = pl.program_id(0)
    last = pl.num_programs(0) - 1
    rows = SSM_REP * SSM_HEAD_DIM

    @pl.when(i == 0)
    def _():
        yt_ref[...] = jnp.zeros_like(yt_ref)

    lane = lax.broadcasted_iota(jnp.int32, (rows, 128), 1)
    for bb in range(SAMPLE_BT):
        b = i * SAMPLE_BT + bb
        mine = lane == b
        for g in range(SSM_GROUPS):
            bg = bm_ref[:, g * D_STATE:(g + 1) * D_STATE].astype(BF16)
            cg = cm_ref[:, g * D_STATE:(g + 1) * D_STATE].astype(BF16)
            xsel = jnp.where(mine, xdt_t_ref[g * rows:(g + 1) * rows, :], 0.0).astype(BF16)
            outer = jnp.dot(xsel, bg, preferred_element_type=F32)
            new = []
            for r in range(SSM_REP):
                h = g * SSM_REP + r
                hn = st_ref[bb, h] * dec_ref[b * SSM_HEADS + h] + outer[r * SSM_HEAD_DIM:(r + 1) * SSM_HEAD_DIM, :]
                nst_ref[bb, h] = hn
                new.append(hn)
            hcat = jnp.concatenate(new, axis=0).astype(BF16)
            res = lax.dot_general(hcat, cg, _NT, preferred_element_type=F32)
            yt_ref[g * rows:(g + 1) * rows, :] += jnp.where(mine, res, 0.0)

    @pl.when(i == last)
    def _():
        y = jnp.concatenate([yt_ref[k * 128:(k + 1) * 128, :].T for k in range(SSM_WIDTH // 128)], axis=1)
        y = y + xs_ref[...] * dsk_ref[...]
        s_ref[...] = _gated_group_norm(y, z_ref[...], nw_ref[...])


def _ssm_sample_state(dec_flat, xdt_t, bm, cm, xs, proj, dsk, nw, state):
    nb = state.shape[0]
    bt = SAMPLE_BT
    return pl.pallas_call(
        _ssm_sample_state_kernel,
        out_shape=(jax.ShapeDtypeStruct(state.shape, F32),
                   jax.ShapeDtypeStruct((nb, SSM_WIDTH), F32)),
        grid=(nb // bt,),
        in_specs=[pl.BlockSpec(memory_space=pltpu.SMEM),
                  _const_spec((SSM_WIDTH, nb)),
                  _const_spec((nb, SSM_GROUPS * D_STATE)),
                  _const_spec((nb, SSM_GROUPS * D_STATE)),
                  _const_spec((nb, SSM_WIDTH)),
                  pl.BlockSpec((nb, SSM_WIDTH), lambda i: (0, COL_Z // SSM_WIDTH)),
                  _const_spec((1, SSM_WIDTH)),
                  _const_spec((1, SSM_WIDTH)),
                  pl.BlockSpec((bt, SSM_HEADS, SSM_HEAD_DIM, D_STATE), lambda i: (i, 0, 0, 0))],
        out_specs=(pl.BlockSpec((bt, SSM_HEADS, SSM_HEAD_DIM, D_STATE), lambda i: (i, 0, 0, 0)),
                   pl.BlockSpec((nb, SSM_WIDTH), lambda i: (0, 0))),
        scratch_shapes=[pltpu.VMEM((SSM_WIDTH, nb), F32)],
        compiler_params=pltpu.CompilerParams(dimension_semantics=("arbitrary",),
                                             vmem_limit_bytes=VMEM_LIMIT),
        name="ssm_sample_state",
    )(dec_flat, xdt_t, bm, cm, xs, proj, dsk, nw, state)


def _pad_lanes(v, width):
    return jnp.pad(v.reshape(1, -1), ((0, 0), (0, width - v.shape[-1])))


def kernel(x_prompt, x_sample, cache_k, cache_v, state_conv, state_ssm, rel_bias, norm1_w, w_in, attn_sinks,
           conv_w, conv_b, dt_bias, A_log, D_skip, ssm_norm_w, w_out, norm2_w, w_gate, w_up, w_down, final_norm_w):
    depth = norm1_w.shape[0]
    assert depth == 1, "single-layer trunk"
    batch, seq, _ = x_prompt.shape
    nb = x_sample.shape[0]
    assert x_sample.shape[1] == 1 and nb == 128

    wi = w_in[0]
    q_c, k_c, v_c, z_c, xbc_c, dt_c = (wi[:, 0:512], wi[:, 512:640], wi[:, 640:768], wi[:, 768:1280],
                                       wi[:, 1280:2304], wi[:, 2304:2312])
    w_perm = jnp.concatenate([q_c, z_c, xbc_c, k_c, v_c, jnp.pad(dt_c, ((0, 0), (0, DT_PAD - SSM_HEADS)))],
                             axis=1).astype(BF16)
    wo = w_out[0].astype(BF16)
    wg = w_gate[0].astype(BF16)
    wu = w_up[0].astype(BF16)
    wd = w_down[0].astype(BF16)
    n1 = norm1_w[0].reshape(1, D_MODEL)
    n2 = norm2_w[0].reshape(1, D_MODEL)
    fn = final_norm_w.reshape(1, D_MODEL)
    cw = conv_w[0]
    cb = conv_b[0].reshape(1, CONV_DIM)
    dtb = _pad_lanes(dt_bias[0], DT_PAD)
    alog = _pad_lanes(A_log[0], DT_PAD)
    dtb_t = jnp.broadcast_to(dt_bias[0][:, None], (SSM_HEADS, CHUNK))
    alog_t = jnp.broadcast_to(A_log[0][:, None], (SSM_HEADS, CHUNK))
    dsk = jnp.repeat(D_skip[0], SSM_HEAD_DIM).reshape(1, SSM_WIDTH)
    nw = ssm_norm_w[0].reshape(1, SSM_WIDTH)
    sinks = attn_sinks[0]
    sink_b = jnp.broadcast_to(sinks[:, None], (N_HEADS, WINDOW))

    tab_p, tab_s, tab_n = _bias_tables(rel_bias)

    xp2 = x_prompt.reshape(batch * seq, D_MODEL)
    y_p, kv_p, conv_p, ssm_p = _prompt_layer(xp2, n1, w_perm, sinks, tab_p, cw, cb, dtb_t, alog_t, dsk, nw,
                                              wo, n2, wg, wu, wd, fn, batch, seq)
    kv_p = kv_p.reshape(batch, WINDOW, 2, N_KV_HEADS, HEAD_DIM)

    xs2 = x_sample.reshape(nb, D_MODEL)
    proj_s = _inproj(xs2, n1, w_perm, nb)
    q3 = proj_s[:, COL_Q:COL_Q + ATTN_WIDTH].reshape(nb, N_HEADS, HEAD_DIM)
    ck = jnp.transpose(cache_k[0], (0, 2, 3, 1)).reshape(nb, N_KV_HEADS * HEAD_DIM, WINDOW)
    cv = jnp.transpose(cache_v[0], (0, 2, 3, 1)).reshape(nb, N_KV_HEADS * HEAD_DIM, WINDOW)
    a_s3, nk, nv = _attn_sample(q3, proj_s, ck, cv, tab_s, tab_n, sink_b)
    conv_t = jnp.transpose(state_conv[0], (1, 0, 2))
    xs_s, bm_s, cm_s, xdt_t, dec, nconv_t = _ssm_sample_prep(proj_s, conv_t, cw, cb, dtb, alog)
    dec_flat = dec[:, :SSM_HEADS].reshape(nb * SSM_HEADS)
    nssm, s_s = _ssm_sample_state(dec_flat, xdt_t, bm_s, cm_s, xs_s, proj_s, dsk, nw, state_ssm[0])
    mix_s = jnp.concatenate([a_s3.reshape(nb, ATTN_WIDTH), s_s], axis=1)
    y_s = _tail(xs2, mix_s, wo, n2, wg, wu, wd, fn, nb)

    return (y_p.reshape(batch, seq, D_MODEL),
            y_s.reshape(nb, 1, D_MODEL),
            kv_p[:, :, 0][None],
            kv_p[:, :, 1][None],
            conv_p[None],
            ssm_p[None],
            jnp.transpose(nk.reshape(nb, N_KV_HEADS, HEAD_DIM, WINDOW), (0, 3, 1, 2))[None],
            jnp.transpose(nv.reshape(nb, N_KV_HEADS, HEAD_DIM, WINDOW), (0, 3, 1, 2))[None],
            jnp.transpose(nconv_t, (1, 0, 2))[None],
            nssm[None])
```

```python
import functools
import math

import numpy as np
import jax
import jax.numpy as jnp
from jax import lax
from jax.experimental import pallas as pl
from jax.experimental.pallas import tpu as pltpu

F32 = jnp.float32
BF16 = jnp.bfloat16

D_MODEL = 1024
HEAD_DIM = 64
N_HEADS = 8
N_KV_HEADS = 2
KV_REP = 4
WINDOW = 128
ATTN_WIDTH = 512
ATTN_SCALE = HEAD_DIM ** -0.5
N_BUCKETS = 32
MAX_DISTANCE = 128
SSM_WIDTH = 512
SSM_HEADS = 8
SSM_GROUPS = 2
SSM_REP = 4
SSM_HEAD_DIM = 64
D_STATE = 128
CONV_W = 4
CONV_DIM = 1024
CHUNK = 128
D_FF = 2816
EPS = 1e-6

COL_Q, COL_Z, COL_XBC, COL_K, COL_V, COL_DT = 0, 512, 1024, 2048, 2176, 2304
PROJ_COLS = 2432
DT_PAD = 128
KV_COLS = 2 * N_KV_HEADS * HEAD_DIM

FF_CHUNK = 256
N_FF_CHUNKS = D_FF // FF_CHUNK

MIX_TILE = 512
MIX_BLOCKS = MIX_TILE // CHUNK
PROJ_PIECES = ((0, 512), (512, 1024), (1024, 1536), (1536, 2048), (2048, PROJ_COLS))

TAIL_TILE = 512

VMEM_LIMIT = 56 * 1024 * 1024
LAYER_VMEM_LIMIT = 60 * 1024 * 1024

_NT = (((1,), (1,)), ((), ()))


def _bucket_table(dist):
    n = np.maximum(dist, 0)
    exact = N_BUCKETS // 2
    nf = np.maximum(n, 1).astype(np.float32)
    large = exact + (np.log(nf / exact) / math.log(MAX_DISTANCE / exact) * (N_BUCKETS - exact)).astype(np.int32)
    return np.where(n < exact, n, np.minimum(large, N_BUCKETS - 1)).astype(np.int32)


def _prompt_buckets():
    i = np.arange(WINDOW)[:, None]
    j = np.arange(WINDOW)[None, :]
    dist = np.where(j > i, i + WINDOW - j, i - j)
    return _bucket_table(dist)


def _sample_buckets():
    dist = WINDOW - np.arange(WINDOW)[None, :]
    band = (dist >= 0) & (dist < WINDOW)
    row = np.where(band, _bucket_table(dist), -1).astype(np.int32)
    return np.tile(row, (N_HEADS, 1))


def _rms(x, w):
    return x * lax.rsqrt(jnp.mean(x * x, axis=-1, keepdims=True) + EPS) * w


def _silu(x):
    return x * (0.5 + 0.5 * jnp.tanh(0.5 * x))


def _softplus(x):
    return jnp.maximum(x, 0.0) + jnp.log1p(jnp.exp(-jnp.abs(x)))


def _const_spec(shape):
    nd = len(shape)
    return pl.BlockSpec(shape, lambda *_: (0,) * nd, pipeline_mode=pl.Buffered(1))


def _bias_kernel(rb_ref, bp_ref, bs_ref, tp_ref, ts_ref, tn_ref):
    bp = bp_ref[...]
    bs = bs_ref[...]
    rowid = lax.broadcasted_iota(jnp.int32, (N_HEADS, WINDOW), 0)
    ts = jnp.zeros((N_HEADS, WINDOW), F32)
    tn = jnp.zeros((N_HEADS, WINDOW), F32)
    for h in range(N_HEADS):
        tp = jnp.zeros((WINDOW, WINDOW), F32)
        for bk in range(N_BUCKETS):
            v = rb_ref[bk, h]
            tp = jnp.where(bp == bk, v, tp)
            ts = jnp.where((bs == bk) & (rowid == h), v, ts)
        tp_ref[h] = tp
        tn = jnp.where(rowid == h, rb_ref[0, h], tn)
    ts_ref[...] = jnp.where(bs < 0, -jnp.inf, ts)
    tn_ref[...] = tn


def _bias_tables(rel_bias):
    return pl.pallas_call(
        _bias_kernel,
        out_shape=(jax.ShapeDtypeStruct((N_HEADS, WINDOW, WINDOW), F32),
                   jax.ShapeDtypeStruct((N_HEADS, WINDOW), F32),
                   jax.ShapeDtypeStruct((N_HEADS, WINDOW), F32)),
        in_specs=[pl.BlockSpec(memory_space=pltpu.SMEM),
                  pl.BlockSpec(memory_space=pltpu.VMEM),
                  pl.BlockSpec(memory_space=pltpu.VMEM)],
        name="bias_tables",
    )(rel_bias, jnp.asarray(_prompt_buckets()), jnp.asarray(_sample_buckets()))


def _inproj_kernel(x_ref, nw_ref, w_ref, o_ref):
    h = _rms(x_ref[...], nw_ref[...]).astype(BF16)
    o_ref[...] = jnp.dot(h, w_ref[...], preferred_element_type=F32)


def _inproj(x2d, norm_w, w_perm, tm):
    n = x2d.shape[0]
    return pl.pallas_call(
        _inproj_kernel,
        out_shape=jax.ShapeDtypeStruct((n, PROJ_COLS), F32),
        grid=(n // tm,),
        in_specs=[pl.BlockSpec((tm, D_MODEL), lambda i: (i, 0)),
                  _const_spec((1, D_MODEL)),
                  _const_spec((D_MODEL, PROJ_COLS))],
        out_specs=pl.BlockSpec((tm, PROJ_COLS), lambda i: (i, 0)),
        compiler_params=pltpu.CompilerParams(dimension_semantics=("arbitrary",),
                                             vmem_limit_bytes=VMEM_LIMIT),
        name="inproj",
    )(x2d, norm_w, w_perm)


def _head_variants(x, low):
    xr = pltpu.roll(x, HEAD_DIM, axis=1)
    zero = jnp.zeros_like(x)
    return ((jnp.where(low, x, zero).astype(BF16), jnp.where(low, zero, xr).astype(BF16)),
            (jnp.where(low, xr, zero).astype(BF16), jnp.where(low, zero, x).astype(BF16)))


def _gated_group_norm(y, z, nw):
    y = y * _silu(z)
    half = SSM_WIDTH // SSM_GROUPS
    parts = []
    for g in range(SSM_GROUPS):
        yg = y[:, g * half:(g + 1) * half]
        parts.append(yg * lax.rsqrt(jnp.mean(yg * yg, axis=-1, keepdims=True) + EPS))
    return jnp.concatenate(parts, axis=-1) * nw


def _attn_scores(pr, r0, kst_ref, vst_ref):
    w = WINDOW
    rows = slice(r0, r0 + w)
    low = lax.broadcasted_iota(jnp.int32, (w, w), 1) < HEAD_DIM

    for st, c0 in ((kst_ref, COL_K), (vst_ref, COL_V)):
        for g in range(N_KV_HEADS):
            st[g, 0:w, :] = st[g, w:2 * w, :]
            st[g, 2 * w:3 * w, :] = st[g, 3 * w:4 * w, :]
        var = _head_variants(pr[rows, c0:c0 + w], low)
        for g in range(N_KV_HEADS):
            st[g, w:2 * w, :] = var[g][0]
            st[g, 3 * w:4 * w, :] = var[g][1]

    scores = []
    for pp in range(N_HEADS // 2):
        q2 = pr[rows, COL_Q + pp * w:COL_Q + (pp + 1) * w].astype(BF16)
        scores.append(lax.dot_general(q2, kst_ref[pp // (KV_REP // 2)], _NT, preferred_element_type=F32))
    return scores


def _attn_softmax(scores, seq_start, sink_ref, tab_ref):
    w = WINDOW
    row = lax.broadcasted_iota(jnp.int32, (w, w), 0)
    lane = lax.broadcasted_iota(jnp.int32, (w, w), 1)
    upper = lane > row
    if seq_start is not None:
        no_prev = (lane - row) > jnp.where(seq_start, 0, w)
    p4s = []
    for pp in range(N_HEADS // 2):
        s4 = scores[pp]
        probs = []
        for half in range(2):
            hh = 2 * pp + half
            sp = s4[:, 2 * w * half:2 * w * half + w]
            sc = s4[:, 2 * w * half + w:2 * w * (half + 1)]
            s = jnp.where(upper, sp, sc) * ATTN_SCALE + tab_ref[hh]
            if seq_start is not None:
                s = jnp.where(no_prev, -jnp.inf, s)
            sk = sink_ref[hh]
            m = jnp.maximum(jnp.max(s, axis=-1, keepdims=True), sk)
            e = jnp.exp(s - m)
            den = jnp.sum(e, axis=-1, keepdims=True) + jnp.exp(sk - m)
            p = e / den
            zero = jnp.zeros_like(p)
            probs.append(jnp.where(upper, p, zero).astype(BF16))
            probs.append(jnp.where(upper, zero, p).astype(BF16))
        p4s.append(jnp.concatenate(probs, axis=1))
    return p4s


def _attn_values(p4s, r0, vst_ref, out_ref):
    w = WINDOW
    for pp in range(N_HEADS // 2):
        out_ref[r0:r0 + w, pp * w:(pp + 1) * w] = jnp.dot(p4s[pp], vst_ref[pp // (KV_REP // 2)],
                                                          preferred_element_type=F32).astype(out_ref.dtype)


def _ssd_decay(pr, r0, dtb_ref, alog_ref):
    rows = slice(r0, r0 + CHUNK)
    row = lax.broadcasted_iota(jnp.int32, (CHUNK, CHUNK), 0)
    lane = lax.broadcasted_iota(jnp.int32, (CHUNK, CHUNK), 1)
    dt_t = _softplus(pr[rows, COL_DT:COL_DT + DT_PAD].T[0:SSM_HEADS, :] + dtb_ref[...])
    a_t = dt_t * (-jnp.exp(alog_ref[...]))
    cs_t = jnp.dot(a_t, (row <= lane).astype(F32), precision=lax.Precision.HIGHEST,
                   preferred_element_type=F32)
    cs_end = cs_t[:, CHUNK - 1:CHUNK]
    w_t = jnp.exp(cs_end - cs_t) * dt_t
    dec_end = jnp.exp(cs_end)
    col = jnp.concatenate([cs_t, jnp.exp(cs_t), jnp.zeros((CHUNK - 2 * SSM_HEADS, CHUNK), F32)], axis=0).T
    return dt_t, cs_t, w_t, dec_end, col


def _ssd_conv(pr, r0, seq_start, cw_ref, cb_ref, hist_ref):
    x = pr[r0:r0 + CHUNK, COL_XBC:COL_XBC + CONV_DIM]
    hist = hist_ref[...]
    if seq_start is not None:
        hist = jnp.where(seq_start, 0.0, hist)
    xx = jnp.concatenate([hist, x], axis=0)
    tot = pltpu.roll(xx, CONV_W - 1, axis=0)[8:, :] * cw_ref[0:1, :]
    for j in range(1, CONV_W - 1):
        tot = tot + pltpu.roll(xx, CONV_W - 1 - j, axis=0)[8:, :] * cw_ref[j:j + 1, :]
    tot = tot + x * cw_ref[CONV_W - 1:CONV_W, :]
    hist_ref[...] = x[CHUNK - 8:, :]
    return _silu(cb_ref[...] + tot)


def _ssd_matmuls(pr, r0, seq_start, decay_terms, xc, dsk_ref, nw_ref, st_ref, out_ref):
    hd = SSM_HEAD_DIM
    rows = slice(r0, r0 + CHUNK)
    dt_t, cs_t, w_t, dec_end, col = decay_terms
    row = lax.broadcasted_iota(jnp.int32, (CHUNK, CHUNK), 0)
    lane = lax.broadcasted_iota(jnp.int32, (CHUNK, CHUNK), 1)
    causal = row >= lane
    low = lane < hd
    xs = xc[:, :SSM_WIDTH]
    bm = xc[:, SSM_WIDTH:SSM_WIDTH + SSM_GROUPS * D_STATE]
    cm = xc[:, SSM_WIDTH + SSM_GROUPS * D_STATE:]

    groups = range(SSM_GROUPS)
    bgs = [bm[:, g * D_STATE:(g + 1) * D_STATE].astype(BF16) for g in groups]
    cgs = [cm[:, g * D_STATE:(g + 1) * D_STATE].astype(BF16) for g in groups]
    h_prevs = []
    for g in groups:
        hp = st_ref[g * SSM_REP:(g + 1) * SSM_REP].reshape(SSM_REP * hd, D_STATE)
        if seq_start is not None:
            hp = jnp.where(seq_start, 0.0, hp)
        h_prevs.append(hp)
    cbs = [lax.dot_general(cgs[g], bgs[g], _NT, preferred_element_type=F32) for g in groups]
    y_offs = [lax.dot_general(cgs[g], h_prevs[g].astype(BF16), _NT, preferred_element_type=F32) for g in groups]

    for g in groups:
        xt, wts, decs = [], [], []
        for pr2 in range(SSM_REP // 2):
            h0 = g * SSM_REP + 2 * pr2
            xt.append(xs[:, h0 * hd:(h0 + 2) * hd].T)
            for h in (h0, h0 + 1):
                wts.append(jnp.broadcast_to(w_t[h:h + 1, :], (hd, CHUNK)))
                decs.append(jnp.broadcast_to(dec_end[h:h + 1, :], (hd, D_STATE)))
        xw = (jnp.concatenate(xt, axis=0) * jnp.concatenate(wts, axis=0)).astype(BF16)
        st = jnp.dot(xw, bgs[g], preferred_element_type=F32)
        h_new = h_prevs[g] * jnp.concatenate(decs, axis=0) + st
        st_ref[g * SSM_REP:(g + 1) * SSM_REP] = h_new.reshape(SSM_REP, hd, D_STATE)

    ys = []
    for g in groups:
        for pr2 in range(SSM_REP // 2):
            h0 = g * SSM_REP + 2 * pr2
            xpair = xs[:, h0 * hd:(h0 + 2) * hd]
            zero = jnp.zeros_like(xpair)
            m = []
            for h in (h0, h0 + 1):
                decay = jnp.exp(jnp.where(causal, col[:, h:h + 1] - cs_t[h:h + 1, :], -jnp.inf))
                m.append((cbs[g] * decay * dt_t[h:h + 1, :]).astype(BF16))
            y_diag = (jnp.dot(m[0], jnp.where(low, xpair, zero).astype(BF16), preferred_element_type=F32)
                      + jnp.dot(m[1], jnp.where(low, zero, xpair).astype(BF16), preferred_element_type=F32))
            e0 = jnp.broadcast_to(col[:, SSM_HEADS + h0:SSM_HEADS + h0 + 1], (CHUNK, CHUNK))
            e1 = jnp.broadcast_to(col[:, SSM_HEADS + h0 + 1:SSM_HEADS + h0 + 2], (CHUNK, CHUNK))
            ys.append(y_diag + y_offs[g][:, 2 * pr2 * hd:(2 * pr2 + 2) * hd] * jnp.where(low, e0, e1))
    y = jnp.concatenate(ys, axis=-1) + xs * dsk_ref[...]
    out_ref[rows, ATTN_WIDTH:ATTN_WIDTH + SSM_WIDTH] = _gated_group_norm(
        y, pr[rows, COL_Z:COL_Z + SSM_WIDTH], nw_ref[...]).astype(out_ref.dtype)


def _prompt_layer_kernel(tiles_per_seq, n_tiles, sink_ref, x_ref, xres_ref, n1_ref, w_ref, tab_ref, cw_ref,
                         cb_ref, dtb_ref, alog_ref, dsk_ref, nw_ref, wo_ref, n2_ref, wg_ref, wu_ref, wd_ref,
                         fn_ref, y_ref, kv_ref, conv_ref, ssm_ref,
                         proj_ref, mix_ref, kst_ref, vst_ref, hist_ref, st_ref):
    t = pl.program_id(0)

    @pl.when(t == 0)
    def _():
        kst_ref[...] = jnp.zeros_like(kst_ref)
        vst_ref[...] = jnp.zeros_like(vst_ref)
        hist_ref[...] = jnp.zeros_like(hist_ref)
        st_ref[...] = jnp.zeros_like(st_ref)

    slot = lax.rem(t, 2)
    pw = proj_ref.at[slot]
    pr = proj_ref.at[1 - slot]
    mw = mix_ref.at[1 - slot]
    mr = mix_ref.at[slot]
    seq_start = lax.rem(t + tiles_per_seq - 1, tiles_per_seq) == 0

    def normed_input():
        return _rms(x_ref[...], n1_ref[...]).astype(BF16)

    def project(h, piece):
        lo, hi = PROJ_PIECES[piece]
        pw[:, lo:hi] = jnp.dot(h, w_ref[:, lo:hi], preferred_element_type=F32)

    def ffn_input():
        x1 = xres_ref[...] + jnp.dot(mr[...], wo_ref[...], preferred_element_type=F32)
        return x1, _rms(x1, n2_ref[...]).astype(BF16)

    def ffn(h2, acc, chunk):
        cols = slice(chunk * FF_CHUNK, (chunk + 1) * FF_CHUNK)
        gate = jnp.dot(h2, wg_ref[:, cols], preferred_element_type=F32)
        up = jnp.dot(h2, wu_ref[:, cols], preferred_element_type=F32)
        act = (_silu(gate) * up).astype(BF16)
        part = jnp.dot(act, wd_ref[cols, :], preferred_element_type=F32)
        acc[0] = part if acc[0] is None else acc[0] + part

    def mix_block(j, between):
        r0 = j * CHUNK
        start = seq_start if j == 0 else None
        decay_terms = _ssd_decay(pr, r0, dtb_ref, alog_ref)
        scores = _attn_scores(pr, r0, kst_ref, vst_ref)
        between[0]()
        xc = _ssd_conv(pr, r0, start, cw_ref, cb_ref, hist_ref)
        between[1]()
        p4s = _attn_softmax(scores, start, sink_ref, tab_ref)
        _attn_values(p4s, r0, vst_ref, mw)
        between[2]()
        _ssd_matmuls(pr, r0, start, decay_terms, xc, dsk_ref, nw_ref, st_ref, mw)
        between[3]()

    def sequence_outputs():
        kv_ref[0] = pr[MIX_TILE - WINDOW:, COL_K:COL_K + KV_COLS]
        conv_ref[0] = pr[MIX_TILE - (CONV_W - 1):, COL_XBC:COL_XBC + CONV_DIM]
        ssm_ref[0] = st_ref[...]

    steady = t >= 2

    @pl.when(steady)
    def _():
        h = normed_input()
        x1, h2 = ffn_input()
        acc = [None]
        pieces = ([functools.partial(project, h, p) for p in range(len(PROJ_PIECES))]
                  + [functools.partial(ffn, h2, acc, c) for c in range(N_FF_CHUNKS)])
        order = [0, 5, 6, 7, 1, 8, 9, 10, 2, 11, 12, 13, 3, 14, 15, 4]
        assert len(order) == 4 * MIX_BLOCKS
        for j in range(MIX_BLOCKS):
            mix_block(j, [pieces[i] for i in order[4 * j:4 * j + 4]])
        y_ref[...] = _rms(x1 + acc[0], fn_ref[...])

    @pl.when(steady & (t <= n_tiles))
    def _():
        sequence_outputs()

    @pl.when(jnp.logical_not(steady))
    def _():
        h = normed_input()
        for p in range(len(PROJ_PIECES)):
            project(h, p)

    @pl.when(t == 1)
    def _():
        for j in range(MIX_BLOCKS):
            mix_block(j, [lambda: None] * 4)
        sequence_outputs()


def _prompt_layer(x2d, n1, w_perm, sinks, tab, conv_w, conv_b, dtb_t, alog_t, dsk, nw, wo, n2, wg, wu, wd, fn,
                  batch, seq):
    assert seq % MIX_TILE == 0
    tiles_per_seq = seq // MIX_TILE
    nt = batch * tiles_per_seq

    def mixed_tile(t):
        return jnp.clip(t - 1, 0, nt - 1)

    def seq_of(t):
        return mixed_tile(t) // tiles_per_seq

    def ffn_tile(t):
        return jnp.maximum(t - 2, 0)

    return pl.pallas_call(
        functools.partial(_prompt_layer_kernel, tiles_per_seq, nt),
        out_shape=(jax.ShapeDtypeStruct((batch * seq, D_MODEL), F32),
                   jax.ShapeDtypeStruct((batch, WINDOW, KV_COLS), F32),
                   jax.ShapeDtypeStruct((batch, CONV_W - 1, CONV_DIM), F32),
                   jax.ShapeDtypeStruct((batch, SSM_HEADS, SSM_HEAD_DIM, D_STATE), F32)),
        grid=(nt + 2,),
        in_specs=[pl.BlockSpec(memory_space=pltpu.SMEM),
                  pl.BlockSpec((MIX_TILE, D_MODEL), lambda t: (jnp.minimum(t, nt - 1), 0)),
                  pl.BlockSpec((MIX_TILE, D_MODEL), lambda t: (ffn_tile(t), 0)),
                  _const_spec((1, D_MODEL)),
                  _const_spec((D_MODEL, PROJ_COLS)),
                  _const_spec((N_HEADS, WINDOW, WINDOW)),
                  _const_spec((CONV_W, CONV_DIM)),
                  _const_spec((1, CONV_DIM)),
                  _const_spec((SSM_HEADS, CHUNK)),
                  _const_spec((SSM_HEADS, CHUNK)),
                  _const_spec((1, SSM_WIDTH)),
                  _const_spec((1, SSM_WIDTH)),
                  _const_spec((D_MODEL, D_MODEL)),
                  _const_spec((1, D_MODEL)),
                  _const_spec((D_MODEL, D_FF)),
                  _const_spec((D_MODEL, D_FF)),
                  _const_spec((D_FF, D_MODEL)),
                  _const_spec((1, D_MODEL))],
        out_specs=(pl.BlockSpec((MIX_TILE, D_MODEL), lambda t: (ffn_tile(t), 0)),
                   pl.BlockSpec((1, WINDOW, KV_COLS), lambda t: (seq_of(t), 0, 0)),
                   pl.BlockSpec((1, CONV_W - 1, CONV_DIM), lambda t: (seq_of(t), 0, 0)),
                   pl.BlockSpec((1, SSM_HEADS, SSM_HEAD_DIM, D_STATE), lambda t: (seq_of(t), 0, 0, 0))),
        scratch_shapes=[pltpu.VMEM((2, MIX_TILE, PROJ_COLS), F32),
                        pltpu.VMEM((2, MIX_TILE, ATTN_WIDTH + SSM_WIDTH), BF16),
                        pltpu.VMEM((N_KV_HEADS, 4 * WINDOW, WINDOW), BF16),
                        pltpu.VMEM((N_KV_HEADS, 4 * WINDOW, WINDOW), BF16),
                        pltpu.VMEM((8, CONV_DIM), F32),
                        pltpu.VMEM((SSM_HEADS, SSM_HEAD_DIM, D_STATE), F32)],
        compiler_params=pltpu.CompilerParams(dimension_semantics=("arbitrary",),
                                             vmem_limit_bytes=LAYER_VMEM_LIMIT),
        name="prompt_layer",
    )(sinks, x2d, x2d, n1, w_perm, tab, conv_w, conv_b, dtb_t, alog_t, dsk, nw, wo, n2, wg, wu, wd, fn)


def _tail_kernel(x_ref, mix_ref, wo_ref, n2_ref, wg_ref, wu_ref, wd_ref, fn_ref, o_ref):
    x1 = x_ref[...] + jnp.dot(mix_ref[...].astype(BF16), wo_ref[...], preferred_element_type=F32)
    h2 = _rms(x1, n2_ref[...]).astype(BF16)
    acc = None
    for j in range(N_FF_CHUNKS):
        cols = slice(j * FF_CHUNK, (j + 1) * FF_CHUNK)
        gate = jnp.dot(h2, wg_ref[:, cols], preferred_element_type=F32)
        up = jnp.dot(h2, wu_ref[:, cols], preferred_element_type=F32)
        act = (_silu(gate) * up).astype(BF16)
        part = jnp.dot(act, wd_ref[cols, :], preferred_element_type=F32)
        acc = part if acc is None else acc + part
    o_ref[...] = _rms(x1 + acc, fn_ref[...])


def _tail(x2d, mix, wo, n2, wg, wu, wd, fn, tm):
    n = x2d.shape[0]
    return pl.pallas_call(
        _tail_kernel,
        out_shape=jax.ShapeDtypeStruct((n, D_MODEL), F32),
        grid=(n // tm,),
        in_specs=[pl.BlockSpec((tm, D_MODEL), lambda i: (i, 0)),
                  pl.BlockSpec((tm, ATTN_WIDTH + SSM_WIDTH), lambda i: (i, 0)),
                  _const_spec((D_MODEL, D_MODEL)),
                  _const_spec((1, D_MODEL)),
                  _const_spec((D_MODEL, D_FF)),
                  _const_spec((D_MODEL, D_FF)),
                  _const_spec((D_FF, D_MODEL)),
                  _const_spec((1, D_MODEL))],
        out_specs=pl.BlockSpec((tm, D_MODEL), lambda i: (i, 0)),
        compiler_params=pltpu.CompilerParams(dimension_semantics=("arbitrary",),
                                             vmem_limit_bytes=VMEM_LIMIT),
        name="outproj_ffn",
    )(x2d, mix, wo, n2, wg, wu, wd, fn)


SAMPLE_BT = 8
ATTN_SAMPLE_BT = 16


def _attn_sample_kernel(q_ref, kn_ref, vn_ref, ck_ref, cv_ref, ts_ref, tn_ref, sink_ref,
                        a_ref, nk_ref, nv_ref):
    rowid = lax.broadcasted_iota(jnp.int32, (N_HEADS, HEAD_DIM), 0)
    ts = ts_ref[...]
    tn = tn_ref[:, 0:1]
    sink = sink_ref[:, 0:1]
    pad = jnp.zeros((WINDOW - ATTN_SAMPLE_BT, WINDOW), F32)
    kn_t = jnp.concatenate([kn_ref[...], pad], axis=0).T
    vn_t = jnp.concatenate([vn_ref[...], pad], axis=0).T
    newest = lax.broadcasted_iota(jnp.int32, (WINDOW, WINDOW), 1) == WINDOW - 1

    def shifted(cache_ref, new_t, bb):
        return jnp.where(newest, new_t[:, bb:bb + 1], pltpu.roll(cache_ref[bb], WINDOW - 1, axis=1))

    qbds, scores = [], []
    for bb in range(ATTN_SAMPLE_BT):
        qb = q_ref[bb]
        qbd = jnp.concatenate([jnp.where(rowid < KV_REP, qb, 0.0), jnp.where(rowid >= KV_REP, qb, 0.0)], axis=1)
        qbds.append(qbd)
        scores.append(jnp.dot(qbd.astype(BF16), ck_ref[bb].astype(BF16), preferred_element_type=F32))
    probs, new_terms = [], []
    for bb in range(ATTN_SAMPLE_BT):
        s_c = scores[bb] * ATTN_SCALE + ts
        s_n = jnp.sum(qbds[bb] * kn_ref[bb:bb + 1, :], axis=-1, keepdims=True) * ATTN_SCALE + tn
        m = jnp.maximum(jnp.maximum(jnp.max(s_c, axis=-1, keepdims=True), s_n), sink)
        e_c = jnp.exp(s_c - m)
        e_n = jnp.exp(s_n - m)
        den = jnp.sum(e_c, axis=-1, keepdims=True) + e_n + jnp.exp(sink - m)
        probs.append((e_c / den).astype(BF16))
        new_terms.append((e_n / den) * vn_ref[bb:bb + 1, :])
    for bb in range(ATTN_SAMPLE_BT):
        o = lax.dot_general(probs[bb], cv_ref[bb].astype(BF16), _NT, preferred_element_type=F32) + new_terms[bb]
        a_ref[bb] = jnp.where(rowid < KV_REP, o[:, :HEAD_DIM], o[:, HEAD_DIM:])
    for bb in range(ATTN_SAMPLE_BT):
        nk_ref[bb] = shifted(ck_ref, kn_t, bb)
        nv_ref[bb] = shifted(cv_ref, vn_t, bb)


def _attn_sample(q3, proj, ck, cv, ts, tn, sink_b):
    nb = q3.shape[0]
    bt = ATTN_SAMPLE_BT
    kv = N_KV_HEADS * HEAD_DIM
    return pl.pallas_call(
        _attn_sample_kernel,
        out_shape=(jax.ShapeDtypeStruct((nb, N_HEADS, HEAD_DIM), F32),
                   jax.ShapeDtypeStruct((nb, WINDOW, kv), F32),
                   jax.ShapeDtypeStruct((nb, WINDOW, kv), F32)),
        grid=(nb // bt,),
        in_specs=[pl.BlockSpec((bt, N_HEADS, HEAD_DIM), lambda i: (i, 0, 0)),
                  pl.BlockSpec((bt, kv), lambda i: (i, COL_K // kv)),
                  pl.BlockSpec((bt, kv), lambda i: (i, COL_V // kv)),
                  pl.BlockSpec((bt, WINDOW, kv), lambda i: (i, 0, 0)),
                  pl.BlockSpec((bt, WINDOW, kv), lambda i: (i, 0, 0)),
                  _const_spec((N_HEADS, WINDOW)),
                  _const_spec((N_HEADS, WINDOW)),
                  _const_spec((N_HEADS, WINDOW))],
        out_specs=(pl.BlockSpec((bt, N_HEADS, HEAD_DIM), lambda i: (i, 0, 0)),
                   pl.BlockSpec((bt, WINDOW, kv), lambda i: (i, 0, 0)),
                   pl.BlockSpec((bt, WINDOW, kv), lambda i: (i, 0, 0))),
        compiler_params=pltpu.CompilerParams(dimension_semantics=("arbitrary",),
                                             vmem_limit_bytes=VMEM_LIMIT),
        name="attn_sample",
    )(q3, proj, proj, ck, cv, ts, tn, sink_b)


def _ssm_sample_prep_kernel(xbc_ref, dt_ref, cst_ref, cw_ref, cb_ref, dtb_ref, alog_ref,
                            xs_ref, bm_ref, cm_ref, xdt_t_ref, dec_ref, nc_ref):
    xbc = xbc_ref[...]
    tot = cst_ref[0] * cw_ref[0:1, :]
    tot = tot + cst_ref[1] * cw_ref[1:2, :]
    tot = tot + cst_ref[2] * cw_ref[2:3, :]
    tot = tot + xbc * cw_ref[3:4, :]
    xc = _silu(cb_ref[...] + tot)
    nc_ref[0] = cst_ref[1]
    nc_ref[1] = cst_ref[2]
    nc_ref[2] = xbc
    xs = xc[:, :SSM_WIDTH]
    xs_ref[...] = xs
    bm_ref[...] = xc[:, SSM_WIDTH:SSM_WIDTH + SSM_GROUPS * D_STATE]
    cm_ref[...] = xc[:, SSM_WIDTH + SSM_GROUPS * D_STATE:]
    dt = _softplus(dt_ref[...] + dtb_ref[...])
    dec_ref[...] = jnp.exp(dt * (-jnp.exp(alog_ref[...])))
    dt_t = dt.T
    for k in range(SSM_WIDTH // 128):
        xt = xs[:, k * 128:(k + 1) * 128].T
        for half in range(2):
            h = 2 * k + half
            lo = h * SSM_HEAD_DIM
            xdt_t_ref[lo:lo + SSM_HEAD_DIM, :] = xt[half * SSM_HEAD_DIM:(half + 1) * SSM_HEAD_DIM, :] * dt_t[h:h + 1, :]


def _ssm_sample_prep(proj, conv_state_t, conv_w, conv_b, dtb, alog):
    nb = proj.shape[0]
    return pl.pallas_call(
        _ssm_sample_prep_kernel,
        out_shape=(jax.ShapeDtypeStruct((nb, SSM_WIDTH), F32),
                   jax.ShapeDtypeStruct((nb, SSM_GROUPS * D_STATE), F32),
                   jax.ShapeDtypeStruct((nb, SSM_GROUPS * D_STATE), F32),
                   jax.ShapeDtypeStruct((SSM_WIDTH, nb), F32),
                   jax.ShapeDtypeStruct((nb, DT_PAD), F32),
                   jax.ShapeDtypeStruct((CONV_W - 1, nb, CONV_DIM), F32)),
        grid=(1,),
        in_specs=[pl.BlockSpec((nb, CONV_DIM), lambda i: (0, COL_XBC // CONV_DIM)),
                  pl.BlockSpec((nb, DT_PAD), lambda i: (0, COL_DT // DT_PAD)),
                  _const_spec((CONV_W - 1, nb, CONV_DIM)),
                  _const_spec((CONV_W, CONV_DIM)),
                  _const_spec((1, CONV_DIM)),
                  _const_spec((1, DT_PAD)),
                  _const_spec((1, DT_PAD))],
        out_specs=(_const_spec((nb, SSM_WIDTH)),
                   _const_spec((nb, SSM_GROUPS * D_STATE)),
                   _const_spec((nb, SSM_GROUPS * D_STATE)),
                   _const_spec((SSM_WIDTH, nb)),
                   _const_spec((nb, DT_PAD)),
                   _const_spec((CONV_W - 1, nb, CONV_DIM))),
        compiler_params=pltpu.CompilerParams(dimension_semantics=("arbitrary",),
                                             vmem_limit_bytes=VMEM_LIMIT),
        name="ssm_sample_prep",
    )(proj, proj, conv_state_t, conv_w, conv_b, dtb, alog)


def _ssm_sample_state_kernel(dec_ref, xdt_t_ref, bm_ref, cm_ref, xs_ref, z_ref, dsk_ref, nw_ref, st_ref,
                             nst_ref, s_ref, yt_ref):
    i = pl.program_id(0)
    last = pl.num_programs(0) - 1
    rows = SSM_REP * SSM_HEAD_DIM

    @pl.when(i == 0)
    def _():
        yt_ref[...] = jnp.zeros_like(yt_ref)

    lane = lax.broadcasted_iota(jnp.int32, (rows, 128), 1)
    for bb in range(SAMPLE_BT):
        b = i * SAMPLE_BT + bb
        mine = lane == b
        for g in range(SSM_GROUPS):
            bg = bm_ref[:, g * D_STATE:(g + 1) * D_STATE].astype(BF16)
            cg = cm_ref[:, g * D_STATE:(g + 1) * D_STATE].astype(BF16)
            xsel = jnp.where(mine, xdt_t_ref[g * rows:(g + 1) * rows, :], 0.0).astype(BF16)
            outer = jnp.dot(xsel, bg, preferred_element_type=F32)
            new = []
            for r in range(SSM_REP):
                h = g * SSM_REP + r
                hn = st_ref[bb, h] * dec_ref[b * SSM_HEADS + h] + outer[r * SSM_HEAD_DIM:(r + 1) * SSM_HEAD_DIM, :]
                nst_ref[bb, h] = hn
                new.append(hn)
            hcat = jnp.concatenate(new, axis=0).astype(BF16)
            res = lax.dot_general(hcat, cg, _NT, preferred_element_type=F32)
            yt_ref[g * rows:(g + 1) * rows, :] += jnp.where(mine, res, 0.0)

    @pl.when(i == last)
    def _():
        y = jnp.concatenate([yt_ref[k * 128:(k + 1) * 128, :].T for k in range(SSM_WIDTH // 128)], axis=1)
        y = y + xs_ref[...] * dsk_ref[...]
        s_ref[...] = _gated_group_norm(y, z_ref[...], nw_ref[...])


def _ssm_sample_state(dec_flat, xdt_t, bm, cm, xs, proj, dsk, nw, state):
    nb = state.shape[0]
    bt = SAMPLE_BT
    return pl.pallas_call(
        _ssm_sample_state_kernel,
        out_shape=(jax.ShapeDtypeStruct(state.shape, F32),
                   jax.ShapeDtypeStruct((nb, SSM_WIDTH), F32)),
        grid=(nb // bt,),
        in_specs=[pl.BlockSpec(memory_space=pltpu.SMEM),
                  _const_spec((SSM_WIDTH, nb)),
                  _const_spec((nb, SSM_GROUPS * D_STATE)),
                  _const_spec((nb, SSM_GROUPS * D_STATE)),
                  _const_spec((nb, SSM_WIDTH)),
                  pl.BlockSpec((nb, SSM_WIDTH), lambda i: (0, COL_Z // SSM_WIDTH)),
                  _const_spec((1, SSM_WIDTH)),
                  _const_spec((1, SSM_WIDTH)),
                  pl.BlockSpec((bt, SSM_HEADS, SSM_HEAD_DIM, D_STATE), lambda i: (i, 0, 0, 0))],
        out_specs=(pl.BlockSpec((bt, SSM_HEADS, SSM_HEAD_DIM, D_STATE), lambda i: (i, 0, 0, 0)),
                   pl.BlockSpec((nb, SSM_WIDTH), lambda i: (0, 0))),
        scratch_shapes=[pltpu.VMEM((SSM_WIDTH, nb), F32)],
        compiler_params=pltpu.CompilerParams(dimension_semantics=("arbitrary",),
                                             vmem_limit_bytes=VMEM_LIMIT),
        name="ssm_sample_state",
    )(dec_flat, xdt_t, bm, cm, xs, proj, dsk, nw, state)


def _pad_lanes(v, width):
    return jnp.pad(v.reshape(1, -1), ((0, 0), (0, width - v.shape[-1])))


def kernel(x_prompt, x_sample, cache_k, cache_v, state_conv, state_ssm, rel_bias, norm1_w, w_in, attn_sinks,
           conv_w, conv_b, dt_bias, A_log, D_skip, ssm_norm_w, w_out, norm2_w, w_gate, w_up, w_down, final_norm_w):
    depth = norm1_w.shape[0]
    assert depth == 1, "single-layer trunk"
    batch, seq, _ = x_prompt.shape
    nb = x_sample.shape[0]
    assert x_sample.shape[1] == 1 and nb == 128

    wi = w_in[0]
    q_c, k_c, v_c, z_c, xbc_c, dt_c = (wi[:, 0:512], wi[:, 512:640], wi[:, 640:768], wi[:, 768:1280],
                                       wi[:, 1280:2304], wi[:, 2304:2312])
    w_perm = jnp.concatenate([q_c, z_c, xbc_c, k_c, v_c, jnp.pad(dt_c, ((0, 0), (0, DT_PAD - SSM_HEADS)))],
                             axis=1).astype(BF16)
    wo = w_out[0].astype(BF16)
    wg = w_gate[0].astype(BF16)
    wu = w_up[0].astype(BF16)
    wd = w_down[0].astype(BF16)
    n1 = norm1_w[0].reshape(1, D_MODEL)
    n2 = norm2_w[0].reshape(1, D_MODEL)
    fn = final_norm_w.reshape(1, D_MODEL)
    cw = conv_w[0]
    cb = conv_b[0].reshape(1, CONV_DIM)
    dtb = _pad_lanes(dt_bias[0], DT_PAD)
    alog = _pad_lanes(A_log[0], DT_PAD)
    dtb_t = jnp.broadcast_to(dt_bias[0][:, None], (SSM_HEADS, CHUNK))
    alog_t = jnp.broadcast_to(A_log[0][:, None], (SSM_HEADS, CHUNK))
    dsk = jnp.repeat(D_skip[0], SSM_HEAD_DIM).reshape(1, SSM_WIDTH)
    nw = ssm_norm_w[0].reshape(1, SSM_WIDTH)
    sinks = attn_sinks[0]
    sink_b = jnp.broadcast_to(sinks[:, None], (N_HEADS, WINDOW))

    tab_p, tab_s, tab_n = _bias_tables(rel_bias)

    xp2 = x_prompt.reshape(batch * seq, D_MODEL)
    y_p, kv_p, conv_p, ssm_p = _prompt_layer(xp2, n1, w_perm, sinks, tab_p, cw, cb, dtb_t, alog_t, dsk, nw,
                                              wo, n2, wg, wu, wd, fn, batch, seq)
    kv_p = kv_p.reshape(batch, WINDOW, 2, N_KV_HEADS, HEAD_DIM)

    xs2 = x_sample.reshape(nb, D_MODEL)
    proj_s = _inproj(xs2, n1, w_perm, nb)
    q3 = proj_s[:, COL_Q:COL_Q + ATTN_WIDTH].reshape(nb, N_HEADS, HEAD_DIM)
    ck = jnp.transpose(cache_k[0], (0, 2, 3, 1)).reshape(nb, N_KV_HEADS * HEAD_DIM, WINDOW)
    cv = jnp.transpose(cache_v[0], (0, 2, 3, 1)).reshape(nb, N_KV_HEADS * HEAD_DIM, WINDOW)
    a_s3, nk, nv = _attn_sample(q3, proj_s, ck, cv, tab_s, tab_n, sink_b)
    conv_t = jnp.transpose(state_conv[0], (1, 0, 2))
    xs_s, bm_s, cm_s, xdt_t, dec, nconv_t = _ssm_sample_prep(proj_s, conv_t, cw, cb, dtb, alog)
    dec_flat = dec[:, :SSM_HEADS].reshape(nb * SSM_HEADS)
    nssm, s_s = _ssm_sample_state(dec_flat, xdt_t, bm_s, cm_s, xs_s, proj_s, dsk, nw, state_ssm[0])
    mix_s = jnp.concatenate([a_s3.reshape(nb, ATTN_WIDTH), s_s], axis=1)
    y_s = _tail(xs2, mix_s, wo, n2, wg, wu, wd, fn, nb)

    return (y_p.reshape(batch, seq, D_MODEL),
            y_s.reshape(nb, 1, D_MODEL),
            kv_p[:, :, 0][None],
            kv_p[:, :, 1][None],
            conv_p[None],
            ssm_p[None],
            jnp.transpose(nk.reshape(nb, N_KV_HEADS, HEAD_DIM, WINDOW), (0, 3, 1, 2))[None],
            jnp.transpose(nv.reshape(nb, N_KV_HEADS, HEAD_DIM, WINDOW), (0, 3, 1, 2))[None],
            jnp.transpose(nconv_t, (1, 0, 2))[None],
            nssm[None])
```

```python
import functools
import math

import numpy as np
import jax
import jax.numpy as jnp
from jax import lax
from jax.experimental import pallas as pl
from jax.experimental.pallas import tpu as pltpu

F32 = jnp.float32
BF16 = jnp.bfloat16

D_MODEL = 1024
HEAD_DIM = 64
N_HEADS = 8
N_KV_HEADS = 2
KV_REP = 4
WINDOW = 128
ATTN_WIDTH = 512
ATTN_SCALE = HEAD_DIM ** -0.5
N_BUCKETS = 32
MAX_DISTANCE = 128
SSM_WIDTH = 512
SSM_HEADS = 8
SSM_GROUPS = 2
SSM_REP = 4
SSM_HEAD_DIM = 64
D_STATE = 128
CONV_W = 4
CONV_DIM = 1024
CHUNK = 128
D_FF = 2816
EPS = 1e-6

COL_Q, COL_Z, COL_XBC, COL_K, COL_V, COL_DT = 0, 512, 1024, 2048, 2176, 2304
PROJ_COLS = 2432
DT_PAD = 128
KV_COLS = 2 * N_KV_HEADS * HEAD_DIM

FF_CHUNK = 256
N_FF_CHUNKS = D_FF // FF_CHUNK

MIX_TILE = 512
MIX_BLOCKS = MIX_TILE // CHUNK
PROJ_PIECES = ((0, 512), (512, 1024), (1024, 1536), (1536, 2048), (2048, PROJ_COLS))

TAIL_TILE = 512

VMEM_LIMIT = 56 * 1024 * 1024
LAYER_VMEM_LIMIT = 60 * 1024 * 1024

_NT = (((1,), (1,)), ((), ()))


def _bucket_table(dist):
    n = np.maximum(dist, 0)
    exact = N_BUCKETS // 2
    nf = np.maximum(n, 1).astype(np.float32)
    large = exact + (np.log(nf / exact) / math.log(MAX_DISTANCE / exact) * (N_BUCKETS - exact)).astype(np.int32)
    return np.where(n < exact, n, np.minimum(large, N_BUCKETS - 1)).astype(np.int32)


def _prompt_buckets():
    i = np.arange(WINDOW)[:, None]
    j = np.arange(WINDOW)[None, :]
    dist = np.where(j > i, i + WINDOW - j, i - j)
    return _bucket_table(dist)


def _sample_buckets():
    dist = WINDOW - np.arange(WINDOW)[None, :]
    band = (dist >= 0) & (dist < WINDOW)
    row = np.where(band, _bucket_table(dist), -1).astype(np.int32)
    return np.tile(row, (N_HEADS, 1))


def _rms(x, w):
    return x * lax.rsqrt(jnp.mean(x * x, axis=-1, keepdims=True) + EPS) * w


def _silu(x):
    return x * (0.5 + 0.5 * jnp.tanh(0.5 * x))


def _softplus(x):
    return jnp.maximum(x, 0.0) + jnp.log1p(jnp.exp(-jnp.abs(x)))


def _const_spec(shape):
    nd = len(shape)
    return pl.BlockSpec(shape, lambda *_: (0,) * nd, pipeline_mode=pl.Buffered(1))


def _bias_kernel(rb_ref, bp_ref, bs_ref, tp_ref, ts_ref, tn_ref):
    bp = bp_ref[...]
    bs = bs_ref[...]
    rowid = lax.broadcasted_iota(jnp.int32, (N_HEADS, WINDOW), 0)
    ts = jnp.zeros((N_HEADS, WINDOW), F32)
    tn = jnp.zeros((N_HEADS, WINDOW), F32)
    for h in range(N_HEADS):
        tp = jnp.zeros((WINDOW, WINDOW), F32)
        for bk in range(N_BUCKETS):
            v = rb_ref[bk, h]
            tp = jnp.where(bp == bk, v, tp)
            ts = jnp.where((bs == bk) & (rowid == h), v, ts)
        tp_ref[h] = tp
        tn = jnp.where(rowid == h, rb_ref[0, h], tn)
    ts_ref[...] = jnp.where(bs < 0, -jnp.inf, ts)
    tn_ref[...] = tn


def _bias_tables(rel_bias):
    return pl.pallas_call(
        _bias_kernel,
        out_shape=(jax.ShapeDtypeStruct((N_HEADS, WINDOW, WINDOW), F32),
                   jax.ShapeDtypeStruct((N_HEADS, WINDOW), F32),
                   jax.ShapeDtypeStruct((N_HEADS, WINDOW), F32)),
        in_specs=[pl.BlockSpec(memory_space=pltpu.SMEM),
                  pl.BlockSpec(memory_space=pltpu.VMEM),
                  pl.BlockSpec(memory_space=pltpu.VMEM)],
        name="bias_tables",
    )(rel_bias, jnp.asarray(_prompt_buckets()), jnp.asarray(_sample_buckets()))


def _inproj_kernel(x_ref, nw_ref, w_ref, o_ref):
    h = _rms(x_ref[...], nw_ref[...]).astype(BF16)
    o_ref[...] = jnp.dot(h, w_ref[...], preferred_element_type=F32)


def _inproj(x2d, norm_w, w_perm, tm):
    n = x2d.shape[0]
    return pl.pallas_call(
        _inproj_kernel,
        out_shape=jax.ShapeDtypeStruct((n, PROJ_COLS), F32),
        grid=(n // tm,),
        in_specs=[pl.BlockSpec((tm, D_MODEL), lambda i: (i, 0)),
                  _const_spec((1, D_MODEL)),
                  _const_spec((D_MODEL, PROJ_COLS))],
        out_specs=pl.BlockSpec((tm, PROJ_COLS), lambda i: (i, 0)),
        compiler_params=pltpu.CompilerParams(dimension_semantics=("arbitrary",),
                                             vmem_limit_bytes=VMEM_LIMIT),
        name="inproj",
    )(x2d, norm_w, w_perm)


def _head_variants(x, low):
    xr = pltpu.roll(x, HEAD_DIM, axis=1)
    zero = jnp.zeros_like(x)
    return ((jnp.where(low, x, zero).astype(BF16), jnp.where(low, zero, xr).astype(BF16)),
            (jnp.where(low, xr, zero).astype(BF16), jnp.where(low, zero, x).astype(BF16)))


def _gated_group_norm(y, z, nw):
    y = y * _silu(z)
    half = SSM_WIDTH // SSM_GROUPS
    parts = []
    for g in range(SSM_GROUPS):
        yg = y[:, g * half:(g + 1) * half]
        parts.append(yg * lax.rsqrt(jnp.mean(yg * yg, axis=-1, keepdims=True) + EPS))
    return jnp.concatenate(parts, axis=-1) * nw


def _attn_scores(pr, r0, kst_ref, vst_ref):
    w = WINDOW
    rows = slice(r0, r0 + w)
    low = lax.broadcasted_iota(jnp.int32, (w, w), 1) < HEAD_DIM

    for st, c0 in ((kst_ref, COL_K), (vst_ref, COL_V)):
        for g in range(N_KV_HEADS):
            st[g, 0:w, :] = st[g, w:2 * w, :]
            st[g, 2 * w:3 * w, :] = st[g, 3 * w:4 * w, :]
        var = _head_variants(pr[rows, c0:c0 + w], low)
        for g in range(N_KV_HEADS):
            st[g, w:2 * w, :] = var[g][0]
            st[g, 3 * w:4 * w, :] = var[g][1]

    scores = []
    for pp in range(N_HEADS // 2):
        q2 = pr[rows, COL_Q + pp * w:COL_Q + (pp + 1) * w].astype(BF16)
        scores.append(lax.dot_general(q2, kst_ref[pp // (KV_REP // 2)], _NT, preferred_element_type=F32))
    return scores


def _attn_softmax(scores, seq_start, sink_ref, tab_ref):
    w = WINDOW
    row = lax.broadcasted_iota(jnp.int32, (w, w), 0)
    lane = lax.broadcasted_iota(jnp.int32, (w, w), 1)
    upper = lane > row
    if seq_start is not None:
        no_prev = (lane - row) > jnp.where(seq_start, 0, w)
    p4s = []
    for pp in range(N_HEADS // 2):
        s4 = scores[pp]
        probs = []
        for half in range(2):
            hh = 2 * pp + half
            sp = s4[:, 2 * w * half:2 * w * half + w]
            sc = s4[:, 2 * w * half + w:2 * w * (half + 1)]
            s = jnp.where(upper, sp, sc) * ATTN_SCALE + tab_ref[hh]
            if seq_start is not None:
                s = jnp.where(no_prev, -jnp.inf, s)
            sk = sink_ref[hh]
            m = jnp.maximum(jnp.max(s, axis=-1, keepdims=True), sk)
            e = jnp.exp(s - m)
            den = jnp.sum(e, axis=-1, keepdims=True) + jnp.exp(sk - m)
            p = e / den
            zero = jnp.zeros_like(p)
            probs.append(jnp.where(upper, p, zero).astype(BF16))
            probs.append(jnp.where(upper, zero, p).astype(BF16))
        p4s.append(jnp.concatenate(probs, axis=1))
    return p4s


def _attn_values(p4s, r0, vst_ref, out_ref):
    w = WINDOW
    for pp in range(N_HEADS // 2):
        out_ref[r0:r0 + w, pp * w:(pp + 1) * w] = jnp.dot(p4s[pp], vst_ref[pp // (KV_REP // 2)],
                                                          preferred_element_type=F32).astype(out_ref.dtype)


def _ssd_decay(pr, r0, dtb_ref, alog_ref):
    rows = slice(r0, r0 + CHUNK)
    row = lax.broadcasted_iota(jnp.int32, (CHUNK, CHUNK), 0)
    lane = lax.broadcasted_iota(jnp.int32, (CHUNK, CHUNK), 1)
    dt_t = _softplus(pr[rows, COL_DT:COL_DT + DT_PAD].T[0:SSM_HEADS, :] + dtb_ref[...])
    a_t = dt_t * (-jnp.exp(alog_ref[...]))
    cs_t = jnp.dot(a_t, (row <= lane).astype(F32), precision=lax.Precision.HIGHEST,
                   preferred_element_type=F32)
    cs_end = cs_t[:, CHUNK - 1:CHUNK]
    w_t = jnp.exp(cs_end - cs_t) * dt_t
    dec_end = jnp.exp(cs_end)
    col = jnp.concatenate([cs_t, jnp.exp(cs_t), jnp.zeros((CHUNK - 2 * SSM_HEADS, CHUNK), F32)], axis=0).T
    return dt_t, cs_t, w_t, dec_end, col


def _ssd_conv(pr, r0, seq_start, cw_ref, cb_ref, hist_ref):
    x = pr[r0:r0 + CHUNK, COL_XBC:COL_XBC + CONV_DIM]
    hist = hist_ref[...]
    if seq_start is not None:
        hist = jnp.where(seq_start, 0.0, hist)
    xx = jnp.concatenate([hist, x], axis=0)
    tot = pltpu.roll(xx, CONV_W - 1, axis=0)[8:, :] * cw_ref[0:1, :]
    for j in range(1, CONV_W - 1):
        tot = tot + pltpu.roll(xx, CONV_W - 1 - j, axis=0)[8:, :] * cw_ref[j:j + 1, :]
    tot = tot + x * cw_ref[CONV_W - 1:CONV_W, :]
    hist_ref[...] = x[CHUNK - 8:, :]
    return _silu(cb_ref[...] + tot)


def _ssd_matmuls(pr, r0, seq_start, decay_terms, xc, dsk_ref, nw_ref, st_ref, out_ref):
    hd = SSM_HEAD_DIM
    rows = slice(r0, r0 + CHUNK)
    dt_t, cs_t, w_t, dec_end, col = decay_terms
    row = lax.broadcasted_iota(jnp.int32, (CHUNK, CHUNK), 0)
    lane = lax.broadcasted_iota(jnp.int32, (CHUNK, CHUNK), 1)
    causal = row >= lane
    low = lane < hd
    xs = xc[:, :SSM_WIDTH]
    bm = xc[:, SSM_WIDTH:SSM_WIDTH + SSM_GROUPS * D_STATE]
    cm = xc[:, SSM_WIDTH + SSM_GROUPS * D_STATE:]

    groups = range(SSM_GROUPS)
    bgs = [bm[:, g * D_STATE:(g + 1) * D_STATE].astype(BF16) for g in groups]
    cgs = [cm[:, g * D_STATE:(g + 1) * D_STATE].astype(BF16) for g in groups]
    h_prevs = []
    for g in groups:
        hp = st_ref[g * SSM_REP:(g + 1) * SSM_REP].reshape(SSM_REP * hd, D_STATE)
        if seq_start is not None:
            hp = jnp.where(seq_start, 0.0, hp)
        h_prevs.append(hp)
    cbs = [lax.dot_general(cgs[g], bgs[g], _NT, preferred_element_type=F32) for g in groups]
    y_offs = [lax.dot_general(cgs[g], h_prevs[g].astype(BF16), _NT, preferred_element_type=F32) for g in groups]

    for g in groups:
        xt, wts, decs = [], [], []
        for pr2 in range(SSM_REP // 2):
            h0 = g * SSM_REP + 2 * pr2
            xt.append(xs[:, h0 * hd:(h0 + 2) * hd].T)
            for h in (h0, h0 + 1):
                wts.append(jnp.broadcast_to(w_t[h:h + 1, :], (hd, CHUNK)))
                decs.append(jnp.broadcast_to(dec_end[h:h + 1, :], (hd, D_STATE)))
        xw = (jnp.concatenate(xt, axis=0) * jnp.concatenate(wts, axis=0)).astype(BF16)
        st = jnp.dot(xw, bgs[g], preferred_element_type=F32)
        h_new = h_prevs[g] * jnp.concatenate(decs, axis=0) + st
        st_ref[g * SSM_REP:(g + 1) * SSM_REP] = h_new.reshape(SSM_REP, hd, D_STATE)

    ys = []
    for g in groups:
        for pr2 in range(SSM_REP // 2):
            h0 = g * SSM_REP + 2 * pr2
            xpair = xs[:, h0 * hd:(h0 + 2) * hd]
            zero = jnp.zeros_like(xpair)
            m = []
            for h in (h0, h0 + 1):
                decay = jnp.exp(jnp.where(causal, col[:, h:h + 1] - cs_t[h:h + 1, :], -jnp.inf))
                m.append((cbs[g] * decay * dt_t[h:h + 1, :]).astype(BF16))
            y_diag = (jnp.dot(m[0], jnp.where(low, xpair, zero).astype(BF16), preferred_element_type=F32)
                      + jnp.dot(m[1], jnp.where(low, zero, xpair).astype(BF16), preferred_element_type=F32))
            e0 = jnp.broadcast_to(col[:, SSM_HEADS + h0:SSM_HEADS + h0 + 1], (CHUNK, CHUNK))
            e1 = jnp.broadcast_to(col[:, SSM_HEADS + h0 + 1:SSM_HEADS + h0 + 2], (CHUNK, CHUNK))
            ys.append(y_diag + y_offs[g][:, 2 * pr2 * hd:(2 * pr2 + 2) * hd] * jnp.where(low, e0, e1))
    y = jnp.concatenate(ys, axis=-1) + xs * dsk_ref[...]
    out_ref[rows, ATTN_WIDTH:ATTN_WIDTH + SSM_WIDTH] = _gated_group_norm(
        y, pr[rows, COL_Z:COL_Z + SSM_WIDTH], nw_ref[...]).astype(out_ref.dtype)


def _prompt_layer_kernel(tiles_per_seq, n_tiles, sink_ref, x_ref, xres_ref, n1_ref, w_ref, tab_ref, cw_ref,
                         cb_ref, dtb_ref, alog_ref, dsk_ref, nw_ref, wo_ref, n2_ref, wg_ref, wu_ref, wd_ref,
                         fn_ref, y_ref, kv_ref, conv_ref, ssm_ref,
                         proj_ref, mix_ref, kst_ref, vst_ref, hist_ref, st_ref):
    t = pl.program_id(0)

    @pl.when(t == 0)
    def _():
        kst_ref[...] = jnp.zeros_like(kst_ref)
        vst_ref[...] = jnp.zeros_like(vst_ref)
        hist_ref[...] = jnp.zeros_like(hist_ref)
        st_ref[...] = jnp.zeros_like(st_ref)

    slot = lax.rem(t, 2)
    pw = proj_ref.at[slot]
    pr = proj_ref.at[1 - slot]
    mw = mix_ref.at[1 - slot]
    mr = mix_ref.at[slot]
    seq_start = lax.rem(t + tiles_per_seq - 1, tiles_per_seq) == 0

    def normed_input():
        return _rms(x_ref[...], n1_ref[...]).astype(BF16)

    def project(h, piece):
        lo, hi = PROJ_PIECES[piece]
        pw[:, lo:hi] = jnp.dot(h, w_ref[:, lo:hi], preferred_element_type=F32)

    def ffn_input():
        x1 = xres_ref[...] + jnp.dot(mr[...], wo_ref[...], preferred_element_type=F32)
        return x1, _rms(x1, n2_ref[...]).astype(BF16)

    def ffn(h2, acc, chunk):
        cols = slice(chunk * FF_CHUNK, (chunk + 1) * FF_CHUNK)
        gate = jnp.dot(h2, wg_ref[:, cols], preferred_element_type=F32)
        up = jnp.dot(h2, wu_ref[:, cols], preferred_element_type=F32)
        act = (_silu(gate) * up).astype(BF16)
        part = jnp.dot(act, wd_ref[cols, :], preferred_element_type=F32)
        acc[0] = part if acc[0] is None else acc[0] + part

    def mix_block(j, between):
        r0 = j * CHUNK
        start = seq_start if j == 0 else None
        decay_terms = _ssd_decay(pr, r0, dtb_ref, alog_ref)
        scores = _attn_scores(pr, r0, kst_ref, vst_ref)
        between[0]()
        xc = _ssd_conv(pr, r0, start, cw_ref, cb_ref, hist_ref)
        between[1]()
        p4s = _attn_softmax(scores, start, sink_ref, tab_ref)
        _attn_values(p4s, r0, vst_ref, mw)
        between[2]()
        _ssd_matmuls(pr, r0, start, decay_terms, xc, dsk_ref, nw_ref, st_ref, mw)
        between[3]()

    def sequence_outputs():
        kv_ref[0] = pr[MIX_TILE - WINDOW:, COL_K:COL_K + KV_COLS]
        conv_ref[0] = pr[MIX_TILE - (CONV_W - 1):, COL_XBC:COL_XBC + CONV_DIM]
        ssm_ref[0] = st_ref[...]

    steady = t >= 2

    @pl.when(steady)
    def _():
        h = normed_input()
        x1, h2 = ffn_input()
        acc = [None]
        pieces = ([functools.partial(project, h, p) for p in range(len(PROJ_PIECES))]
                  + [functools.partial(ffn, h2, acc, c) for c in range(N_FF_CHUNKS)])
        order = [0, 5, 6, 7, 1, 8, 9, 10, 2, 11, 12, 13, 3, 14, 15, 4]
        assert len(order) == 4 * MIX_BLOCKS
        for j in range(MIX_BLOCKS):
            mix_block(j, [pieces[i] for i in order[4 * j:4 * j + 4]])
        y_ref[...] = _rms(x1 + acc[0], fn_ref[...])

    @pl.when(steady & (t <= n_tiles))
    def _():
        sequence_outputs()

    @pl.when(jnp.logical_not(steady))
    def _():
        h = normed_input()
        for p in range(len(PROJ_PIECES)):
            project(h, p)

    @pl.when(t == 1)
    def _():
        for j in range(MIX_BLOCKS):
            mix_block(j, [lambda: None] * 4)
        sequence_outputs()


def _prompt_layer(x2d, n1, w_perm, sinks, tab, conv_w, conv_b, dtb_t, alog_t, dsk, nw, wo, n2, wg, wu, wd, fn,
                  batch, seq):
    assert seq % MIX_TILE == 0
    tiles_per_seq = seq // MIX_TILE
    nt = batch * tiles_per_seq

    def mixed_tile(t):
        return jnp.clip(t - 1, 0, nt - 1)

    def seq_of(t):
        return mixed_tile(t) // tiles_per_seq

    def ffn_tile(t):
        return jnp.maximum(t - 2, 0)

    return pl.pallas_call(
        functools.partial(_prompt_layer_kernel, tiles_per_seq, nt),
        out_shape=(jax.ShapeDtypeStruct((batch * seq, D_MODEL), F32),
                   jax.ShapeDtypeStruct((batch, WINDOW, KV_COLS), F32),
                   jax.ShapeDtypeStruct((batch, CONV_W - 1, CONV_DIM), F32),
                   jax.ShapeDtypeStruct((batch, SSM_HEADS, SSM_HEAD_DIM, D_STATE), F32)),
        grid=(nt + 2,),
        in_specs=[pl.BlockSpec(memory_space=pltpu.SMEM),
                  pl.BlockSpec((MIX_TILE, D_MODEL), lambda t: (jnp.minimum(t, nt - 1), 0)),
                  pl.BlockSpec((MIX_TILE, D_MODEL), lambda t: (ffn_tile(t), 0)),
                  _const_spec((1, D_MODEL)),
                  _const_spec((D_MODEL, PROJ_COLS)),
                  _const_spec((N_HEADS, WINDOW, WINDOW)),
                  _const_spec((CONV_W, CONV_DIM)),
                  _const_spec((1, CONV_DIM)),
                  _const_spec((SSM_HEADS, CHUNK)),
                  _const_spec((SSM_HEADS, CHUNK)),
                  _const_spec((1, SSM_WIDTH)),
                  _const_spec((1, SSM_WIDTH)),
                  _const_spec((D_MODEL, D_MODEL)),
                  _const_spec((1, D_MODEL)),
                  _const_spec((D_MODEL, D_FF)),
                  _const_spec((D_MODEL, D_FF)),
                  _const_spec((D_FF, D_MODEL)),
                  _const_spec((1, D_MODEL))],
        out_specs=(pl.BlockSpec((MIX_TILE, D_MODEL), lambda t: (ffn_tile(t), 0)),
                   pl.BlockSpec((1, WINDOW, KV_COLS), lambda t: (seq_of(t), 0, 0)),
                   pl.BlockSpec((1, CONV_W - 1, CONV_DIM), lambda t: (seq_of(t), 0, 0)),
                   pl.BlockSpec((1, SSM_HEADS, SSM_HEAD_DIM, D_STATE), lambda t: (seq_of(t), 0, 0, 0))),
        scratch_shapes=[pltpu.VMEM((2, MIX_TILE, PROJ_COLS), F32),
                        pltpu.VMEM((2, MIX_TILE, ATTN_WIDTH + SSM_WIDTH), BF16),
                        pltpu.VMEM((N_KV_HEADS, 4 * WINDOW, WINDOW), BF16),
                        pltpu.VMEM((N_KV_HEADS, 4 * WINDOW, WINDOW), BF16),
                        pltpu.VMEM((8, CONV_DIM), F32),
                        pltpu.VMEM((SSM_HEADS, SSM_HEAD_DIM, D_STATE), F32)],
        compiler_params=pltpu.CompilerParams(dimension_semantics=("arbitrary",),
                                             vmem_limit_bytes=LAYER_VMEM_LIMIT),
        name="prompt_layer",
    )(sinks, x2d, x2d, n1, w_perm, tab, conv_w, conv_b, dtb_t, alog_t, dsk, nw, wo, n2, wg, wu, wd, fn)


def _tail_kernel(x_ref, mix_ref, wo_ref, n2_ref, wg_ref, wu_ref, wd_ref, fn_ref, o_ref):
    x1 = x_ref[...] + jnp.dot(mix_ref[...].astype(BF16), wo_ref[...], preferred_element_type=F32)
    h2 = _rms(x1, n2_ref[...]).astype(BF16)
    acc = None
    for j in range(N_FF_CHUNKS):
        cols = slice(j * FF_CHUNK, (j + 1) * FF_CHUNK)
        gate = jnp.dot(h2, wg_ref[:, cols], preferred_element_type=F32)
        up = jnp.dot(h2, wu_ref[:, cols], preferred_element_type=F32)
        act = (_silu(gate) * up).astype(BF16)
        part = jnp.dot(act, wd_ref[cols, :], preferred_element_type=F32)
        acc = part if acc is None else acc + part
    o_ref[...] = _rms(x1 + acc, fn_ref[...])


def _tail(x2d, mix, wo, n2, wg, wu, wd, fn, tm):
    n = x2d.shape[0]
    return pl.pallas_call(
        _tail_kernel,
        out_shape=jax.ShapeDtypeStruct((n, D_MODEL), F32),
        grid=(n // tm,),
        in_specs=[pl.BlockSpec((tm, D_MODEL), lambda i: (i, 0)),
                  pl.BlockSpec((tm, ATTN_WIDTH + SSM_WIDTH), lambda i: (i, 0)),
                  _const_spec((D_MODEL, D_MODEL)),
                  _const_spec((1, D_MODEL)),
                  _const_spec((D_MODEL, D_FF)),
                  _const_spec((D_MODEL, D_FF)),
                  _const_spec((D_FF, D_MODEL)),
                  _const_spec((1, D_MODEL))],
        out_specs=pl.BlockSpec((tm, D_MODEL), lambda i: (i, 0)),
        compiler_params=pltpu.CompilerParams(dimension_semantics=("arbitrary",),
                                             vmem_limit_bytes=VMEM_LIMIT),
        name="outproj_ffn",
    )(x2d, mix, wo, n2, wg, wu, wd, fn)


SAMPLE_BT = 8
ATTN_SAMPLE_BT = 16


def _attn_sample_kernel(q_ref, kn_ref, vn_ref, ck_ref, cv_ref, ts_ref, tn_ref, sink_ref,
                        wo_ref, wg_ref, wu_ref, wd_ref,
                        a_ref, nk_ref, nv_ref, wo16_ref, wg16_ref, wu16_ref, wd16_ref):
    for src, dst in ((wo_ref, wo16_ref), (wg_ref, wg16_ref), (wu_ref, wu16_ref), (wd_ref, wd16_ref)):
        dst[...] = src[...].astype(BF16)

    rowid = lax.broadcasted_iota(jnp.int32, (N_HEADS, HEAD_DIM), 0)
    ts = ts_ref[...]
    tn = tn_ref[:, 0:1]
    sink = sink_ref[:, 0:1]
    pad = jnp.zeros((WINDOW - ATTN_SAMPLE_BT, WINDOW), F32)
    kn_t = jnp.concatenate([kn_ref[...], pad], axis=0).T
    vn_t = jnp.concatenate([vn_ref[...], pad], axis=0).T
    newest = lax.broadcasted_iota(jnp.int32, (WINDOW, WINDOW), 1) == WINDOW - 1

    def shifted(cache_ref, new_t, bb):
        return jnp.where(newest, new_t[:, bb:bb + 1], pltpu.roll(cache_ref[bb], WINDOW - 1, axis=1))

    qbds, scores = [], []
    for bb in range(ATTN_SAMPLE_BT):
        qb = q_ref[bb]
        qbd = jnp.concatenate([jnp.where(rowid < KV_REP, qb, 0.0), jnp.where(rowid >= KV_REP, qb, 0.0)], axis=1)
        qbds.append(qbd)
        scores.append(jnp.dot(qbd.astype(BF16), ck_ref[bb].astype(BF16), preferred_element_type=F32))
    probs, new_terms = [], []
    for bb in range(ATTN_SAMPLE_BT):
        s_c = scores[bb] * ATTN_SCALE + ts
        s_n = jnp.sum(qbds[bb] * kn_ref[bb:bb + 1, :], axis=-1, keepdims=True) * ATTN_SCALE + tn
        m = jnp.maximum(jnp.maximum(jnp.max(s_c, axis=-1, keepdims=True), s_n), sink)
        e_c = jnp.exp(s_c - m)
        e_n = jnp.exp(s_n - m)
        den = jnp.sum(e_c, axis=-1, keepdims=True) + e_n + jnp.exp(sink - m)
        probs.append((e_c / den).astype(BF16))
        new_terms.append((e_n / den) * vn_ref[bb:bb + 1, :])
    for bb in range(ATTN_SAMPLE_BT):
        o = lax.dot_general(probs[bb], cv_ref[bb].astype(BF16), _NT, preferred_element_type=F32) + new_terms[bb]
        a_ref[bb] = jnp.where(rowid < KV_REP, o[:, :HEAD_DIM], o[:, HEAD_DIM:])
    for bb in range(ATTN_SAMPLE_BT):
        nk_ref[bb] = shifted(ck_ref, kn_t, bb)
        nv_ref[bb] = shifted(cv_ref, vn_t, bb)


def _attn_sample(q3, proj, ck, cv, ts, tn, sink_b, weights):
    nb = q3.shape[0]
    bt = ATTN_SAMPLE_BT
    steps = nb // bt
    kv = N_KV_HEADS * HEAD_DIM

    def slab(w):
        rows, cols = w.shape
        assert rows % (steps * 16) == 0
        return pl.BlockSpec((rows // steps, cols), lambda i: (i, 0))

    return pl.pallas_call(
        _attn_sample_kernel,
        out_shape=(jax.ShapeDtypeStruct((nb, N_HEADS, HEAD_DIM), F32),
                   jax.ShapeDtypeStruct((nb, WINDOW, kv), F32),
                   jax.ShapeDtypeStruct((nb, WINDOW, kv), F32))
        + tuple(jax.ShapeDtypeStruct(w.shape, BF16) for w in weights),
        grid=(steps,),
        in_specs=[pl.BlockSpec((bt, N_HEADS, HEAD_DIM), lambda i: (i, 0, 0)),
                  pl.BlockSpec((bt, kv), lambda i: (i, COL_K // kv)),
                  pl.BlockSpec((bt, kv), lambda i: (i, COL_V // kv)),
                  pl.BlockSpec((bt, WINDOW, kv), lambda i: (i, 0, 0)),
                  pl.BlockSpec((bt, WINDOW, kv), lambda i: (i, 0, 0)),
                  _const_spec((N_HEADS, WINDOW)),
                  _const_spec((N_HEADS, WINDOW)),
                  _const_spec((N_HEADS, WINDOW))] + [slab(w) for w in weights],
        out_specs=(pl.BlockSpec((bt, N_HEADS, HEAD_DIM), lambda i: (i, 0, 0)),
                   pl.BlockSpec((bt, WINDOW, kv), lambda i: (i, 0, 0)),
                   pl.BlockSpec((bt, WINDOW, kv), lambda i: (i, 0, 0))) + tuple(slab(w) for w in weights),
        compiler_params=pltpu.CompilerParams(dimension_semantics=("arbitrary",),
                                             vmem_limit_bytes=VMEM_LIMIT),
        name="attn_sample",
    )(q3, proj, proj, ck, cv, ts, tn, sink_b, *weights)


def _ssm_sample_prep_kernel(xbc_ref, dt_ref, cst_ref, cw_ref, cb_ref, dtb_ref, alog_ref,
                            xs_ref, bm_ref, cm_ref, xdt_t_ref, dec_ref, nc_ref):
    xbc = xbc_ref[...]
    tot = cst_ref[0] * cw_ref[0:1, :]
    tot = tot + cst_ref[1] * cw_ref[1:2, :]
    tot = tot + cst_ref[2] * cw_ref[2:3, :]
    tot = tot + xbc * cw_ref[3:4, :]
    xc = _silu(cb_ref[...] + tot)
    nc_ref[0] = cst_ref[1]
    nc_ref[1] = cst_ref[2]
    nc_ref[2] = xbc
    xs = xc[:, :SSM_WIDTH]
    xs_ref[...] = xs
    bm_ref[...] = xc[:, SSM_WIDTH:SSM_WIDTH + SSM_GROUPS * D_STATE]
    cm_ref[...] = xc[:, SSM_WIDTH + SSM_GROUPS * D_STATE:]
    dt = _softplus(dt_ref[...] + dtb_ref[...])
    dec_ref[...] = jnp.exp(dt * (-jnp.exp(alog_ref[...])))
    dt_t = dt.T
    for k in range(SSM_WIDTH // 128):
        xt = xs[:, k * 128:(k + 1) * 128].T
        for half in range(2):
            h = 2 * k + half
            lo = h * SSM_HEAD_DIM
            xdt_t_ref[lo:lo + SSM_HEAD_DIM, :] = xt[half * SSM_HEAD_DIM:(half + 1) * SSM_HEAD_DIM, :] * dt_t[h:h + 1, :]


def _ssm_sample_prep(proj, conv_state_t, conv_w, conv_b, dtb, alog):
    nb = proj.shape[0]
    return pl.pallas_call(
        _ssm_sample_prep_kernel,
        out_shape=(jax.ShapeDtypeStruct((nb, SSM_WIDTH), F32),
                   jax.ShapeDtypeStruct((nb, SSM_GROUPS * D_STATE), F32),
                   jax.ShapeDtypeStruct((nb, SSM_GROUPS * D_STATE), F32),
                   jax.ShapeDtypeStruct((SSM_WIDTH, nb), F32),
                   jax.ShapeDtypeStruct((nb, DT_PAD), F32),
                   jax.ShapeDtypeStruct((CONV_W - 1, nb, CONV_DIM), F32)),
        grid=(1,),
        in_specs=[pl.BlockSpec((nb, CONV_DIM), lambda i: (0, COL_XBC // CONV_DIM)),
                  pl.BlockSpec((nb, DT_PAD), lambda i: (0, COL_DT // DT_PAD)),
                  _const_spec((CONV_W - 1, nb, CONV_DIM)),
                  _const_spec((CONV_W, CONV_DIM)),
                  _const_spec((1, CONV_DIM)),
                  _const_spec((1, DT_PAD)),
                  _const_spec((1, DT_PAD))],
        out_specs=(_const_spec((nb, SSM_WIDTH)),
                   _const_spec((nb, SSM_GROUPS * D_STATE)),
                   _const_spec((nb, SSM_GROUPS * D_STATE)),
                   _const_spec((SSM_WIDTH, nb)),
                   _const_spec((nb, DT_PAD)),
                   _const_spec((CONV_W - 1, nb, CONV_DIM))),
        compiler_params=pltpu.CompilerParams(dimension_semantics=("arbitrary",),
                                             vmem_limit_bytes=VMEM_LIMIT),
        name="ssm_sample_prep",
    )(proj, proj, conv_state_t, conv_w, conv_b, dtb, alog)


def _ssm_sample_state_kernel(dec_ref, xdt_t_ref, bm_ref, cm_ref, xs_ref, z_ref, dsk_ref, nw_ref, st_ref,
                             nst_ref, s_ref, yt_ref):
    i = pl.program_id(0)
    last = pl.num_programs(0) - 1
    rows = SSM_REP * SSM_HEAD_DIM

    @pl.when(i == 0)
    def _():
        yt_ref[...] = jnp.zeros_like(yt_ref)

    lane = lax.broadcasted_iota(jnp.int32, (rows, 128), 1)
    for bb in range(SAMPLE_BT):
        b = i * SAMPLE_BT + bb
        mine = lane == b
        for g in range(SSM_GROUPS):
            bg = bm_ref[:, g * D_STATE:(g + 1) * D_STATE].astype(BF16)
            cg = cm_ref[:, g * D_STATE:(g + 1) * D_STATE].astype(BF16)
            xsel = jnp.where(mine, xdt_t_ref[g * rows:(g + 1) * rows, :], 0.0).astype(BF16)
            outer = jnp.dot(xsel, bg, preferred_element_type=F32)
            new = []
            for r in range(SSM_REP):
                h = g * SSM_REP + r
                hn = st_ref[bb, h] * dec_ref[b * SSM_HEADS + h] + outer[r * SSM_HEAD_DIM:(r + 1) * SSM_HEAD_DIM, :]
                nst_ref[bb, h] = hn
                new.append(hn)
            hcat = jnp.concatenate(new, axis=0).astype(BF16)
            res = lax.dot_general(hcat, cg, _NT, preferred_element_type=F32)
            yt_ref[g * rows:(g + 1) * rows, :] += jnp.where(mine, res, 0.0)

    @pl.when(i == last)
    def _():
        y = jnp.concatenate([yt_ref[k * 128:(k + 1) * 128, :].T for k in range(SSM_WIDTH // 128)], axis=1)
        y = y + xs_ref[...] * dsk_ref[...]
        s_ref[...] = _gated_group_norm(y, z_ref[...], nw_ref[...])


def _ssm_sample_state(dec_flat, xdt_t, bm, cm, xs, proj, dsk, nw, state):
    nb = state.shape[0]
    bt = SAMPLE_BT
    return pl.pallas_call(
        _ssm_sample_state_kernel,
        out_shape=(jax.ShapeDtypeStruct(state.shape, F32),
                   jax.ShapeDtypeStruct((nb, SSM_WIDTH), F32)),
        grid=(nb // bt,),
        in_specs=[pl.BlockSpec(memory_space=pltpu.SMEM),
                  _const_spec((SSM_WIDTH, nb)),
                  _const_spec((nb, SSM_GROUPS * D_STATE)),
                  _const_spec((nb, SSM_GROUPS * D_STATE)),
                  _const_spec((nb, SSM_WIDTH)),
                  pl.BlockSpec((nb, SSM_WIDTH), lambda i: (0, COL_Z // SSM_WIDTH)),
                  _const_spec((1, SSM_WIDTH)),
                  _const_spec((1, SSM_WIDTH)),
                  pl.BlockSpec((bt, SSM_HEADS, SSM_HEAD_DIM, D_STATE), lambda i: (i, 0, 0, 0))],
        out_specs=(pl.BlockSpec((bt, SSM_HEADS, SSM_HEAD_DIM, D_STATE), lambda i: (i, 0, 0, 0)),
                   pl.BlockSpec((nb, SSM_WIDTH), lambda i: (0, 0))),
        scratch_shapes=[pltpu.VMEM((SSM_WIDTH, nb), F32)],
        compiler_params=pltpu.CompilerParams(dimension_semantics=("arbitrary",),
                                             vmem_limit_bytes=VMEM_LIMIT),
        name="ssm_sample_state",
    )(dec_flat, xdt_t, bm, cm, xs, proj, dsk, nw, state)


def _pad_lanes(v, width):
    return jnp.pad(v.reshape(1, -1), ((0, 0), (0, width - v.shape[-1])))


def kernel(x_prompt, x_sample, cache_k, cache_v, state_conv, state_ssm, rel_bias, norm1_w, w_in, attn_sinks,
           conv_w, conv_b, dt_bias, A_log, D_skip, ssm_norm_w, w_out, norm2_w, w_gate, w_up, w_down, final_norm_w):
    depth = norm1_w.shape[0]
    assert depth == 1, "single-layer trunk"
    batch, seq, _ = x_prompt.shape
    nb = x_sample.shape[0]
    assert x_sample.shape[1] == 1 and nb == 128

    wi = w_in[0]
    q_c, k_c, v_c, z_c, xbc_c, dt_c = (wi[:, 0:512], wi[:, 512:640], wi[:, 640:768], wi[:, 768:1280],
                                       wi[:, 1280:2304], wi[:, 2304:2312])
    w_perm = jnp.concatenate([q_c, z_c, xbc_c, k_c, v_c, jnp.pad(dt_c, ((0, 0), (0, DT_PAD - SSM_HEADS)))],
                             axis=1).astype(BF16)
    n1 = norm1_w[0].reshape(1, D_MODEL)
    n2 = norm2_w[0].reshape(1, D_MODEL)
    fn = final_norm_w.reshape(1, D_MODEL)
    cw = conv_w[0]
    cb = conv_b[0].reshape(1, CONV_DIM)
    dtb = _pad_lanes(dt_bias[0], DT_PAD)
    alog = _pad_lanes(A_log[0], DT_PAD)
    dtb_t = jnp.broadcast_to(dt_bias[0][:, None], (SSM_HEADS, CHUNK))
    alog_t = jnp.broadcast_to(A_log[0][:, None], (SSM_HEADS, CHUNK))
    dsk = jnp.repeat(D_skip[0], SSM_HEAD_DIM).reshape(1, SSM_WIDTH)
    nw = ssm_norm_w[0].reshape(1, SSM_WIDTH)
    sinks = attn_sinks[0]
    sink_b = jnp.broadcast_to(sinks[:, None], (N_HEADS, WINDOW))

    tab_p, tab_s, tab_n = _bias_tables(rel_bias)

    xs2 = x_sample.reshape(nb, D_MODEL)
    proj_s = _inproj(xs2, n1, w_perm, nb)
    q3 = proj_s[:, COL_Q:COL_Q + ATTN_WIDTH].reshape(nb, N_HEADS, HEAD_DIM)
    ck = jnp.transpose(cache_k[0], (0, 2, 3, 1)).reshape(nb, N_KV_HEADS * HEAD_DIM, WINDOW)
    cv = jnp.transpose(cache_v[0], (0, 2, 3, 1)).reshape(nb, N_KV_HEADS * HEAD_DIM, WINDOW)
    a_s3, nk, nv, wo, wg, wu, wd = _attn_sample(q3, proj_s, ck, cv, tab_s, tab_n, sink_b,
                                                 (w_out[0], w_gate[0], w_up[0], w_down[0]))
    conv_t = jnp.transpose(state_conv[0], (1, 0, 2))
    xs_s, bm_s, cm_s, xdt_t, dec, nconv_t = _ssm_sample_prep(proj_s, conv_t, cw, cb, dtb, alog)
    dec_flat = dec[:, :SSM_HEADS].reshape(nb * SSM_HEADS)
    nssm, s_s = _ssm_sample_state(dec_flat, xdt_t, bm_s, cm_s, xs_s, proj_s, dsk, nw, state_ssm[0])
    mix_s = jnp.concatenate([a_s3.reshape(nb, ATTN_WIDTH), s_s], axis=1)
    y_s = _tail(xs2, mix_s, wo, n2, wg, wu, wd, fn, nb)

    xp2 = x_prompt.reshape(batch * seq, D_MODEL)
    y_p, kv_p, conv_p, ssm_p = _prompt_layer(xp2, n1, w_perm, sinks, tab_p, cw, cb, dtb_t, alog_t, dsk, nw,
                                              wo, n2, wg, wu, wd, fn, batch, seq)
    kv_p = kv_p.reshape(batch, WINDOW, 2, N_KV_HEADS, HEAD_DIM)

    return (y_p.reshape(batch, seq, D_MODEL),
            y_s.reshape(nb, 1, D_MODEL),
            kv_p[:, :, 0][None],
            kv_p[:, :, 1][None],
            conv_p[None],
            ssm_p[None],
            jnp.transpose(nk.reshape(nb, N_KV_HEADS, HEAD_DIM, WINDOW), (0, 3, 1, 2))[None],
            jnp.transpose(nv.reshape(nb, N_KV_HEADS, HEAD_DIM, WINDOW), (0, 3, 1, 2))[None],
            jnp.transpose(nconv_t, (1, 0, 2))[None],
            nssm[None])
```

```python
import functools
import math

import numpy as np
import jax
import jax.numpy as jnp
from jax import lax
from jax.experimental import pallas as pl
from jax.experimental.pallas import tpu as pltpu

F32 = jnp.float32
BF16 = jnp.bfloat16

D_MODEL = 1024
HEAD_DIM = 64
N_HEADS = 8
N_KV_HEADS = 2
KV_REP = 4
WINDOW = 128
ATTN_WIDTH = 512
ATTN_SCALE = HEAD_DIM ** -0.5
N_BUCKETS = 32
MAX_DISTANCE = 128
SSM_WIDTH = 512
SSM_HEADS = 8
SSM_GROUPS = 2
SSM_REP = 4
SSM_HEAD_DIM = 64
D_STATE = 128
CONV_W = 4
CONV_DIM = 1024
CHUNK = 128
D_FF = 2816
EPS = 1e-6

COL_Q, COL_Z, COL_XBC, COL_K, COL_V, COL_DT = 0, 512, 1024, 2048, 2176, 2304
PROJ_COLS = 2432
DT_PAD = 128
D_IN_PROJ = 2312
KV_COLS = 2 * N_KV_HEADS * HEAD_DIM

FF_CHUNK = 256
N_FF_CHUNKS = D_FF // FF_CHUNK

MIX_TILE = 512
MIX_BLOCKS = MIX_TILE // CHUNK
PROJ_PIECES = ((0, 512), (512, 1024), (1024, 1536), (1536, 2048), (2048, PROJ_COLS))

TAIL_TILE = 512

VMEM_LIMIT = 56 * 1024 * 1024
LAYER_VMEM_LIMIT = 60 * 1024 * 1024

_NT = (((1,), (1,)), ((), ()))


def _bucket_table(dist):
    n = np.maximum(dist, 0)
    exact = N_BUCKETS // 2
    nf = np.maximum(n, 1).astype(np.float32)
    large = exact + (np.log(nf / exact) / math.log(MAX_DISTANCE / exact) * (N_BUCKETS - exact)).astype(np.int32)
    return np.where(n < exact, n, np.minimum(large, N_BUCKETS - 1)).astype(np.int32)


def _prompt_buckets():
    i = np.arange(WINDOW)[:, None]
    j = np.arange(WINDOW)[None, :]
    dist = np.where(j > i, i + WINDOW - j, i - j)
    return _bucket_table(dist)


def _sample_buckets():
    dist = WINDOW - np.arange(WINDOW)[None, :]
    band = (dist >= 0) & (dist < WINDOW)
    row = np.where(band, _bucket_table(dist), -1).astype(np.int32)
    return np.tile(row, (N_HEADS, 1))


def _rms(x, w):
    return x * lax.rsqrt(jnp.mean(x * x, axis=-1, keepdims=True) + EPS) * w


def _silu(x):
    return x * (0.5 + 0.5 * jnp.tanh(0.5 * x))


def _softplus(x):
    return jnp.maximum(x, 0.0) + jnp.log1p(jnp.exp(-jnp.abs(x)))


def _const_spec(shape):
    nd = len(shape)
    return pl.BlockSpec(shape, lambda *_: (0,) * nd, pipeline_mode=pl.Buffered(1))


def _bias_kernel(rb_ref, bp_ref, bs_ref, tp_ref, ts_ref, tn_ref):
    bp = bp_ref[...]
    bs = bs_ref[...]
    rowid = lax.broadcasted_iota(jnp.int32, (N_HEADS, WINDOW), 0)
    ts = jnp.zeros((N_HEADS, WINDOW), F32)
    tn = jnp.zeros((N_HEADS, WINDOW), F32)
    for h in range(N_HEADS):
        tp = jnp.zeros((WINDOW, WINDOW), F32)
        for bk in range(N_BUCKETS):
            v = rb_ref[bk, h]
            tp = jnp.where(bp == bk, v, tp)
            ts = jnp.where((bs == bk) & (rowid == h), v, ts)
        tp_ref[h] = tp
        tn = jnp.where(rowid == h, rb_ref[0, h], tn)
    ts_ref[...] = jnp.where(bs < 0, -jnp.inf, ts)
    tn_ref[...] = tn


def _bias_tables(rel_bias):
    return pl.pallas_call(
        _bias_kernel,
        out_shape=(jax.ShapeDtypeStruct((N_HEADS, WINDOW, WINDOW), F32),
                   jax.ShapeDtypeStruct((N_HEADS, WINDOW), F32),
                   jax.ShapeDtypeStruct((N_HEADS, WINDOW), F32)),
        in_specs=[pl.BlockSpec(memory_space=pltpu.SMEM),
                  pl.BlockSpec(memory_space=pltpu.VMEM),
                  pl.BlockSpec(memory_space=pltpu.VMEM)],
        name="bias_tables",
    )(rel_bias, jnp.asarray(_prompt_buckets()), jnp.asarray(_sample_buckets()))


_W_IN_MOVES = ((0, 512, COL_Q), (512, 128, COL_K), (640, 128, COL_V), (768, 512, COL_Z), (1280, 1024, COL_XBC))


def _inproj_kernel(x_ref, nw_ref, w_ref, wdt_ref, o_ref, w16_ref):
    for src, width, dst in _W_IN_MOVES:
        w16_ref[:, dst:dst + width] = w_ref[:, src:src + width].astype(BF16)
    w16_ref[:, COL_DT:COL_DT + DT_PAD] = wdt_ref[...].astype(BF16)
    h = _rms(x_ref[...], nw_ref[...]).astype(BF16)
    o_ref[...] = jnp.dot(h, w16_ref[...], preferred_element_type=F32)


def _inproj(x2d, norm_w, w_in, w_dt_pad):
    n = x2d.shape[0]
    return pl.pallas_call(
        _inproj_kernel,
        out_shape=(jax.ShapeDtypeStruct((n, PROJ_COLS), F32),
                   jax.ShapeDtypeStruct((D_MODEL, PROJ_COLS), BF16)),
        grid=(1,),
        in_specs=[_const_spec((n, D_MODEL)),
                  _const_spec((1, D_MODEL)),
                  _const_spec(w_in.shape),
                  _const_spec((D_MODEL, DT_PAD))],
        out_specs=(_const_spec((n, PROJ_COLS)),
                   _const_spec((D_MODEL, PROJ_COLS))),
        compiler_params=pltpu.CompilerParams(dimension_semantics=("arbitrary",),
                                             vmem_limit_bytes=VMEM_LIMIT),
        name="inproj",
    )(x2d, norm_w, w_in, w_dt_pad)


def _head_variants(x, low):
    xr = pltpu.roll(x, HEAD_DIM, axis=1)
    zero = jnp.zeros_like(x)
    return ((jnp.where(low, x, zero).astype(BF16), jnp.where(low, zero, xr).astype(BF16)),
            (jnp.where(low, xr, zero).astype(BF16), jnp.where(low, zero, x).astype(BF16)))


def _gated_group_norm(y, z, nw):
    y = y * _silu(z)
    half = SSM_WIDTH // SSM_GROUPS
    parts = []
    for g in range(SSM_GROUPS):
        yg = y[:, g * half:(g + 1) * half]
        parts.append(yg * lax.rsqrt(jnp.mean(yg * yg, axis=-1, keepdims=True) + EPS))
    return jnp.concatenate(parts, axis=-1) * nw


def _attn_scores(pr, r0, kst_ref, vst_ref):
    w = WINDOW
    rows = slice(r0, r0 + w)
    low = lax.broadcasted_iota(jnp.int32, (w, w), 1) < HEAD_DIM

    for st, c0 in ((kst_ref, COL_K), (vst_ref, COL_V)):
        for g in range(N_KV_HEADS):
            st[g, 0:w, :] = st[g, w:2 * w, :]
            st[g, 2 * w:3 * w, :] = st[g, 3 * w:4 * w, :]
        var = _head_variants(pr[rows, c0:c0 + w], low)
        for g in range(N_KV_HEADS):
            st[g, w:2 * w, :] = var[g][0]
            st[g, 3 * w:4 * w, :] = var[g][1]

    scores = []
    for pp in range(N_HEADS // 2):
        q2 = pr[rows, COL_Q + pp * w:COL_Q + (pp + 1) * w].astype(BF16)
        scores.append(lax.dot_general(q2, kst_ref[pp // (KV_REP // 2)], _NT, preferred_element_type=F32))
    return scores


def _attn_softmax(scores, seq_start, sink_ref, tab_ref):
    w = WINDOW
    row = lax.broadcasted_iota(jnp.int32, (w, w), 0)
    lane = lax.broadcasted_iota(jnp.int32, (w, w), 1)
    upper = lane > row
    if seq_start is not None:
        no_prev = (lane - row) > jnp.where(seq_start, 0, w)
    p4s = []
    for pp in range(N_HEADS // 2):
        s4 = scores[pp]
        probs = []
        for half in range(2):
            hh = 2 * pp + half
            sp = s4[:, 2 * w * half:2 * w * half + w]
            sc = s4[:, 2 * w * half + w:2 * w * (half + 1)]
            s = jnp.where(upper, sp, sc) * ATTN_SCALE + tab_ref[hh]
            if seq_start is not None:
                s = jnp.where(no_prev, -jnp.inf, s)
            sk = sink_ref[hh]
            m = jnp.maximum(jnp.max(s, axis=-1, keepdims=True), sk)
            e = jnp.exp(s - m)
            den = jnp.sum(e, axis=-1, keepdims=True) + jnp.exp(sk - m)
            p = e / den
            zero = jnp.zeros_like(p)
            probs.append(jnp.where(upper, p, zero).astype(BF16))
            probs.append(jnp.where(upper, zero, p).astype(BF16))
        p4s.append(jnp.concatenate(probs, axis=1))
    return p4s


def _attn_values(p4s, r0, vst_ref, out_ref):
    w = WINDOW
    for pp in range(N_HEADS // 2):
        out_ref[r0:r0 + w, pp * w:(pp + 1) * w] = jnp.dot(p4s[pp], vst_ref[pp // (KV_REP // 2)],
                                                          preferred_element_type=F32).astype(out_ref.dtype)


def _ssd_decay(pr, r0, dtb_ref, alog_ref):
    rows = slice(r0, r0 + CHUNK)
    row = lax.broadcasted_iota(jnp.int32, (CHUNK, CHUNK), 0)
    lane = lax.broadcasted_iota(jnp.int32, (CHUNK, CHUNK), 1)
    dt_t = _softplus(pr[rows, COL_DT:COL_DT + DT_PAD].T[0:SSM_HEADS, :] + dtb_ref[...])
    a_t = dt_t * (-jnp.exp(alog_ref[...]))
    cs_t = jnp.dot(a_t, (row <= lane).astype(F32), precision=lax.Precision.HIGHEST,
                   preferred_element_type=F32)
    cs_end = cs_t[:, CHUNK - 1:CHUNK]
    w_t = jnp.exp(cs_end - cs_t) * dt_t
    dec_end = jnp.exp(cs_end)
    col = jnp.concatenate([cs_t, jnp.exp(cs_t), jnp.zeros((CHUNK - 2 * SSM_HEADS, CHUNK), F32)], axis=0).T
    return dt_t, cs_t, w_t, dec_end, col


def _ssd_conv(pr, r0, seq_start, cw_ref, cb_ref, hist_ref):
    x = pr[r0:r0 + CHUNK, COL_XBC:COL_XBC + CONV_DIM]
    hist = hist_ref[...]
    if seq_start is not None:
        hist = jnp.where(seq_start, 0.0, hist)
    xx = jnp.concatenate([hist, x], axis=0)
    tot = pltpu.roll(xx, CONV_W - 1, axis=0)[8:, :] * cw_ref[0:1, :]
    for j in range(1, CONV_W - 1):
        tot = tot + pltpu.roll(xx, CONV_W - 1 - j, axis=0)[8:, :] * cw_ref[j:j + 1, :]
    tot = tot + x * cw_ref[CONV_W - 1:CONV_W, :]
    hist_ref[...] = x[CHUNK - 8:, :]
    return _silu(cb_ref[...] + tot)


def _ssd_matmuls(pr, r0, seq_start, decay_terms, xc, dsk_ref, nw_ref, st_ref, out_ref):
    hd = SSM_HEAD_DIM
    rows = slice(r0, r0 + CHUNK)
    dt_t, cs_t, w_t, dec_end, col = decay_terms
    row = lax.broadcasted_iota(jnp.int32, (CHUNK, CHUNK), 0)
    lane = lax.broadcasted_iota(jnp.int32, (CHUNK, CHUNK), 1)
    causal = row >= lane
    low = lane < hd
    xs = xc[:, :SSM_WIDTH]
    bm = xc[:, SSM_WIDTH:SSM_WIDTH + SSM_GROUPS * D_STATE]
    cm = xc[:, SSM_WIDTH + SSM_GROUPS * D_STATE:]

    groups = range(SSM_GROUPS)
    bgs = [bm[:, g * D_STATE:(g + 1) * D_STATE].astype(BF16) for g in groups]
    cgs = [cm[:, g * D_STATE:(g + 1) * D_STATE].astype(BF16) for g in groups]
    h_prevs = []
    for g in groups:
        hp = st_ref[g * SSM_REP:(g + 1) * SSM_REP].reshape(SSM_REP * hd, D_STATE)
        if seq_start is not None:
            hp = jnp.where(seq_start, 0.0, hp)
        h_prevs.append(hp)
    cbs = [lax.dot_general(cgs[g], bgs[g], _NT, preferred_element_type=F32) for g in groups]
    y_offs = [lax.dot_general(cgs[g], h_prevs[g].astype(BF16), _NT, preferred_element_type=F32) for g in groups]

    for g in groups:
        xt, wts, decs = [], [], []
        for pr2 in range(SSM_REP // 2):
            h0 = g * SSM_REP + 2 * pr2
            xt.append(xs[:, h0 * hd:(h0 + 2) * hd].T)
            for h in (h0, h0 + 1):
                wts.append(jnp.broadcast_to(w_t[h:h + 1, :], (hd, CHUNK)))
                decs.append(jnp.broadcast_to(dec_end[h:h + 1, :], (hd, D_STATE)))
        xw = (jnp.concatenate(xt, axis=0) * jnp.concatenate(wts, axis=0)).astype(BF16)
        st = jnp.dot(xw, bgs[g], preferred_element_type=F32)
        h_new = h_prevs[g] * jnp.concatenate(decs, axis=0) + st
        st_ref[g * SSM_REP:(g + 1) * SSM_REP] = h_new.reshape(SSM_REP, hd, D_STATE)

    ys = []
    for g in groups:
        for pr2 in range(SSM_REP // 2):
            h0 = g * SSM_REP + 2 * pr2
            xpair = xs[:, h0 * hd:(h0 + 2) * hd]
            zero = jnp.zeros_like(xpair)
            m = []
            for h in (h0, h0 + 1):
                decay = jnp.exp(jnp.where(causal, col[:, h:h + 1] - cs_t[h:h + 1, :], -jnp.inf))
                m.append((cbs[g] * decay * dt_t[h:h + 1, :]).astype(BF16))
            y_diag = (jnp.dot(m[0], jnp.where(low, xpair, zero).astype(BF16), preferred_element_type=F32)
                      + jnp.dot(m[1], jnp.where(low, zero, xpair).astype(BF16), preferred_element_type=F32))
            e0 = jnp.broadcast_to(col[:, SSM_HEADS + h0:SSM_HEADS + h0 + 1], (CHUNK, CHUNK))
            e1 = jnp.broadcast_to(col[:, SSM_HEADS + h0 + 1:SSM_HEADS + h0 + 2], (CHUNK, CHUNK))
            ys.append(y_diag + y_offs[g][:, 2 * pr2 * hd:(2 * pr2 + 2) * hd] * jnp.where(low, e0, e1))
    y = jnp.concatenate(ys, axis=-1) + xs * dsk_ref[...]
    out_ref[rows, ATTN_WIDTH:ATTN_WIDTH + SSM_WIDTH] = _gated_group_norm(
        y, pr[rows, COL_Z:COL_Z + SSM_WIDTH], nw_ref[...]).astype(out_ref.dtype)


def _prompt_layer_kernel(tiles_per_seq, n_tiles, sink_ref, x_ref, xres_ref, n1_ref, w_ref, tab_ref, cw_ref,
                         cb_ref, dtb_ref, alog_ref, dsk_ref, nw_ref, wo_ref, n2_ref, wg_ref, wu_ref, wd_ref,
                         fn_ref, y_ref, kv_ref, conv_ref, ssm_ref,
                         proj_ref, mix_ref, kst_ref, vst_ref, hist_ref, st_ref):
    t = pl.program_id(0)

    @pl.when(t == 0)
    def _():
        kst_ref[...] = jnp.zeros_like(kst_ref)
        vst_ref[...] = jnp.zeros_like(vst_ref)
        hist_ref[...] = jnp.zeros_like(hist_ref)
        st_ref[...] = jnp.zeros_like(st_ref)

    slot = lax.rem(t, 2)
    pw = proj_ref.at[slot]
    pr = proj_ref.at[1 - slot]
    mw = mix_ref.at[1 - slot]
    mr = mix_ref.at[slot]
    seq_start = lax.rem(t + tiles_per_seq - 1, tiles_per_seq) == 0

    def normed_input():
        return _rms(x_ref[...], n1_ref[...]).astype(BF16)

    def project(h, piece):
        lo, hi = PROJ_PIECES[piece]
        pw[:, lo:hi] = jnp.dot(h, w_ref[:, lo:hi], preferred_element_type=F32)

    def ffn_input():
        x1 = xres_ref[...] + jnp.dot(mr[...], wo_ref[...], preferred_element_type=F32)
        return x1, _rms(x1, n2_ref[...]).astype(BF16)

    def ffn(h2, acc, chunk):
        cols = slice(chunk * FF_CHUNK, (chunk + 1) * FF_CHUNK)
        gate = jnp.dot(h2, wg_ref[:, cols], preferred_element_type=F32)
        up = jnp.dot(h2, wu_ref[:, cols], preferred_element_type=F32)
        act = (_silu(gate) * up).astype(BF16)
        part = jnp.dot(act, wd_ref[cols, :], preferred_element_type=F32)
        acc[0] = part if acc[0] is None else acc[0] + part

    def mix_block(j, between):
        r0 = j * CHUNK
        start = seq_start if j == 0 else None
        decay_terms = _ssd_decay(pr, r0, dtb_ref, alog_ref)
        scores = _attn_scores(pr, r0, kst_ref, vst_ref)
        between[0]()
        xc = _ssd_conv(pr, r0, start, cw_ref, cb_ref, hist_ref)
        between[1]()
        p4s = _attn_softmax(scores, start, sink_ref, tab_ref)
        _attn_values(p4s, r0, vst_ref, mw)
        between[2]()
        _ssd_matmuls(pr, r0, start, decay_terms, xc, dsk_ref, nw_ref, st_ref, mw)
        between[3]()

    def sequence_outputs():
        kv_ref[0] = pr[MIX_TILE - WINDOW:, COL_K:COL_K + KV_COLS]
        conv_ref[0] = pr[MIX_TILE - (CONV_W - 1):, COL_XBC:COL_XBC + CONV_DIM]
        ssm_ref[0] = st_ref[...]

    steady = t >= 2

    @pl.when(steady)
    def _():
        h = normed_input()
        x1, h2 = ffn_input()
        acc = [None]
        pieces = ([functools.partial(project, h, p) for p in range(len(PROJ_PIECES))]
                  + [functools.partial(ffn, h2, acc, c) for c in range(N_FF_CHUNKS)])
        order = [0, 5, 6, 7, 1, 8, 9, 10, 2, 11, 12, 13, 3, 14, 15, 4]
        assert len(order) == 4 * MIX_BLOCKS
        for j in range(MIX_BLOCKS):
            mix_block(j, [pieces[i] for i in order[4 * j:4 * j + 4]])
        y_ref[...] = _rms(x1 + acc[0], fn_ref[...])

    @pl.when(steady & (t <= n_tiles))
    def _():
        sequence_outputs()

    @pl.when(jnp.logical_not(steady))
    def _():
        h = normed_input()
        for p in range(len(PROJ_PIECES)):
            project(h, p)

    @pl.when(t == 1)
    def _():
        for j in range(MIX_BLOCKS):
            mix_block(j, [lambda: None] * 4)
        sequence_outputs()


def _prompt_layer(x2d, n1, w_perm, sinks, tab, conv_w, conv_b, dtb_t, alog_t, dsk, nw, wo, n2, wg, wu, wd, fn,
                  batch, seq):
    assert seq % MIX_TILE == 0
    tiles_per_seq = seq // MIX_TILE
    nt = batch * tiles_per_seq

    def mixed_tile(t):
        return jnp.clip(t - 1, 0, nt - 1)

    def seq_of(t):
        return mixed_tile(t) // tiles_per_seq

    def ffn_tile(t):
        return jnp.maximum(t - 2, 0)

    return pl.pallas_call(
        functools.partial(_prompt_layer_kernel, tiles_per_seq, nt),
        out_shape=(jax.ShapeDtypeStruct((batch * seq, D_MODEL), F32),
                   jax.ShapeDtypeStruct((batch, WINDOW, KV_COLS), F32),
                   jax.ShapeDtypeStruct((batch, CONV_W - 1, CONV_DIM), F32),
                   jax.ShapeDtypeStruct((batch, SSM_HEADS, SSM_HEAD_DIM, D_STATE), F32)),
        grid=(nt + 2,),
        in_specs=[pl.BlockSpec(memory_space=pltpu.SMEM),
                  pl.BlockSpec((MIX_TILE, D_MODEL), lambda t: (jnp.minimum(t, nt - 1), 0)),
                  pl.BlockSpec((MIX_TILE, D_MODEL), lambda t: (ffn_tile(t), 0)),
                  _const_spec((1, D_MODEL)),
                  _const_spec((D_MODEL, PROJ_COLS)),
                  _const_spec((N_HEADS, WINDOW, WINDOW)),
                  _const_spec((CONV_W, CONV_DIM)),
                  _const_spec((1, CONV_DIM)),
                  _const_spec((SSM_HEADS, CHUNK)),
                  _const_spec((SSM_HEADS, CHUNK)),
                  _const_spec((1, SSM_WIDTH)),
                  _const_spec((1, SSM_WIDTH)),
                  _const_spec((D_MODEL, D_MODEL)),
                  _const_spec((1, D_MODEL)),
                  _const_spec((D_MODEL, D_FF)),
                  _const_spec((D_MODEL, D_FF)),
                  _const_spec((D_FF, D_MODEL)),
                  _const_spec((1, D_MODEL))],
        out_specs=(pl.BlockSpec((MIX_TILE, D_MODEL), lambda t: (ffn_tile(t), 0)),
                   pl.BlockSpec((1, WINDOW, KV_COLS), lambda t: (seq_of(t), 0, 0)),
                   pl.BlockSpec((1, CONV_W - 1, CONV_DIM), lambda t: (seq_of(t), 0, 0)),
                   pl.BlockSpec((1, SSM_HEADS, SSM_HEAD_DIM, D_STATE), lambda t: (seq_of(t), 0, 0, 0))),
        scratch_shapes=[pltpu.VMEM((2, MIX_TILE, PROJ_COLS), F32),
                        pltpu.VMEM((2, MIX_TILE, ATTN_WIDTH + SSM_WIDTH), BF16),
                        pltpu.VMEM((N_KV_HEADS, 4 * WINDOW, WINDOW), BF16),
                        pltpu.VMEM((N_KV_HEADS, 4 * WINDOW, WINDOW), BF16),
                        pltpu.VMEM((8, CONV_DIM), F32),
                        pltpu.VMEM((SSM_HEADS, SSM_HEAD_DIM, D_STATE), F32)],
        compiler_params=pltpu.CompilerParams(dimension_semantics=("arbitrary",),
                                             vmem_limit_bytes=LAYER_VMEM_LIMIT),
        name="prompt_layer",
    )(sinks, x2d, x2d, n1, w_perm, tab, conv_w, conv_b, dtb_t, alog_t, dsk, nw, wo, n2, wg, wu, wd, fn)


def _tail_kernel(x_ref, mix_ref, wo_ref, n2_ref, wg_ref, wu_ref, wd_ref, fn_ref, o_ref):
    x1 = x_ref[...] + jnp.dot(mix_ref[...].astype(BF16), wo_ref[...], preferred_element_type=F32)
    h2 = _rms(x1, n2_ref[...]).astype(BF16)
    acc = None
    for j in range(N_FF_CHUNKS):
        cols = slice(j * FF_CHUNK, (j + 1) * FF_CHUNK)
        gate = jnp.dot(h2, wg_ref[:, cols], preferred_element_type=F32)
        up = jnp.dot(h2, wu_ref[:, cols], preferred_element_type=F32)
        act = (_silu(gate) * up).astype(BF16)
        part = jnp.dot(act, wd_ref[cols, :], preferred_element_type=F32)
        acc = part if acc is None else acc + part
    o_ref[...] = _rms(x1 + acc, fn_ref[...])


def _tail(x2d, mix, wo, n2, wg, wu, wd, fn, tm):
    n = x2d.shape[0]
    return pl.pallas_call(
        _tail_kernel,
        out_shape=jax.ShapeDtypeStruct((n, D_MODEL), F32),
        grid=(n // tm,),
        in_specs=[pl.BlockSpec((tm, D_MODEL), lambda i: (i, 0)),
                  pl.BlockSpec((tm, ATTN_WIDTH + SSM_WIDTH), lambda i: (i, 0)),
                  _const_spec((D_MODEL, D_MODEL)),
                  _const_spec((1, D_MODEL)),
                  _const_spec((D_MODEL, D_FF)),
                  _const_spec((D_MODEL, D_FF)),
                  _const_spec((D_FF, D_MODEL)),
                  _const_spec((1, D_MODEL))],
        out_specs=pl.BlockSpec((tm, D_MODEL), lambda i: (i, 0)),
        compiler_params=pltpu.CompilerParams(dimension_semantics=("arbitrary",),
                                             vmem_limit_bytes=VMEM_LIMIT),
        name="outproj_ffn",
    )(x2d, mix, wo, n2, wg, wu, wd, fn)


SAMPLE_BT = 8
ATTN_SAMPLE_BT = 16


def _attn_sample_kernel(q_ref, kn_ref, vn_ref, ck_ref, cv_ref, ts_ref, tn_ref, sink_ref,
                        wo_ref, wg_ref, wu_ref, wd_ref,
                        a_ref, nk_ref, nv_ref, wo16_ref, wg16_ref, wu16_ref, wd16_ref):
    for src, dst in ((wo_ref, wo16_ref), (wg_ref, wg16_ref), (wu_ref, wu16_ref), (wd_ref, wd16_ref)):
        dst[...] = src[...].astype(BF16)

    rowid = lax.broadcasted_iota(jnp.int32, (N_HEADS, HEAD_DIM), 0)
    ts = ts_ref[...]
    tn = tn_ref[:, 0:1]
    sink = sink_ref[:, 0:1]
    pad = jnp.zeros((WINDOW - ATTN_SAMPLE_BT, WINDOW), F32)
    kn_t = jnp.concatenate([kn_ref[...], pad], axis=0).T
    vn_t = jnp.concatenate([vn_ref[...], pad], axis=0).T
    newest = lax.broadcasted_iota(jnp.int32, (WINDOW, WINDOW), 1) == WINDOW - 1

    def shifted(cache_ref, new_t, bb):
        return jnp.where(newest, new_t[:, bb:bb + 1], pltpu.roll(cache_ref[bb], WINDOW - 1, axis=1))

    qbds, scores = [], []
    for bb in range(ATTN_SAMPLE_BT):
        qb = q_ref[bb]
        qbd = jnp.concatenate([jnp.where(rowid < KV_REP, qb, 0.0), jnp.where(rowid >= KV_REP, qb, 0.0)], axis=1)
        qbds.append(qbd)
        scores.append(jnp.dot(qbd.astype(BF16), ck_ref[bb].astype(BF16), preferred_element_type=F32))
    probs, new_terms = [], []
    for bb in range(ATTN_SAMPLE_BT):
        s_c = scores[bb] * ATTN_SCALE + ts
        s_n = jnp.sum(qbds[bb] * kn_ref[bb:bb + 1, :], axis=-1, keepdims=True) * ATTN_SCALE + tn
        m = jnp.maximum(jnp.maximum(jnp.max(s_c, axis=-1, keepdims=True), s_n), sink)
        e_c = jnp.exp(s_c - m)
        e_n = jnp.exp(s_n - m)
        den = jnp.sum(e_c, axis=-1, keepdims=True) + e_n + jnp.exp(sink - m)
        probs.append((e_c / den).astype(BF16))
        new_terms.append((e_n / den) * vn_ref[bb:bb + 1, :])
    for bb in range(ATTN_SAMPLE_BT):
        o = lax.dot_general(probs[bb], cv_ref[bb].astype(BF16), _NT, preferred_element_type=F32) + new_terms[bb]
        a_ref[bb] = jnp.where(rowid < KV_REP, o[:, :HEAD_DIM], o[:, HEAD_DIM:])
    for bb in range(ATTN_SAMPLE_BT):
        nk_ref[bb] = shifted(ck_ref, kn_t, bb)
        nv_ref[bb] = shifted(cv_ref, vn_t, bb)


def _attn_sample(q3, proj, ck, cv, ts, tn, sink_b, weights):
    nb = q3.shape[0]
    bt = ATTN_SAMPLE_BT
    steps = nb // bt
    kv = N_KV_HEADS * HEAD_DIM

    def slab(w):
        rows, cols = w.shape
        assert rows % (steps * 16) == 0
        return pl.BlockSpec((rows // steps, cols), lambda i: (i, 0))

    return pl.pallas_call(
        _attn_sample_kernel,
        out_shape=(jax.ShapeDtypeStruct((nb, N_HEADS, HEAD_DIM), F32),
                   jax.ShapeDtypeStruct((nb, WINDOW, kv), F32),
                   jax.ShapeDtypeStruct((nb, WINDOW, kv), F32))
        + tuple(jax.ShapeDtypeStruct(w.shape, BF16) for w in weights),
        grid=(steps,),
        in_specs=[pl.BlockSpec((bt, N_HEADS, HEAD_DIM), lambda i: (i, 0, 0)),
                  pl.BlockSpec((bt, kv), lambda i: (i, COL_K // kv)),
                  pl.BlockSpec((bt, kv), lambda i: (i, COL_V // kv)),
                  pl.BlockSpec((bt, WINDOW, kv), lambda i: (i, 0, 0)),
                  pl.BlockSpec((bt, WINDOW, kv), lambda i: (i, 0, 0)),
                  _const_spec((N_HEADS, WINDOW)),
                  _const_spec((N_HEADS, WINDOW)),
                  _const_spec((N_HEADS, WINDOW))] + [slab(w) for w in weights],
        out_specs=(pl.BlockSpec((bt, N_HEADS, HEAD_DIM), lambda i: (i, 0, 0)),
                   pl.BlockSpec((bt, WINDOW, kv), lambda i: (i, 0, 0)),
                   pl.BlockSpec((bt, WINDOW, kv), lambda i: (i, 0, 0))) + tuple(slab(w) for w in weights),
        compiler_params=pltpu.CompilerParams(dimension_semantics=("arbitrary",),
                                             vmem_limit_bytes=VMEM_LIMIT),
        name="attn_sample",
    )(q3, proj, proj, ck, cv, ts, tn, sink_b, *weights)


def _ssm_sample_prep_kernel(xbc_ref, dt_ref, cst_ref, cw_ref, cb_ref, dtb_ref, alog_ref,
                            xs_ref, bm_ref, cm_ref, xdt_t_ref, dec_ref, nc_ref):
    xbc = xbc_ref[...]
    tot = cst_ref[0] * cw_ref[0:1, :]
    tot = tot + cst_ref[1] * cw_ref[1:2, :]
    tot = tot + cst_ref[2] * cw_ref[2:3, :]
    tot = tot + xbc * cw_ref[3:4, :]
    xc = _silu(cb_ref[...] + tot)
    nc_ref[0] = cst_ref[1]
    nc_ref[1] = cst_ref[2]
    nc_ref[2] = xbc
    xs = xc[:, :SSM_WIDTH]
    xs_ref[...] = xs
    bm_ref[...] = xc[:, SSM_WIDTH:SSM_WIDTH + SSM_GROUPS * D_STATE]
    cm_ref[...] = xc[:, SSM_WIDTH + SSM_GROUPS * D_STATE:]
    dt = _softplus(dt_ref[...] + dtb_ref[...])
    dec_ref[...] = jnp.exp(dt * (-jnp.exp(alog_ref[...])))
    dt_t = dt.T
    for k in range(SSM_WIDTH // 128):
        xt = xs[:, k * 128:(k + 1) * 128].T
        for half in range(2):
            h = 2 * k + half
            lo = h * SSM_HEAD_DIM
            xdt_t_ref[lo:lo + SSM_HEAD_DIM, :] = xt[half * SSM_HEAD_DIM:(half + 1) * SSM_HEAD_DIM, :] * dt_t[h:h + 1, :]


def _ssm_sample_prep(proj, conv_state_t, conv_w, conv_b, dtb, alog):
    nb = proj.shape[0]
    return pl.pallas_call(
        _ssm_sample_prep_kernel,
        out_shape=(jax.ShapeDtypeStruct((nb, SSM_WIDTH), F32),
                   jax.ShapeDtypeStruct((nb, SSM_GROUPS * D_STATE), F32),
                   jax.ShapeDtypeStruct((nb, SSM_GROUPS * D_STATE), F32),
                   jax.ShapeDtypeStruct((SSM_WIDTH, nb), F32),
                   jax.ShapeDtypeStruct((nb, DT_PAD), F32),
                   jax.ShapeDtypeStruct((CONV_W - 1, nb, CONV_DIM), F32)),
        grid=(1,),
        in_specs=[pl.BlockSpec((nb, CONV_DIM), lambda i: (0, COL_XBC // CONV_DIM)),
                  pl.BlockSpec((nb, DT_PAD), lambda i: (0, COL_DT // DT_PAD)),
                  _const_spec((CONV_W - 1, nb, CONV_DIM)),
                  _const_spec((CONV_W, CONV_DIM)),
                  _const_spec((1, CONV_DIM)),
                  _const_spec((1, DT_PAD)),
                  _const_spec((1, DT_PAD))],
        out_specs=(_const_spec((nb, SSM_WIDTH)),
                   _const_spec((nb, SSM_GROUPS * D_STATE)),
                   _const_spec((nb, SSM_GROUPS * D_STATE)),
                   _const_spec((SSM_WIDTH, nb)),
                   _const_spec((nb, DT_PAD)),
                   _const_spec((CONV_W - 1, nb, CONV_DIM))),
        compiler_params=pltpu.CompilerParams(dimension_semantics=("arbitrary",),
                                             vmem_limit_bytes=VMEM_LIMIT),
        name="ssm_sample_prep",
    )(proj, proj, conv_state_t, conv_w, conv_b, dtb, alog)


def _ssm_sample_state_kernel(dec_ref, xdt_t_ref, bm_ref, cm_ref, xs_ref, z_ref, dsk_ref, nw_ref, st_ref,
                             nst_ref, s_ref, yt_ref):
    i = pl.program_id(0)
    last = pl.num_programs(0) - 1
    rows = SSM_REP * SSM_HEAD_DIM

    @pl.when(i == 0)
    def _():
        yt_ref[...] = jnp.zeros_like(yt_ref)

    lane = lax.broadcasted_iota(jnp.int32, (rows, 128), 1)
    for bb in range(SAMPLE_BT):
        b = i * SAMPLE_BT + bb
        mine = lane == b
        for g in range(SSM_GROUPS):
            bg = bm_ref[:, g * D_STATE:(g + 1) * D_STATE].astype(BF16)
            cg = cm_ref[:, g * D_STATE:(g + 1) * D_STATE].astype(BF16)
            xsel = jnp.where(mine, xdt_t_ref[g * rows:(g + 1) * rows, :], 0.0).astype(BF16)
            outer = jnp.dot(xsel, bg, preferred_element_type=F32)
            new = []
            for r in range(SSM_REP):
                h = g * SSM_REP + r
                hn = st_ref[bb, h] * dec_ref[b * SSM_HEADS + h] + outer[r * SSM_HEAD_DIM:(r + 1) * SSM_HEAD_DIM, :]
                nst_ref[bb, h] = hn
                new.append(hn)
            hcat = jnp.concatenate(new, axis=0).astype(BF16)
            res = lax.dot_general(hcat, cg, _NT, preferred_element_type=F32)
            yt_ref[g * rows:(g + 1) * rows, :] += jnp.where(mine, res, 0.0)

    @pl.when(i == last)
    def _():
        y = jnp.concatenate([yt_ref[k * 128:(k + 1) * 128, :].T for k in range(SSM_WIDTH // 128)], axis=1)
        y = y + xs_ref[...] * dsk_ref[...]
        s_ref[...] = _gated_group_norm(y, z_ref[...], nw_ref[...])


def _ssm_sample_state(dec_flat, xdt_t, bm, cm, xs, proj, dsk, nw, state):
    nb = state.shape[0]
    bt = SAMPLE_BT
    return pl.pallas_call(
        _ssm_sample_state_kernel,
        out_shape=(jax.ShapeDtypeStruct(state.shape, F32),
                   jax.ShapeDtypeStruct((nb, SSM_WIDTH), F32)),
        grid=(nb // bt,),
        in_specs=[pl.BlockSpec(memory_space=pltpu.SMEM),
                  _const_spec((SSM_WIDTH, nb)),
                  _const_spec((nb, SSM_GROUPS * D_STATE)),
                  _const_spec((nb, SSM_GROUPS * D_STATE)),
                  _const_spec((nb, SSM_WIDTH)),
                  pl.BlockSpec((nb, SSM_WIDTH), lambda i: (0, COL_Z // SSM_WIDTH)),
                  _const_spec((1, SSM_WIDTH)),
                  _const_spec((1, SSM_WIDTH)),
                  pl.BlockSpec((bt, SSM_HEADS, SSM_HEAD_DIM, D_STATE), lambda i: (i, 0, 0, 0))],
        out_specs=(pl.BlockSpec((bt, SSM_HEADS, SSM_HEAD_DIM, D_STATE), lambda i: (i, 0, 0, 0)),
                   pl.BlockSpec((nb, SSM_WIDTH), lambda i: (0, 0))),
        scratch_shapes=[pltpu.VMEM((SSM_WIDTH, nb), F32)],
        compiler_params=pltpu.CompilerParams(dimension_semantics=("arbitrary",),
                                             vmem_limit_bytes=VMEM_LIMIT),
        name="ssm_sample_state",
    )(dec_flat, xdt_t, bm, cm, xs, proj, dsk, nw, state)


def _pad_lanes(v, width):
    return jnp.pad(v.reshape(1, -1), ((0, 0), (0, width - v.shape[-1])))


def kernel(x_prompt, x_sample, cache_k, cache_v, state_conv, state_ssm, rel_bias, norm1_w, w_in, attn_sinks,
           conv_w, conv_b, dt_bias, A_log, D_skip, ssm_norm_w, w_out, norm2_w, w_gate, w_up, w_down, final_norm_w):
    depth = norm1_w.shape[0]
    assert depth == 1, "single-layer trunk"
    batch, seq, _ = x_prompt.shape
    nb = x_sample.shape[0]
    assert x_sample.shape[1] == 1 and nb == 128

    w_dt_pad = jnp.pad(w_in[0][:, D_IN_PROJ - SSM_HEADS:], ((0, 0), (0, DT_PAD - SSM_HEADS)))
    n1 = norm1_w[0].reshape(1, D_MODEL)
    n2 = norm2_w[0].reshape(1, D_MODEL)
    fn = final_norm_w.reshape(1, D_MODEL)
    cw = conv_w[0]
    cb = conv_b[0].reshape(1, CONV_DIM)
    dtb = _pad_lanes(dt_bias[0], DT_PAD)
    alog = _pad_lanes(A_log[0], DT_PAD)
    dtb_t = jnp.broadcast_to(dt_bias[0][:, None], (SSM_HEADS, CHUNK))
    alog_t = jnp.broadcast_to(A_log[0][:, None], (SSM_HEADS, CHUNK))
    dsk = jnp.repeat(D_skip[0], SSM_HEAD_DIM).reshape(1, SSM_WIDTH)
    nw = ssm_norm_w[0].reshape(1, SSM_WIDTH)
    sinks = attn_sinks[0]
    sink_b = jnp.broadcast_to(sinks[:, None], (N_HEADS, WINDOW))

    tab_p, tab_s, tab_n = _bias_tables(rel_bias)

    xs2 = x_sample.reshape(nb, D_MODEL)
    proj_s, w_perm = _inproj(xs2, n1, w_in[0], w_dt_pad)
    q3 = proj_s[:, COL_Q:COL_Q + ATTN_WIDTH].reshape(nb, N_HEADS, HEAD_DIM)
    ck = jnp.transpose(cache_k[0], (0, 2, 3, 1)).reshape(nb, N_KV_HEADS * HEAD_DIM, WINDOW)
    cv = jnp.transpose(cache_v[0], (0, 2, 3, 1)).reshape(nb, N_KV_HEADS * HEAD_DIM, WINDOW)
    a_s3, nk, nv, wo, wg, wu, wd = _attn_sample(q3, proj_s, ck, cv, tab_s, tab_n, sink_b,
                                                 (w_out[0], w_gate[0], w_up[0], w_down[0]))
    conv_t = jnp.transpose(state_conv[0], (1, 0, 2))
    xs_s, bm_s, cm_s, xdt_t, dec, nconv_t = _ssm_sample_prep(proj_s, conv_t, cw, cb, dtb, alog)
    dec_flat = dec[:, :SSM_HEADS].reshape(nb * SSM_HEADS)
    nssm, s_s = _ssm_sample_state(dec_flat, xdt_t, bm_s, cm_s, xs_s, proj_s, dsk, nw, state_ssm[0])
    mix_s = jnp.concatenate([a_s3.reshape(nb, ATTN_WIDTH), s_s], axis=1)
    y_s = _tail(xs2, mix_s, wo, n2, wg, wu, wd, fn, nb)

    xp2 = x_prompt.reshape(batch * seq, D_MODEL)
    y_p, kv_p, conv_p, ssm_p = _prompt_layer(xp2, n1, w_perm, sinks, tab_p, cw, cb, dtb_t, alog_t, dsk, nw,
                                              wo, n2, wg, wu, wd, fn, batch, seq)
    kv_p = kv_p.reshape(batch, WINDOW, 2, N_KV_HEADS, HEAD_DIM)

    return (y_p.reshape(batch, seq, D_MODEL),
            y_s.reshape(nb, 1, D_MODEL),
            kv_p[:, :, 0][None],
            kv_p[:, :, 1][None],
            conv_p[None],
            ssm_p[None],
            jnp.transpose(nk.reshape(nb, N_KV_HEADS, HEAD_DIM, WINDOW), (0, 3, 1, 2))[None],
            jnp.transpose(nv.reshape(nb, N_KV_HEADS, HEAD_DIM, WINDOW), (0, 3, 1, 2))[None],
            jnp.transpose(nconv_t, (1, 0, 2))[None],
            nssm[None])
```

```python
import functools
import math

import numpy as np
import jax
import jax.numpy as jnp
from jax import lax
from jax.experimental import pallas as pl
from jax.experimental.pallas import tpu as pltpu

F32 = jnp.float32
BF16 = jnp.bfloat16

D_MODEL = 1024
HEAD_DIM = 64
N_HEADS = 8
N_KV_HEADS = 2
KV_REP = 4
WINDOW = 128
ATTN_WIDTH = 512
ATTN_SCALE = HEAD_DIM ** -0.5
N_BUCKETS = 32
MAX_DISTANCE = 128
SSM_WIDTH = 512
SSM_HEADS = 8
SSM_GROUPS = 2
SSM_REP = 4
SSM_HEAD_DIM = 64
D_STATE = 128
CONV_W = 4
CONV_DIM = 1024
CHUNK = 128
D_FF = 2816
EPS = 1e-6

COL_Q, COL_Z, COL_XBC, COL_K, COL_V, COL_DT = 0, 512, 1024, 2048, 2176, 2304
PROJ_COLS = 2432
DT_PAD = 128
KV_COLS = 2 * N_KV_HEADS * HEAD_DIM

FF_CHUNK = 256
N_FF_CHUNKS = D_FF // FF_CHUNK

MIX_TILE = 512
MIX_BLOCKS = MIX_TILE // CHUNK
PROJ_PIECES = ((0, 512), (512, 1024), (1024, 1536), (1536, 2048), (2048, PROJ_COLS))

TAIL_TILE = 512

VMEM_LIMIT = 56 * 1024 * 1024
LAYER_VMEM_LIMIT = 60 * 1024 * 1024

_NT = (((1,), (1,)), ((), ()))


def _bucket_table(dist):
    n = np.maximum(dist, 0)
    exact = N_BUCKETS // 2
    nf = np.maximum(n, 1).astype(np.float32)
    large = exact + (np.log(nf / exact) / math.log(MAX_DISTANCE / exact) * (N_BUCKETS - exact)).astype(np.int32)
    return np.where(n < exact, n, np.minimum(large, N_BUCKETS - 1)).astype(np.int32)


def _prompt_buckets():
    i = np.arange(WINDOW)[:, None]
    j = np.arange(WINDOW)[None, :]
    dist = np.where(j > i, i + WINDOW - j, i - j)
    return _bucket_table(dist)


def _sample_buckets():
    dist = WINDOW - np.arange(WINDOW)[None, :]
    band = (dist >= 0) & (dist < WINDOW)
    row = np.where(band, _bucket_table(dist), -1).astype(np.int32)
    return np.tile(row, (N_HEADS, 1))


def _rms(x, w):
    return x * lax.rsqrt(jnp.mean(x * x, axis=-1, keepdims=True) + EPS) * w


def _silu(x):
    return x * (0.5 + 0.5 * jnp.tanh(0.5 * x))


def _softplus(x):
    return jnp.maximum(x, 0.0) + jnp.log1p(jnp.exp(-jnp.abs(x)))


def _const_spec(shape):
    nd = len(shape)
    return pl.BlockSpec(shape, lambda *_: (0,) * nd, pipeline_mode=pl.Buffered(1))


def _bias_kernel(rb_ref, bp_ref, bs_ref, tp_ref, ts_ref, tn_ref):
    bp = bp_ref[...]
    bs = bs_ref[...]
    rowid = lax.broadcasted_iota(jnp.int32, (N_HEADS, WINDOW), 0)
    ts = jnp.zeros((N_HEADS, WINDOW), F32)
    tn = jnp.zeros((N_HEADS, WINDOW), F32)
    for h in range(N_HEADS):
        tp = jnp.zeros((WINDOW, WINDOW), F32)
        for bk in range(N_BUCKETS):
            v = rb_ref[bk, h]
            tp = jnp.where(bp == bk, v, tp)
            ts = jnp.where((bs == bk) & (rowid == h), v, ts)
        tp_ref[h] = tp
        tn = jnp.where(rowid == h, rb_ref[0, h], tn)
    ts_ref[...] = jnp.where(bs < 0, -jnp.inf, ts)
    tn_ref[...] = tn


def _bias_tables(rel_bias):
    return pl.pallas_call(
        _bias_kernel,
        out_shape=(jax.ShapeDtypeStruct((N_HEADS, WINDOW, WINDOW), F32),
                   jax.ShapeDtypeStruct((N_HEADS, WINDOW), F32),
                   jax.ShapeDtypeStruct((N_HEADS, WINDOW), F32)),
        in_specs=[pl.BlockSpec(memory_space=pltpu.SMEM),
                  pl.BlockSpec(memory_space=pltpu.VMEM),
                  pl.BlockSpec(memory_space=pltpu.VMEM)],
        name="bias_tables",
    )(rel_bias, jnp.asarray(_prompt_buckets()), jnp.asarray(_sample_buckets()))


def _inproj_kernel(x_ref, nw_ref, w_ref, o_ref):
    h = _rms(x_ref[...], nw_ref[...]).astype(BF16)
    o_ref[...] = jnp.dot(h, w_ref[...], preferred_element_type=F32)


def _inproj(x2d, norm_w, w_perm, tm):
    n = x2d.shape[0]
    return pl.pallas_call(
        _inproj_kernel,
        out_shape=jax.ShapeDtypeStruct((n, PROJ_COLS), F32),
        grid=(n // tm,),
        in_specs=[pl.BlockSpec((tm, D_MODEL), lambda i: (i, 0)),
                  _const_spec((1, D_MODEL)),
                  _const_spec((D_MODEL, PROJ_COLS))],
        out_specs=pl.BlockSpec((tm, PROJ_COLS), lambda i: (i, 0)),
        compiler_params=pltpu.CompilerParams(dimension_semantics=("arbitrary",),
                                             vmem_limit_bytes=VMEM_LIMIT),
        name="inproj",
    )(x2d, norm_w, w_perm)


def _head_variants(x, low):
    xr = pltpu.roll(x, HEAD_DIM, axis=1)
    zero = jnp.zeros_like(x)
    return ((jnp.where(low, x, zero).astype(BF16), jnp.where(low, zero, xr).astype(BF16)),
            (jnp.where(low, xr, zero).astype(BF16), jnp.where(low, zero, x).astype(BF16)))


def _gated_group_norm(y, z, nw):
    y = y * _silu(z)
    half = SSM_WIDTH // SSM_GROUPS
    parts = []
    for g in range(SSM_GROUPS):
        yg = y[:, g * half:(g + 1) * half]
        parts.append(yg * lax.rsqrt(jnp.mean(yg * yg, axis=-1, keepdims=True) + EPS))
    return jnp.concatenate(parts, axis=-1) * nw


def _attn_scores(pr, r0, kst_ref, vst_ref):
    w = WINDOW
    rows = slice(r0, r0 + w)
    low = lax.broadcasted_iota(jnp.int32, (w, w), 1) < HEAD_DIM

    for st, c0 in ((kst_ref, COL_K), (vst_ref, COL_V)):
        for g in range(N_KV_HEADS):
            st[g, 0:w, :] = st[g, w:2 * w, :]
            st[g, 2 * w:3 * w, :] = st[g, 3 * w:4 * w, :]
        var = _head_variants(pr[rows, c0:c0 + w], low)
        for g in range(N_KV_HEADS):
            st[g, w:2 * w, :] = var[g][0]
            st[g, 3 * w:4 * w, :] = var[g][1]

    scores = []
    for pp in range(N_HEADS // 2):
        q2 = pr[rows, COL_Q + pp * w:COL_Q + (pp + 1) * w].astype(BF16)
        scores.append(lax.dot_general(q2, kst_ref[pp // (KV_REP // 2)], _NT, preferred_element_type=F32))
    return scores


def _attn_softmax(scores, seq_start, sink_ref, tab_ref):
    w = WINDOW
    row = lax.broadcasted_iota(jnp.int32, (w, w), 0)
    lane = lax.broadcasted_iota(jnp.int32, (w, w), 1)
    upper = lane > row
    if seq_start is not None:
        no_prev = (lane - row) > jnp.where(seq_start, 0, w)
    p4s = []
    for pp in range(N_HEADS // 2):
        s4 = scores[pp]
        probs = []
        for half in range(2):
            hh = 2 * pp + half
            sp = s4[:, 2 * w * half:2 * w * half + w]
            sc = s4[:, 2 * w * half + w:2 * w * (half + 1)]
            s = jnp.where(upper, sp, sc) * ATTN_SCALE + tab_ref[hh]
            if seq_start is not None:
                s = jnp.where(no_prev, -jnp.inf, s)
            sk = sink_ref[hh]
            m = jnp.maximum(jnp.max(s, axis=-1, keepdims=True), sk)
            e = jnp.exp(s - m)
            den = jnp.sum(e, axis=-1, keepdims=True) + jnp.exp(sk - m)
            p = e / den
            zero = jnp.zeros_like(p)
            probs.append(jnp.where(upper, p, zero).astype(BF16))
            probs.append(jnp.where(upper, zero, p).astype(BF16))
        p4s.append(jnp.concatenate(probs, axis=1))
    return p4s


def _attn_values(p4s, r0, vst_ref, out_ref):
    w = WINDOW
    for pp in range(N_HEADS // 2):
        out_ref[r0:r0 + w, pp * w:(pp + 1) * w] = jnp.dot(p4s[pp], vst_ref[pp // (KV_REP // 2)],
                                                          preferred_element_type=F32).astype(out_ref.dtype)


def _ssd_decay(pr, r0, dtb_ref, alog_ref):
    rows = slice(r0, r0 + CHUNK)
    lane = lax.broadcasted_iota(jnp.int32, (SSM_HEADS, CHUNK), 1)
    dt_t = _softplus(pr[rows, COL_DT:COL_DT + DT_PAD].T[0:SSM_HEADS, :] + dtb_ref[...])
    a_t = dt_t * (-jnp.exp(alog_ref[...]))
    cs_t = a_t
    shift = 1
    while shift < CHUNK:
        cs_t = cs_t + jnp.where(lane >= shift, pltpu.roll(cs_t, shift, axis=1), 0.0)
        shift *= 2
    cs_end = cs_t[:, CHUNK - 1:CHUNK]
    w_t = jnp.exp(cs_end - cs_t) * dt_t
    dec_end = jnp.exp(cs_end)
    col = jnp.concatenate([cs_t, jnp.exp(cs_t), jnp.zeros((CHUNK - 2 * SSM_HEADS, CHUNK), F32)], axis=0).T
    return dt_t, cs_t, w_t, dec_end, col


def _ssd_conv(pr, r0, seq_start, cw_ref, cb_ref, hist_ref):
    x = pr[r0:r0 + CHUNK, COL_XBC:COL_XBC + CONV_DIM]
    hist = hist_ref[...]
    if seq_start is not None:
        hist = jnp.where(seq_start, 0.0, hist)
    xx = jnp.concatenate([hist, x], axis=0)
    tot = pltpu.roll(xx, CONV_W - 1, axis=0)[8:, :] * cw_ref[0:1, :]
    for j in range(1, CONV_W - 1):
        tot = tot + pltpu.roll(xx, CONV_W - 1 - j, axis=0)[8:, :] * cw_ref[j:j + 1, :]
    tot = tot + x * cw_ref[CONV_W - 1:CONV_W, :]
    hist_ref[...] = x[CHUNK - 8:, :]
    return _silu(cb_ref[...] + tot)


def _ssd_matmuls(pr, r0, seq_start, decay_terms, xc, dsk_ref, nw_ref, st_ref, out_ref):
    hd = SSM_HEAD_DIM
    rows = slice(r0, r0 + CHUNK)
    dt_t, cs_t, w_t, dec_end, col = decay_terms
    row = lax.broadcasted_iota(jnp.int32, (CHUNK, CHUNK), 0)
    lane = lax.broadcasted_iota(jnp.int32, (CHUNK, CHUNK), 1)
    causal = row >= lane
    xs = xc[:, :SSM_WIDTH]
    bm = xc[:, SSM_WIDTH:SSM_WIDTH + SSM_GROUPS * D_STATE]
    cm = xc[:, SSM_WIDTH + SSM_GROUPS * D_STATE:]

    groups = range(SSM_GROUPS)
    bgs = [bm[:, g * D_STATE:(g + 1) * D_STATE].astype(BF16) for g in groups]
    cgs = [cm[:, g * D_STATE:(g + 1) * D_STATE].astype(BF16) for g in groups]
    h_prevs = []
    for g in groups:
        hp = st_ref[g * SSM_REP:(g + 1) * SSM_REP].reshape(SSM_REP * hd, D_STATE)
        if seq_start is not None:
            hp = jnp.where(seq_start, 0.0, hp)
        h_prevs.append(hp)
    cbs = [lax.dot_general(cgs[g], bgs[g], _NT, preferred_element_type=F32) for g in groups]
    y_offs = [lax.dot_general(cgs[g], h_prevs[g].astype(BF16), _NT, preferred_element_type=F32) for g in groups]

    for g in groups:
        xt, wts, decs = [], [], []
        for pr2 in range(SSM_REP // 2):
            h0 = g * SSM_REP + 2 * pr2
            xt.append(xs[:, h0 * hd:(h0 + 2) * hd].T)
            for h in (h0, h0 + 1):
                wts.append(jnp.broadcast_to(w_t[h:h + 1, :], (hd, CHUNK)))
                decs.append(jnp.broadcast_to(dec_end[h:h + 1, :], (hd, D_STATE)))
        xw = (jnp.concatenate(xt, axis=0) * jnp.concatenate(wts, axis=0)).astype(BF16)
        st = jnp.dot(xw, bgs[g], preferred_element_type=F32)
        h_new = h_prevs[g] * jnp.concatenate(decs, axis=0) + st
        st_ref[g * SSM_REP:(g + 1) * SSM_REP] = h_new.reshape(SSM_REP, hd, D_STATE)

    gw = SSM_REP * hd
    seg = lax.broadcasted_iota(jnp.int32, (CHUNK, gw), 1) // hd
    ys = []
    for g in groups:
        xg = xs[:, g * gw:(g + 1) * gw]
        zero = jnp.zeros_like(xg)
        ms, xstack, scale = [], [], None
        for r in range(SSM_REP):
            h = g * SSM_REP + r
            decay = jnp.exp(jnp.where(causal, col[:, h:h + 1] - cs_t[h:h + 1, :], -jnp.inf))
            ms.append((cbs[g] * decay * dt_t[h:h + 1, :]).astype(BF16))
            xstack.append(jnp.where(seg == r, xg, zero).astype(BF16))
            e_h = jnp.broadcast_to(col[:, SSM_HEADS + h:SSM_HEADS + h + 1], (CHUNK, gw))
            scale = e_h if scale is None else jnp.where(seg == r, e_h, scale)
        y_diag = jnp.dot(jnp.concatenate(ms, axis=1), jnp.concatenate(xstack, axis=0),
                         preferred_element_type=F32)
        ys.append(y_diag + y_offs[g] * scale)
    y = jnp.concatenate(ys, axis=-1) + xs * dsk_ref[...]
    out_ref[rows, ATTN_WIDTH:ATTN_WIDTH + SSM_WIDTH] = _gated_group_norm(
        y, pr[rows, COL_Z:COL_Z + SSM_WIDTH], nw_ref[...]).astype(out_ref.dtype)


def _prompt_layer_kernel(tiles_per_seq, n_tiles, sink_ref, x_ref, xres_ref, n1_ref, w_ref, tab_ref, cw_ref,
                         cb_ref, dtb_ref, alog_ref, dsk_ref, nw_ref, wo_ref, n2_ref, wg_ref, wu_ref, wd_ref,
                         fn_ref, y_ref, kv_ref, conv_ref, ssm_ref,
                         proj_ref, mix_ref, kst_ref, vst_ref, hist_ref, st_ref):
    t = pl.program_id(0)

    @pl.when(t == 0)
    def _():
        kst_ref[...] = jnp.zeros_like(kst_ref)
        vst_ref[...] = jnp.zeros_like(vst_ref)
        hist_ref[...] = jnp.zeros_like(hist_ref)
        st_ref[...] = jnp.zeros_like(st_ref)

    slot = lax.rem(t, 2)
    pw = proj_ref.at[slot]
    pr = proj_ref.at[1 - slot]
    mw = mix_ref.at[1 - slot]
    mr = mix_ref.at[slot]
    seq_start = lax.rem(t + tiles_per_seq - 1, tiles_per_seq) == 0

    def normed_input():
        return _rms(x_ref[...], n1_ref[...]).astype(BF16)

    def project(h, piece):
        lo, hi = PROJ_PIECES[piece]
        pw[:, lo:hi] = jnp.dot(h, w_ref[:, lo:hi], preferred_element_type=F32)

    def ffn_input():
        x1 = xres_ref[...] + jnp.dot(mr[...], wo_ref[...], preferred_element_type=F32)
        return x1, _rms(x1, n2_ref[...]).astype(BF16)

    def ffn(h2, acc, chunk):
        cols = slice(chunk * FF_CHUNK, (chunk + 1) * FF_CHUNK)
        gate = jnp.dot(h2, wg_ref[:, cols], preferred_element_type=F32)
        up = jnp.dot(h2, wu_ref[:, cols], preferred_element_type=F32)
        act = (_silu(gate) * up).astype(BF16)
        part = jnp.dot(act, wd_ref[cols, :], preferred_element_type=F32)
        acc[0] = part if acc[0] is None else acc[0] + part

    def mix_block(j, between):
        r0 = j * CHUNK
        start = seq_start if j == 0 else None
        decay_terms = _ssd_decay(pr, r0, dtb_ref, alog_ref)
        scores = _attn_scores(pr, r0, kst_ref, vst_ref)
        between[0]()
        xc = _ssd_conv(pr, r0, start, cw_ref, cb_ref, hist_ref)
        between[1]()
        p4s = _attn_softmax(scores, start, sink_ref, tab_ref)
        _attn_values(p4s, r0, vst_ref, mw)
        between[2]()
        _ssd_matmuls(pr, r0, start, decay_terms, xc, dsk_ref, nw_ref, st_ref, mw)
        between[3]()

    def sequence_outputs():
        kv_ref[0] = pr[MIX_TILE - WINDOW:, COL_K:COL_K + KV_COLS]
        conv_ref[0] = pr[MIX_TILE - (CONV_W - 1):, COL_XBC:COL_XBC + CONV_DIM]
        ssm_ref[0] = st_ref[...]

    steady = t >= 2

    @pl.when(steady)
    def _():
        h = normed_input()
        x1, h2 = ffn_input()
        acc = [None]
        pieces = ([functools.partial(project, h, p) for p in range(len(PROJ_PIECES))]
                  + [functools.partial(ffn, h2, acc, c) for c in range(N_FF_CHUNKS)])
        order = [0, 5, 6, 7, 1, 8, 9, 10, 2, 11, 12, 13, 3, 14, 15, 4]
        assert len(order) == 4 * MIX_BLOCKS
        for j in range(MIX_BLOCKS):
            mix_block(j, [pieces[i] for i in order[4 * j:4 * j + 4]])
        y_ref[...] = _rms(x1 + acc[0], fn_ref[...])

    @pl.when(steady & (t <= n_tiles))
    def _():
        sequence_outputs()

    @pl.when(jnp.logical_not(steady))
    def _():
        h = normed_input()
        for p in range(len(PROJ_PIECES)):
            project(h, p)

    @pl.when(t == 1)
    def _():
        for j in range(MIX_BLOCKS):
            mix_block(j, [lambda: None] * 4)
        sequence_outputs()


def _prompt_layer(x2d, n1, w_perm, sinks, tab, conv_w, conv_b, dtb_t, alog_t, dsk, nw, wo, n2, wg, wu, wd, fn,
                  batch, seq):
    assert seq % MIX_TILE == 0
    tiles_per_seq = seq // MIX_TILE
    nt = batch * tiles_per_seq

    def mixed_tile(t):
        return jnp.clip(t - 1, 0, nt - 1)

    def seq_of(t):
        return mixed_tile(t) // tiles_per_seq

    def ffn_tile(t):
        return jnp.maximum(t - 2, 0)

    return pl.pallas_call(
        functools.partial(_prompt_layer_kernel, tiles_per_seq, nt),
        out_shape=(jax.ShapeDtypeStruct((batch * seq, D_MODEL), F32),
                   jax.ShapeDtypeStruct((batch, WINDOW, KV_COLS), F32),
                   jax.ShapeDtypeStruct((batch, CONV_W - 1, CONV_DIM), F32),
                   jax.ShapeDtypeStruct((batch, SSM_HEADS, SSM_HEAD_DIM, D_STATE), F32)),
        grid=(nt + 2,),
        in_specs=[pl.BlockSpec(memory_space=pltpu.SMEM),
                  pl.BlockSpec((MIX_TILE, D_MODEL), lambda t: (jnp.minimum(t, nt - 1), 0)),
                  pl.BlockSpec((MIX_TILE, D_MODEL), lambda t: (ffn_tile(t), 0)),
                  _const_spec((1, D_MODEL)),
                  _const_spec((D_MODEL, PROJ_COLS)),
                  _const_spec((N_HEADS, WINDOW, WINDOW)),
                  _const_spec((CONV_W, CONV_DIM)),
                  _const_spec((1, CONV_DIM)),
                  _const_spec((SSM_HEADS, CHUNK)),
                  _const_spec((SSM_HEADS, CHUNK)),
                  _const_spec((1, SSM_WIDTH)),
                  _const_spec((1, SSM_WIDTH)),
                  _const_spec((D_MODEL, D_MODEL)),
                  _const_spec((1, D_MODEL)),
                  _const_spec((D_MODEL, D_FF)),
                  _const_spec((D_MODEL, D_FF)),
                  _const_spec((D_FF, D_MODEL)),
                  _const_spec((1, D_MODEL))],
        out_specs=(pl.BlockSpec((MIX_TILE, D_MODEL), lambda t: (ffn_tile(t), 0)),
                   pl.BlockSpec((1, WINDOW, KV_COLS), lambda t: (seq_of(t), 0, 0)),
                   pl.BlockSpec((1, CONV_W - 1, CONV_DIM), lambda t: (seq_of(t), 0, 0)),
                   pl.BlockSpec((1, SSM_HEADS, SSM_HEAD_DIM, D_STATE), lambda t: (seq_of(t), 0, 0, 0))),
        scratch_shapes=[pltpu.VMEM((2, MIX_TILE, PROJ_COLS), F32),
                        pltpu.VMEM((2, MIX_TILE, ATTN_WIDTH + SSM_WIDTH), BF16),
                        pltpu.VMEM((N_KV_HEADS, 4 * WINDOW, WINDOW), BF16),
                        pltpu.VMEM((N_KV_HEADS, 4 * WINDOW, WINDOW), BF16),
                        pltpu.VMEM((8, CONV_DIM), F32),
                        pltpu.VMEM((SSM_HEADS, SSM_HEAD_DIM, D_STATE), F32)],
        compiler_params=pltpu.CompilerParams(dimension_semantics=("arbitrary",),
                                             vmem_limit_bytes=LAYER_VMEM_LIMIT),
        name="prompt_layer",
    )(sinks, x2d, x2d, n1, w_perm, tab, conv_w, conv_b, dtb_t, alog_t, dsk, nw, wo, n2, wg, wu, wd, fn)


def _tail_kernel(x_ref, mix_ref, wo_ref, n2_ref, wg_ref, wu_ref, wd_ref, fn_ref, o_ref):
    x1 = x_ref[...] + jnp.dot(mix_ref[...].astype(BF16), wo_ref[...], preferred_element_type=F32)
    h2 = _rms(x1, n2_ref[...]).astype(BF16)
    acc = None
    for j in range(N_FF_CHUNKS):
        cols = slice(j * FF_CHUNK, (j + 1) * FF_CHUNK)
        gate = jnp.dot(h2, wg_ref[:, cols], preferred_element_type=F32)
        up = jnp.dot(h2, wu_ref[:, cols], preferred_element_type=F32)
        act = (_silu(gate) * up).astype(BF16)
        part = jnp.dot(act, wd_ref[cols, :], preferred_element_type=F32)
        acc = part if acc is None else acc + part
    o_ref[...] = _rms(x1 + acc, fn_ref[...])


def _tail(x2d, mix, wo, n2, wg, wu, wd, fn, tm):
    n = x2d.shape[0]
    return pl.pallas_call(
        _tail_kernel,
        out_shape=jax.ShapeDtypeStruct((n, D_MODEL), F32),
        grid=(n // tm,),
        in_specs=[pl.BlockSpec((tm, D_MODEL), lambda i: (i, 0)),
                  pl.BlockSpec((tm, ATTN_WIDTH + SSM_WIDTH), lambda i: (i, 0)),
                  _const_spec((D_MODEL, D_MODEL)),
                  _const_spec((1, D_MODEL)),
                  _const_spec((D_MODEL, D_FF)),
                  _const_spec((D_MODEL, D_FF)),
                  _const_spec((D_FF, D_MODEL)),
                  _const_spec((1, D_MODEL))],
        out_specs=pl.BlockSpec((tm, D_MODEL), lambda i: (i, 0)),
        compiler_params=pltpu.CompilerParams(dimension_semantics=("arbitrary",),
                                             vmem_limit_bytes=VMEM_LIMIT),
        name="outproj_ffn",
    )(x2d, mix, wo, n2, wg, wu, wd, fn)


SAMPLE_BT = 8
ATTN_SAMPLE_BT = 16


def _attn_sample_kernel(q_ref, kn_ref, vn_ref, ck_ref, cv_ref, ts_ref, tn_ref, sink_ref,
                        wo_ref, wg_ref, wu_ref, wd_ref,
                        a_ref, nk_ref, nv_ref, wo16_ref, wg16_ref, wu16_ref, wd16_ref):
    for src, dst in ((wo_ref, wo16_ref), (wg_ref, wg16_ref), (wu_ref, wu16_ref), (wd_ref, wd16_ref)):
        dst[...] = src[...].astype(BF16)

    rowid = lax.broadcasted_iota(jnp.int32, (N_HEADS, HEAD_DIM), 0)
    ts = ts_ref[...]
    tn = tn_ref[:, 0:1]
    sink = sink_ref[:, 0:1]
    pad = jnp.zeros((WINDOW - ATTN_SAMPLE_BT, WINDOW), F32)
    kn_t = jnp.concatenate([kn_ref[...], pad], axis=0).T
    vn_t = jnp.concatenate([vn_ref[...], pad], axis=0).T
    newest = lax.broadcasted_iota(jnp.int32, (WINDOW, WINDOW), 1) == WINDOW - 1

    def shifted(cache_ref, new_t, bb):
        return jnp.where(newest, new_t[:, bb:bb + 1], pltpu.roll(cache_ref[bb], WINDOW - 1, axis=1))

    qbds, scores = [], []
    for bb in range(ATTN_SAMPLE_BT):
        qb = q_ref[bb]
        qbd = jnp.concatenate([jnp.where(rowid < KV_REP, qb, 0.0), jnp.where(rowid >= KV_REP, qb, 0.0)], axis=1)
        qbds.append(qbd)
        scores.append(jnp.dot(qbd.astype(BF16), ck_ref[bb].astype(BF16), preferred_element_type=F32))
    probs, new_terms = [], []
    for bb in range(ATTN_SAMPLE_BT):
        s_c = scores[bb] * ATTN_SCALE + ts
        s_n = jnp.sum(qbds[bb] * kn_ref[bb:bb + 1, :], axis=-1, keepdims=True) * ATTN_SCALE + tn
        m = jnp.maximum(jnp.maximum(jnp.max(s_c, axis=-1, keepdims=True), s_n), sink)
        e_c = jnp.exp(s_c - m)
        e_n = jnp.exp(s_n - m)
        den = jnp.sum(e_c, axis=-1, keepdims=True) + e_n + jnp.exp(sink - m)
        probs.append((e_c / den).astype(BF16))
        new_terms.append((e_n / den) * vn_ref[bb:bb + 1, :])
    for bb in range(ATTN_SAMPLE_BT):
        o = lax.dot_general(probs[bb], cv_ref[bb].astype(BF16), _NT, preferred_element_type=F32) + new_terms[bb]
        a_ref[bb] = jnp.where(rowid < KV_REP, o[:, :HEAD_DIM], o[:, HEAD_DIM:])
    for bb in range(ATTN_SAMPLE_BT):
        nk_ref[bb] = shifted(ck_ref, kn_t, bb)
        nv_ref[bb] = shifted(cv_ref, vn_t, bb)


def _attn_sample(q3, proj, ck, cv, ts, tn, sink_b, weights):
    nb = q3.shape[0]
    bt = ATTN_SAMPLE_BT
    steps = nb // bt
    kv = N_KV_HEADS * HEAD_DIM

    def slab(w):
        rows, cols = w.shape
        assert rows % (steps * 16) == 0
        return pl.BlockSpec((rows // steps, cols), lambda i: (i, 0))

    return pl.pallas_call(
        _attn_sample_kernel,
        out_shape=(jax.ShapeDtypeStruct((nb, N_HEADS, HEAD_DIM), F32),
                   jax.ShapeDtypeStruct((nb, WINDOW, kv), F32),
                   jax.ShapeDtypeStruct((nb, WINDOW, kv), F32))
        + tuple(jax.ShapeDtypeStruct(w.shape, BF16) for w in weights),
        grid=(steps,),
        in_specs=[pl.BlockSpec((bt, N_HEADS, HEAD_DIM), lambda i: (i, 0, 0)),
                  pl.BlockSpec((bt, kv), lambda i: (i, COL_K // kv)),
                  pl.BlockSpec((bt, kv), lambda i: (i, COL_V // kv)),
                  pl.BlockSpec((bt, WINDOW, kv), lambda i: (i, 0, 0)),
                  pl.BlockSpec((bt, WINDOW, kv), lambda i: (i, 0, 0)),
                  _const_spec((N_HEADS, WINDOW)),
                  _const_spec((N_HEADS, WINDOW)),
                  _const_spec((N_HEADS, WINDOW))] + [slab(w) for w in weights],
        out_specs=(pl.BlockSpec((bt, N_HEADS, HEAD_DIM), lambda i: (i, 0, 0)),
                   pl.BlockSpec((bt, WINDOW, kv), lambda i: (i, 0, 0)),
                   pl.BlockSpec((bt, WINDOW, kv), lambda i: (i, 0, 0))) + tuple(slab(w) for w in weights),
        compiler_params=pltpu.CompilerParams(dimension_semantics=("arbitrary",),
                                             vmem_limit_bytes=VMEM_LIMIT),
        name="attn_sample",
    )(q3, proj, proj, ck, cv, ts, tn, sink_b, *weights)


def _ssm_sample_prep_kernel(xbc_ref, dt_ref, cst_ref, cw_ref, cb_ref, dtb_ref, alog_ref,
                            xs_ref, bm_ref, cm_ref, xdt_t_ref, dec_ref, nc_ref):
    xbc = xbc_ref[...]
    tot = cst_ref[0] * cw_ref[0:1, :]
    tot = tot + cst_ref[1] * cw_ref[1:2, :]
    tot = tot + cst_ref[2] * cw_ref[2:3, :]
    tot = tot + xbc * cw_ref[3:4, :]
    xc = _silu(cb_ref[...] + tot)
    nc_ref[0] = cst_ref[1]
    nc_ref[1] = cst_ref[2]
    nc_ref[2] = xbc
    xs = xc[:, :SSM_WIDTH]
    xs_ref[...] = xs
    bm_ref[...] = xc[:, SSM_WIDTH:SSM_WIDTH + SSM_GROUPS * D_STATE]
    cm_ref[...] = xc[:, SSM_WIDTH + SSM_GROUPS * D_STATE:]
    dt = _softplus(dt_ref[...] + dtb_ref[...])
    dec_ref[...] = jnp.exp(dt * (-jnp.exp(alog_ref[...])))
    dt_t = dt.T
    for k in range(SSM_WIDTH // 128):
        xt = xs[:, k * 128:(k + 1) * 128].T
        for half in range(2):
            h = 2 * k + half
            lo = h * SSM_HEAD_DIM
            xdt_t_ref[lo:lo + SSM_HEAD_DIM, :] = xt[half * SSM_HEAD_DIM:(half + 1) * SSM_HEAD_DIM, :] * dt_t[h:h + 1, :]


def _ssm_sample_prep(proj, conv_state_t, conv_w, conv_b, dtb, alog):
    nb = proj.shape[0]
    return pl.pallas_call(
        _ssm_sample_prep_kernel,
        out_shape=(jax.ShapeDtypeStruct((nb, SSM_WIDTH), F32),
                   jax.ShapeDtypeStruct((nb, SSM_GROUPS * D_STATE), F32),
                   jax.ShapeDtypeStruct((nb, SSM_GROUPS * D_STATE), F32),
                   jax.ShapeDtypeStruct((SSM_WIDTH, nb), F32),
                   jax.ShapeDtypeStruct((nb, DT_PAD), F32),
                   jax.ShapeDtypeStruct((CONV_W - 1, nb, CONV_DIM), F32)),
        grid=(1,),
        in_specs=[pl.BlockSpec((nb, CONV_DIM), lambda i: (0, COL_XBC // CONV_DIM)),
                  pl.BlockSpec((nb, DT_PAD), lambda i: (0, COL_DT // DT_PAD)),
                  _const_spec((CONV_W - 1, nb, CONV_DIM)),
                  _const_spec((CONV_W, CONV_DIM)),
                  _const_spec((1, CONV_DIM)),
                  _const_spec((1, DT_PAD)),
                  _const_spec((1, DT_PAD))],
        out_specs=(_const_spec((nb, SSM_WIDTH)),
                   _const_spec((nb, SSM_GROUPS * D_STATE)),
                   _const_spec((nb, SSM_GROUPS * D_STATE)),
                   _const_spec((SSM_WIDTH, nb)),
                   _const_spec((nb, DT_PAD)),
                   _const_spec((CONV_W - 1, nb, CONV_DIM))),
        compiler_params=pltpu.CompilerParams(dimension_semantics=("arbitrary",),
                                             vmem_limit_bytes=VMEM_LIMIT),
        name="ssm_sample_prep",
    )(proj, proj, conv_state_t, conv_w, conv_b, dtb, alog)


def _ssm_sample_state_kernel(dec_ref, xdt_t_ref, bm_ref, cm_ref, xs_ref, z_ref, dsk_ref, nw_ref, st_ref,
                             nst_ref, s_ref, yt_ref):
    i = pl.program_id(0)
    last = pl.num_programs(0) - 1
    rows = SSM_REP * SSM_HEAD_DIM

    @pl.when(i == 0)
    def _():
        yt_ref[...] = jnp.zeros_like(yt_ref)

    lane = lax.broadcasted_iota(jnp.int32, (rows, 128), 1)
    for bb in range(SAMPLE_BT):
        b = i * SAMPLE_BT + bb
        mine = lane == b
        for g in range(SSM_GROUPS):
            bg = bm_ref[:, g * D_STATE:(g + 1) * D_STATE].astype(BF16)
            cg = cm_ref[:, g * D_STATE:(g + 1) * D_STATE].astype(BF16)
            xsel = jnp.where(mine, xdt_t_ref[g * rows:(g + 1) * rows, :], 0.0).astype(BF16)
            outer = jnp.dot(xsel, bg, preferred_element_type=F32)
            new = []
            for r in range(SSM_REP):
                h = g * SSM_REP + r
                hn = st_ref[bb, h] * dec_ref[b * SSM_HEADS + h] + outer[r * SSM_HEAD_DIM:(r + 1) * SSM_HEAD_DIM, :]
                nst_ref[bb, h] = hn
                new.append(hn)
            hcat = jnp.concatenate(new, axis=0).astype(BF16)
            res = lax.dot_general(hcat, cg, _NT, preferred_element_type=F32)
            yt_ref[g * rows:(g + 1) * rows, :] += jnp.where(mine, res, 0.0)

    @pl.when(i == last)
    def _():
        y = jnp.concatenate([yt_ref[k * 128:(k + 1) * 128, :].T for k in range(SSM_WIDTH // 128)], axis=1)
        y = y + xs_ref[...] * dsk_ref[...]
        s_ref[...] = _gated_group_norm(y, z_ref[...], nw_ref[...])


def _ssm_sample_state(dec_flat, xdt_t, bm, cm, xs, proj, dsk, nw, state):
    nb = state.shape[0]
    bt = SAMPLE_BT
    return pl.pallas_call(
        _ssm_sample_state_kernel,
        out_shape=(jax.ShapeDtypeStruct(state.shape, F32),
                   jax.ShapeDtypeStruct((nb, SSM_WIDTH), F32)),
        grid=(nb // bt,),
        in_specs=[pl.BlockSpec(memory_space=pltpu.SMEM),
                  _const_spec((SSM_WIDTH, nb)),
                  _const_spec((nb, SSM_GROUPS * D_STATE)),
                  _const_spec((nb, SSM_GROUPS * D_STATE)),
                  _const_spec((nb, SSM_WIDTH)),
                  pl.BlockSpec((nb, SSM_WIDTH), lambda i: (0, COL_Z // SSM_WIDTH)),
                  _const_spec((1, SSM_WIDTH)),
                  _const_spec((1, SSM_WIDTH)),
                  pl.BlockSpec((bt, SSM_HEADS, SSM_HEAD_DIM, D_STATE), lambda i: (i, 0, 0, 0))],
        out_specs=(pl.BlockSpec((bt, SSM_HEADS, SSM_HEAD_DIM, D_STATE), lambda i: (i, 0, 0, 0)),
                   pl.BlockSpec((nb, SSM_WIDTH), lambda i: (0, 0))),
        scratch_shapes=[pltpu.VMEM((SSM_WIDTH, nb), F32)],
        compiler_params=pltpu.CompilerParams(dimension_semantics=("arbitrary",),
                                             vmem_limit_bytes=VMEM_LIMIT),
        name="ssm_sample_state",
    )(dec_flat, xdt_t, bm, cm, xs, proj, dsk, nw, state)


def _pad_lanes(v, width):
    return jnp.pad(v.reshape(1, -1), ((0, 0), (0, width - v.shape[-1])))


def kernel(x_prompt, x_sample, cache_k, cache_v, state_conv, state_ssm, rel_bias, norm1_w, w_in, attn_sinks,
           conv_w, conv_b, dt_bias, A_log, D_skip, ssm_norm_w, w_out, norm2_w, w_gate, w_up, w_down, final_norm_w):
    depth = norm1_w.shape[0]
    assert depth == 1, "single-layer trunk"
    batch, seq, _ = x_prompt.shape
    nb = x_sample.shape[0]
    assert x_sample.shape[1] == 1 and nb == 128

    wi = w_in[0]
    q_c, k_c, v_c, z_c, xbc_c, dt_c = (wi[:, 0:512], wi[:, 512:640], wi[:, 640:768], wi[:, 768:1280],
                                       wi[:, 1280:2304], wi[:, 2304:2312])
    w_perm = jnp.concatenate([q_c, z_c, xbc_c, k_c, v_c, jnp.pad(dt_c, ((0, 0), (0, DT_PAD - SSM_HEADS)))],
                             axis=1).astype(BF16)
    n1 = norm1_w[0].reshape(1, D_MODEL)
    n2 = norm2_w[0].reshape(1, D_MODEL)
    fn = final_norm_w.reshape(1, D_MODEL)
    cw = conv_w[0]
    cb = conv_b[0].reshape(1, CONV_DIM)
    dtb = _pad_lanes(dt_bias[0], DT_PAD)
    alog = _pad_lanes(A_log[0], DT_PAD)
    dtb_t = jnp.broadcast_to(dt_bias[0][:, None], (SSM_HEADS, CHUNK))
    alog_t = jnp.broadcast_to(A_log[0][:, None], (SSM_HEADS, CHUNK))
    dsk = jnp.repeat(D_skip[0], SSM_HEAD_DIM).reshape(1, SSM_WIDTH)
    nw = ssm_norm_w[0].reshape(1, SSM_WIDTH)
    sinks = attn_sinks[0]
    sink_b = jnp.broadcast_to(sinks[:, None], (N_HEADS, WINDOW))

    tab_p, tab_s, tab_n = _bias_tables(rel_bias)

    xs2 = x_sample.reshape(nb, D_MODEL)
    proj_s = _inproj(xs2, n1, w_perm, nb)
    q3 = proj_s[:, COL_Q:COL_Q + ATTN_WIDTH].reshape(nb, N_HEADS, HEAD_DIM)
    ck = jnp.transpose(cache_k[0], (0, 2, 3, 1)).reshape(nb, N_KV_HEADS * HEAD_DIM, WINDOW)
    cv = jnp.transpose(cache_v[0], (0, 2, 3, 1)).reshape(nb, N_KV_HEADS * HEAD_DIM, WINDOW)
    a_s3, nk, nv, wo, wg, wu, wd = _attn_sample(q3, proj_s, ck, cv, tab_s, tab_n, sink_b,
                                                 (w_out[0], w_gate[0], w_up[0], w_down[0]))
    conv_t = jnp.transpose(state_conv[0], (1, 0, 2))
    xs_s, bm_s, cm_s, xdt_t, dec, nconv_t = _ssm_sample_prep(proj_s, conv_t, cw, cb, dtb, alog)
    dec_flat = dec[:, :SSM_HEADS].reshape(nb * SSM_HEADS)
    nssm, s_s = _ssm_sample_state(dec_flat, xdt_t, bm_s, cm_s, xs_s, proj_s, dsk, nw, state_ssm[0])
    mix_s = jnp.concatenate([a_s3.reshape(nb, ATTN_WIDTH), s_s], axis=1)
    y_s = _tail(xs2, mix_s, wo, n2, wg, wu, wd, fn, nb)

    xp2 = x_prompt.reshape(batch * seq, D_MODEL)
    y_p, kv_p, conv_p, ssm_p = _prompt_layer(xp2, n1, w_perm, sinks, tab_p, cw, cb, dtb_t, alog_t, dsk, nw,
                                              wo, n2, wg, wu, wd, fn, batch, seq)
    kv_p = kv_p.reshape(batch, WINDOW, 2, N_KV_HEADS, HEAD_DIM)

    return (y_p.reshape(batch, seq, D_MODEL),
            y_s.reshape(nb, 1, D_MODEL),
            kv_p[:, :, 0][None],
            kv_p[:, :, 1][None],
            conv_p[None],
            ssm_p[None],
            jnp.transpose(nk.reshape(nb, N_KV_HEADS, HEAD_DIM, WINDOW), (0, 3, 1, 2))[None],
            jnp.transpose(nv.reshape(nb, N_KV_HEADS, HEAD_DIM, WINDOW), (0, 3, 1, 2))[None],
            jnp.transpose(nconv_t, (1, 0, 2))[None],
            nssm[None])
```

```python
import functools
import math

import numpy as np
import jax
import jax.numpy as jnp
from jax import lax
from jax.experimental import pallas as pl
from jax.experimental.pallas import tpu as pltpu

F32 = jnp.float32
BF16 = jnp.bfloat16

D_MODEL = 1024
HEAD_DIM = 64
N_HEADS = 8
N_KV_HEADS = 2
KV_REP = 4
WINDOW = 128
ATTN_WIDTH = 512
ATTN_SCALE = HEAD_DIM ** -0.5
N_BUCKETS = 32
MAX_DISTANCE = 128
SSM_WIDTH = 512
SSM_HEADS = 8
SSM_GROUPS = 2
SSM_REP = 4
SSM_HEAD_DIM = 64
D_STATE = 128
CONV_W = 4
CONV_DIM = 1024
CHUNK = 128
D_FF = 2816
EPS = 1e-6

COL_Q, COL_Z, COL_XBC, COL_K, COL_V, COL_DT = 0, 512, 1024, 2048, 2176, 2304
PROJ_COLS = 2432
DT_PAD = 128
KV_COLS = 2 * N_KV_HEADS * HEAD_DIM

FF_CHUNK = 256
N_FF_CHUNKS = D_FF // FF_CHUNK

MIX_TILE = 512
MIX_BLOCKS = MIX_TILE // CHUNK
PROJ_PIECES = ((0, 512), (512, 1024), (1024, 1536), (1536, 2048), (2048, PROJ_COLS))

TAIL_TILE = 512

VMEM_LIMIT = 56 * 1024 * 1024
LAYER_VMEM_LIMIT = 60 * 1024 * 1024

_NT = (((1,), (1,)), ((), ()))


def _bucket_table(dist):
    n = np.maximum(dist, 0)
    exact = N_BUCKETS // 2
    nf = np.maximum(n, 1).astype(np.float32)
    large = exact + (np.log(nf / exact) / math.log(MAX_DISTANCE / exact) * (N_BUCKETS - exact)).astype(np.int32)
    return np.where(n < exact, n, np.minimum(large, N_BUCKETS - 1)).astype(np.int32)


def _prompt_buckets():
    i = np.arange(WINDOW)[:, None]
    j = np.arange(WINDOW)[None, :]
    dist = np.where(j > i, i + WINDOW - j, i - j)
    return _bucket_table(dist)


def _sample_buckets():
    dist = WINDOW - np.arange(WINDOW)[None, :]
    band = (dist >= 0) & (dist < WINDOW)
    row = np.where(band, _bucket_table(dist), -1).astype(np.int32)
    return np.tile(row, (N_HEADS, 1))


def _rms(x, w):
    return x * lax.rsqrt(jnp.mean(x * x, axis=-1, keepdims=True) + EPS) * w


def _silu(x):
    u = 0.5 * x
    return u + u * jnp.tanh(u)


def _softplus(x):
    return jnp.maximum(x, 0.0) + jnp.log1p(jnp.exp(-jnp.abs(x)))


def _const_spec(shape):
    nd = len(shape)
    return pl.BlockSpec(shape, lambda *_: (0,) * nd, pipeline_mode=pl.Buffered(1))


def _bias_kernel(rb_ref, bp_ref, bs_ref, tp_ref, ts_ref, tn_ref):
    bp = bp_ref[...]
    bs = bs_ref[...]
    rowid = lax.broadcasted_iota(jnp.int32, (N_HEADS, WINDOW), 0)
    ts = jnp.zeros((N_HEADS, WINDOW), F32)
    tn = jnp.zeros((N_HEADS, WINDOW), F32)
    for h in range(N_HEADS):
        tp = jnp.zeros((WINDOW, WINDOW), F32)
        for bk in range(N_BUCKETS):
            v = rb_ref[bk, h]
            tp = jnp.where(bp == bk, v, tp)
            ts = jnp.where((bs == bk) & (rowid == h), v, ts)
        tp_ref[h] = tp
        tn = jnp.where(rowid == h, rb_ref[0, h], tn)
    ts_ref[...] = jnp.where(bs < 0, -jnp.inf, ts)
    tn_ref[...] = tn


def _bias_tables(rel_bias):
    return pl.pallas_call(
        _bias_kernel,
        out_shape=(jax.ShapeDtypeStruct((N_HEADS, WINDOW, WINDOW), F32),
                   jax.ShapeDtypeStruct((N_HEADS, WINDOW), F32),
                   jax.ShapeDtypeStruct((N_HEADS, WINDOW), F32)),
        in_specs=[pl.BlockSpec(memory_space=pltpu.SMEM),
                  pl.BlockSpec(memory_space=pltpu.VMEM),
                  pl.BlockSpec(memory_space=pltpu.VMEM)],
        name="bias_tables",
    )(rel_bias, jnp.asarray(_prompt_buckets()), jnp.asarray(_sample_buckets()))


def _inproj_kernel(x_ref, nw_ref, w_ref, o_ref):
    h = _rms(x_ref[...], nw_ref[...]).astype(BF16)
    o_ref[...] = jnp.dot(h, w_ref[...], preferred_element_type=F32)


def _inproj(x2d, norm_w, w_perm, tm):
    n = x2d.shape[0]
    return pl.pallas_call(
        _inproj_kernel,
        out_shape=jax.ShapeDtypeStruct((n, PROJ_COLS), F32),
        grid=(n // tm,),
        in_specs=[pl.BlockSpec((tm, D_MODEL), lambda i: (i, 0)),
                  _const_spec((1, D_MODEL)),
                  _const_spec((D_MODEL, PROJ_COLS))],
        out_specs=pl.BlockSpec((tm, PROJ_COLS), lambda i: (i, 0)),
        compiler_params=pltpu.CompilerParams(dimension_semantics=("arbitrary",),
                                             vmem_limit_bytes=VMEM_LIMIT),
        name="inproj",
    )(x2d, norm_w, w_perm)


def _head_variants(x, low):
    xr = pltpu.roll(x, HEAD_DIM, axis=1)
    zero = jnp.zeros_like(x)
    return ((jnp.where(low, x, zero).astype(BF16), jnp.where(low, zero, xr).astype(BF16)),
            (jnp.where(low, xr, zero).astype(BF16), jnp.where(low, zero, x).astype(BF16)))


def _gated_group_norm(y, z, nw):
    y = y * _silu(z)
    half = SSM_WIDTH // SSM_GROUPS
    parts = []
    for g in range(SSM_GROUPS):
        yg = y[:, g * half:(g + 1) * half]
        parts.append(yg * lax.rsqrt(jnp.mean(yg * yg, axis=-1, keepdims=True) + EPS))
    return jnp.concatenate(parts, axis=-1) * nw


def _attn_scores(pr, r0, kst_ref, vst_ref):
    w = WINDOW
    rows = slice(r0, r0 + w)
    low = lax.broadcasted_iota(jnp.int32, (w, w), 1) < HEAD_DIM

    for st, c0 in ((kst_ref, COL_K), (vst_ref, COL_V)):
        for g in range(N_KV_HEADS):
            st[g, 0:w, :] = st[g, w:2 * w, :]
            st[g, 2 * w:3 * w, :] = st[g, 3 * w:4 * w, :]
        var = _head_variants(pr[rows, c0:c0 + w], low)
        for g in range(N_KV_HEADS):
            st[g, w:2 * w, :] = var[g][0]
            st[g, 3 * w:4 * w, :] = var[g][1]

    scores = []
    for pp in range(N_HEADS // 2):
        q2 = pr[rows, COL_Q + pp * w:COL_Q + (pp + 1) * w].astype(BF16)
        scores.append(lax.dot_general(q2, kst_ref[pp // (KV_REP // 2)], _NT, preferred_element_type=F32))
    return scores


def _attn_softmax(scores, seq_start, sink_ref, tab_ref):
    w = WINDOW
    row = lax.broadcasted_iota(jnp.int32, (w, w), 0)
    lane = lax.broadcasted_iota(jnp.int32, (w, w), 1)
    upper = lane > row
    if seq_start is not None:
        no_prev = (lane - row) > jnp.where(seq_start, 0, w)
    p4s = []
    for pp in range(N_HEADS // 2):
        s4 = scores[pp]
        probs = []
        for half in range(2):
            hh = 2 * pp + half
            sp = s4[:, 2 * w * half:2 * w * half + w]
            sc = s4[:, 2 * w * half + w:2 * w * (half + 1)]
            s = jnp.where(upper, sp, sc) * ATTN_SCALE + tab_ref[hh]
            if seq_start is not None:
                s = jnp.where(no_prev, -jnp.inf, s)
            sk = sink_ref[hh]
            m = jnp.maximum(jnp.max(s, axis=-1, keepdims=True), sk)
            e = jnp.exp(s - m)
            den = jnp.sum(e, axis=-1, keepdims=True) + jnp.exp(sk - m)
            p = e / den
            zero = jnp.zeros_like(p)
            probs.append(jnp.where(upper, p, zero).astype(BF16))
            probs.append(jnp.where(upper, zero, p).astype(BF16))
        p4s.append(jnp.concatenate(probs, axis=1))
    return p4s


def _attn_values(p4s, r0, vst_ref, out_ref):
    w = WINDOW
    for pp in range(N_HEADS // 2):
        out_ref[r0:r0 + w, pp * w:(pp + 1) * w] = jnp.dot(p4s[pp], vst_ref[pp // (KV_REP // 2)],
                                                          preferred_element_type=F32).astype(out_ref.dtype)


def _ssd_decay(pr, r0, dtb_ref, alog_ref):
    rows = slice(r0, r0 + CHUNK)
    lane = lax.broadcasted_iota(jnp.int32, (SSM_HEADS, CHUNK), 1)
    dt_t = _softplus(pr[rows, COL_DT:COL_DT + DT_PAD].T[0:SSM_HEADS, :] + dtb_ref[...])
    a_t = dt_t * (-jnp.exp(alog_ref[...]))
    cs_t = a_t
    shift = 1
    while shift < CHUNK:
        cs_t = cs_t + jnp.where(lane >= shift, pltpu.roll(cs_t, shift, axis=1), 0.0)
        shift *= 2
    cs_end = cs_t[:, CHUNK - 1:CHUNK]
    w_t = jnp.exp(cs_end - cs_t) * dt_t
    dec_end = jnp.exp(cs_end)
    col = jnp.concatenate([cs_t, jnp.exp(cs_t), jnp.zeros((CHUNK - 2 * SSM_HEADS, CHUNK), F32)], axis=0).T
    return dt_t, cs_t, w_t, dec_end, col


def _ssd_conv(pr, r0, seq_start, cw_ref, cb_ref, hist_ref):
    x = pr[r0:r0 + CHUNK, COL_XBC:COL_XBC + CONV_DIM]
    hist = hist_ref[...]
    if seq_start is not None:
        hist = jnp.where(seq_start, 0.0, hist)
    xx = jnp.concatenate([hist, x], axis=0)
    tot = pltpu.roll(xx, CONV_W - 1, axis=0)[8:, :] * cw_ref[0:1, :]
    for j in range(1, CONV_W - 1):
        tot = tot + pltpu.roll(xx, CONV_W - 1 - j, axis=0)[8:, :] * cw_ref[j:j + 1, :]
    tot = tot + x * cw_ref[CONV_W - 1:CONV_W, :]
    hist_ref[...] = x[CHUNK - 8:, :]
    return _silu(cb_ref[...] + tot)


def _ssd_matmuls(pr, r0, seq_start, decay_terms, xc, dsk_ref, nw_ref, st_ref, out_ref):
    hd = SSM_HEAD_DIM
    rows = slice(r0, r0 + CHUNK)
    dt_t, cs_t, w_t, dec_end, col = decay_terms
    row = lax.broadcasted_iota(jnp.int32, (CHUNK, CHUNK), 0)
    lane = lax.broadcasted_iota(jnp.int32, (CHUNK, CHUNK), 1)
    causal = row >= lane
    xs = xc[:, :SSM_WIDTH]
    bm = xc[:, SSM_WIDTH:SSM_WIDTH + SSM_GROUPS * D_STATE]
    cm = xc[:, SSM_WIDTH + SSM_GROUPS * D_STATE:]

    groups = range(SSM_GROUPS)
    bgs = [bm[:, g * D_STATE:(g + 1) * D_STATE].astype(BF16) for g in groups]
    cgs = [cm[:, g * D_STATE:(g + 1) * D_STATE].astype(BF16) for g in groups]
    h_prevs = []
    for g in groups:
        hp = st_ref[g * SSM_REP:(g + 1) * SSM_REP].reshape(SSM_REP * hd, D_STATE)
        if seq_start is not None:
            hp = jnp.where(seq_start, 0.0, hp)
        h_prevs.append(hp)
    gl = lax.broadcasted_iota(jnp.int32, bm.shape, 1) // D_STATE
    zero_b = jnp.zeros_like(bm)
    b_diag = jnp.concatenate([jnp.where(gl == g, bm, zero_b) for g in groups], axis=0).astype(BF16)
    cb_all = lax.dot_general(cm.astype(BF16), b_diag, _NT, preferred_element_type=F32)
    cbs = [cb_all[:, g * CHUNK:(g + 1) * CHUNK] for g in groups]
    y_offs = [lax.dot_general(cgs[g], h_prevs[g].astype(BF16), _NT, preferred_element_type=F32) for g in groups]

    for g in groups:
        xt, wts, decs = [], [], []
        for pr2 in range(SSM_REP // 2):
            h0 = g * SSM_REP + 2 * pr2
            xt.append(xs[:, h0 * hd:(h0 + 2) * hd].T)
            for h in (h0, h0 + 1):
                wts.append(jnp.broadcast_to(w_t[h:h + 1, :], (hd, CHUNK)))
                decs.append(jnp.broadcast_to(dec_end[h:h + 1, :], (hd, D_STATE)))
        xw = (jnp.concatenate(xt, axis=0) * jnp.concatenate(wts, axis=0)).astype(BF16)
        st = jnp.dot(xw, bgs[g], preferred_element_type=F32)
        h_new = h_prevs[g] * jnp.concatenate(decs, axis=0) + st
        st_ref[g * SSM_REP:(g + 1) * SSM_REP] = h_new.reshape(SSM_REP, hd, D_STATE)

    gw = SSM_REP * hd
    seg = lax.broadcasted_iota(jnp.int32, (CHUNK, gw), 1) // hd
    ys = []
    for g in groups:
        xg = xs[:, g * gw:(g + 1) * gw]
        zero = jnp.zeros_like(xg)
        ms, xstack, scale = [], [], None
        for r in range(SSM_REP):
            h = g * SSM_REP + r
            decay = jnp.exp(jnp.where(causal, col[:, h:h + 1] - cs_t[h:h + 1, :], -jnp.inf))
            ms.append((cbs[g] * decay * dt_t[h:h + 1, :]).astype(BF16))
            xstack.append(jnp.where(seg == r, xg, zero).astype(BF16))
            e_h = jnp.broadcast_to(col[:, SSM_HEADS + h:SSM_HEADS + h + 1], (CHUNK, gw))
            scale = e_h if scale is None else jnp.where(seg == r, e_h, scale)
        y_diag = jnp.dot(jnp.concatenate(ms, axis=1), jnp.concatenate(xstack, axis=0),
                         preferred_element_type=F32)
        ys.append(y_diag + y_offs[g] * scale)
    y = jnp.concatenate(ys, axis=-1) + xs * dsk_ref[...]
    out_ref[rows, ATTN_WIDTH:ATTN_WIDTH + SSM_WIDTH] = _gated_group_norm(
        y, pr[rows, COL_Z:COL_Z + SSM_WIDTH], nw_ref[...]).astype(out_ref.dtype)


def _prompt_layer_kernel(tiles_per_seq, n_tiles, sink_ref, x_ref, xres_ref, n1_ref, w_ref, tab_ref, cw_ref,
                         cb_ref, dtb_ref, alog_ref, dsk_ref, nw_ref, wo_ref, n2_ref, wg_ref, wu_ref, wd_ref,
                         fn_ref, y_ref, kv_ref, conv_ref, ssm_ref,
                         proj_ref, mix_ref, kst_ref, vst_ref, hist_ref, st_ref):
    t = pl.program_id(0)

    @pl.when(t == 0)
    def _():
        kst_ref[...] = jnp.zeros_like(kst_ref)
        vst_ref[...] = jnp.zeros_like(vst_ref)
        hist_ref[...] = jnp.zeros_like(hist_ref)
        st_ref[...] = jnp.zeros_like(st_ref)

    slot = lax.rem(t, 2)
    pw = proj_ref.at[slot]
    pr = proj_ref.at[1 - slot]
    mw = mix_ref.at[1 - slot]
    mr = mix_ref.at[slot]
    seq_start = lax.rem(t + tiles_per_seq - 1, tiles_per_seq) == 0

    def normed_input():
        return _rms(x_ref[...], n1_ref[...]).astype(BF16)

    def project(h, piece):
        lo, hi = PROJ_PIECES[piece]
        pw[:, lo:hi] = jnp.dot(h, w_ref[:, lo:hi], preferred_element_type=F32)

    def ffn_input():
        x1 = xres_ref[...] + jnp.dot(mr[...], wo_ref[...], preferred_element_type=F32)
        return x1, _rms(x1, n2_ref[...]).astype(BF16)

    def ffn(h2, acc, chunk):
        cols = slice(chunk * FF_CHUNK, (chunk + 1) * FF_CHUNK)
        gate = jnp.dot(h2, wg_ref[:, cols], preferred_element_type=F32)
        up = jnp.dot(h2, wu_ref[:, cols], preferred_element_type=F32)
        act = (_silu(gate) * up).astype(BF16)
        part = jnp.dot(act, wd_ref[cols, :], preferred_element_type=F32)
        acc[0] = part if acc[0] is None else acc[0] + part

    def mix_block(j, between):
        r0 = j * CHUNK
        start = seq_start if j == 0 else None
        decay_terms = _ssd_decay(pr, r0, dtb_ref, alog_ref)
        scores = _attn_scores(pr, r0, kst_ref, vst_ref)
        between[0]()
        xc = _ssd_conv(pr, r0, start, cw_ref, cb_ref, hist_ref)
        between[1]()
        p4s = _attn_softmax(scores, start, sink_ref, tab_ref)
        _attn_values(p4s, r0, vst_ref, mw)
        between[2]()
        _ssd_matmuls(pr, r0, start, decay_terms, xc, dsk_ref, nw_ref, st_ref, mw)
        between[3]()

    def sequence_outputs():
        kv_ref[0] = pr[MIX_TILE - WINDOW:, COL_K:COL_K + KV_COLS]
        conv_ref[0] = pr[MIX_TILE - (CONV_W - 1):, COL_XBC:COL_XBC + CONV_DIM]
        ssm_ref[0] = st_ref[...]

    steady = t >= 2

    @pl.when(steady)
    def _():
        h = normed_input()
        x1, h2 = ffn_input()
        acc = [None]

        def last_ffn():
            ffn(h2, acc, N_FF_CHUNKS - 1)
            y_ref[...] = _rms(x1 + acc[0], fn_ref[...])

        pieces = ([functools.partial(project, h, p) for p in range(len(PROJ_PIECES))]
                  + [functools.partial(ffn, h2, acc, c) for c in range(N_FF_CHUNKS - 1)] + [last_ffn])
        order = [0, 5, 6, 7, 8, 9, 10, 11, 12, 13, 14, 15, 1, 2, 3, 4]
        assert sorted(order) == list(range(4 * MIX_BLOCKS))
        for j in range(MIX_BLOCKS):
            mix_block(j, [pieces[i] for i in order[4 * j:4 * j + 4]])

    @pl.when(steady & (t <= n_tiles) & (lax.rem(t, tiles_per_seq) == 0))
    def _():
        sequence_outputs()

    @pl.when(jnp.logical_not(steady))
    def _():
        h = normed_input()
        for p in range(len(PROJ_PIECES)):
            project(h, p)

    @pl.when(t == 1)
    def _():
        for j in range(MIX_BLOCKS):
            mix_block(j, [lambda: None] * 4)
        sequence_outputs()


def _prompt_layer(x2d, n1, w_perm, sinks, tab, conv_w, conv_b, dtb_t, alog_t, dsk, nw, wo, n2, wg, wu, wd, fn,
                  batch, seq):
    assert seq % MIX_TILE == 0
    tiles_per_seq = seq // MIX_TILE
    nt = batch * tiles_per_seq

    def mixed_tile(t):
        return jnp.clip(t - 1, 0, nt - 1)

    def seq_of(t):
        return mixed_tile(t) // tiles_per_seq

    def ffn_tile(t):
        return jnp.maximum(t - 2, 0)

    return pl.pallas_call(
        functools.partial(_prompt_layer_kernel, tiles_per_seq, nt),
        out_shape=(jax.ShapeDtypeStruct((batch * seq, D_MODEL), F32),
                   jax.ShapeDtypeStruct((batch, WINDOW, KV_COLS), F32),
                   jax.ShapeDtypeStruct((batch, CONV_W - 1, CONV_DIM), F32),
                   jax.ShapeDtypeStruct((batch, SSM_HEADS, SSM_HEAD_DIM, D_STATE), F32)),
        grid=(nt + 2,),
        in_specs=[pl.BlockSpec(memory_space=pltpu.SMEM),
                  pl.BlockSpec((MIX_TILE, D_MODEL), lambda t: (jnp.minimum(t, nt - 1), 0)),
                  pl.BlockSpec((MIX_TILE, D_MODEL), lambda t: (ffn_tile(t), 0)),
                  _const_spec((1, D_MODEL)),
                  _const_spec((D_MODEL, PROJ_COLS)),
                  _const_spec((N_HEADS, WINDOW, WINDOW)),
                  _const_spec((CONV_W, CONV_DIM)),
                  _const_spec((1, CONV_DIM)),
                  _const_spec((SSM_HEADS, CHUNK)),
                  _const_spec((SSM_HEADS, CHUNK)),
                  _const_spec((1, SSM_WIDTH)),
                  _const_spec((1, SSM_WIDTH)),
                  _const_spec((D_MODEL, D_MODEL)),
                  _const_spec((1, D_MODEL)),
                  _const_spec((D_MODEL, D_FF)),
                  _const_spec((D_MODEL, D_FF)),
                  _const_spec((D_FF, D_MODEL)),
                  _const_spec((1, D_MODEL))],
        out_specs=(pl.BlockSpec((MIX_TILE, D_MODEL), lambda t: (ffn_tile(t), 0)),
                   pl.BlockSpec((1, WINDOW, KV_COLS), lambda t: (seq_of(t), 0, 0)),
                   pl.BlockSpec((1, CONV_W - 1, CONV_DIM), lambda t: (seq_of(t), 0, 0)),
                   pl.BlockSpec((1, SSM_HEADS, SSM_HEAD_DIM, D_STATE), lambda t: (seq_of(t), 0, 0, 0))),
        scratch_shapes=[pltpu.VMEM((2, MIX_TILE, PROJ_COLS), F32),
                        pltpu.VMEM((2, MIX_TILE, ATTN_WIDTH + SSM_WIDTH), BF16),
                        pltpu.VMEM((N_KV_HEADS, 4 * WINDOW, WINDOW), BF16),
                        pltpu.VMEM((N_KV_HEADS, 4 * WINDOW, WINDOW), BF16),
                        pltpu.VMEM((8, CONV_DIM), F32),
                        pltpu.VMEM((SSM_HEADS, SSM_HEAD_DIM, D_STATE), F32)],
        compiler_params=pltpu.CompilerParams(dimension_semantics=("arbitrary",),
                                             vmem_limit_bytes=LAYER_VMEM_LIMIT),
        name="prompt_layer",
    )(sinks, x2d, x2d, n1, w_perm, tab, conv_w, conv_b, dtb_t, alog_t, dsk, nw, wo, n2, wg, wu, wd, fn)


def _tail_kernel(x_ref, mix_ref, wo_ref, n2_ref, wg_ref, wu_ref, wd_ref, fn_ref, o_ref):
    x1 = x_ref[...] + jnp.dot(mix_ref[...].astype(BF16), wo_ref[...], preferred_element_type=F32)
    h2 = _rms(x1, n2_ref[...]).astype(BF16)
    acc = None
    for j in range(N_FF_CHUNKS):
        cols = slice(j * FF_CHUNK, (j + 1) * FF_CHUNK)
        gate = jnp.dot(h2, wg_ref[:, cols], preferred_element_type=F32)
        up = jnp.dot(h2, wu_ref[:, cols], preferred_element_type=F32)
        act = (_silu(gate) * up).astype(BF16)
        part = jnp.dot(act, wd_ref[cols, :], preferred_element_type=F32)
        acc = part if acc is None else acc + part
    o_ref[...] = _rms(x1 + acc, fn_ref[...])


def _tail(x2d, mix, wo, n2, wg, wu, wd, fn, tm):
    n = x2d.shape[0]
    return pl.pallas_call(
        _tail_kernel,
        out_shape=jax.ShapeDtypeStruct((n, D_MODEL), F32),
        grid=(n // tm,),
        in_specs=[pl.BlockSpec((tm, D_MODEL), lambda i: (i, 0)),
                  pl.BlockSpec((tm, ATTN_WIDTH + SSM_WIDTH), lambda i: (i, 0)),
                  _const_spec((D_MODEL, D_MODEL)),
                  _const_spec((1, D_MODEL)),
                  _const_spec((D_MODEL, D_FF)),
                  _const_spec((D_MODEL, D_FF)),
                  _const_spec((D_FF, D_MODEL)),
                  _const_spec((1, D_MODEL))],
        out_specs=pl.BlockSpec((tm, D_MODEL), lambda i: (i, 0)),
        compiler_params=pltpu.CompilerParams(dimension_semantics=("arbitrary",),
                                             vmem_limit_bytes=VMEM_LIMIT),
        name="outproj_ffn",
    )(x2d, mix, wo, n2, wg, wu, wd, fn)


SAMPLE_BT = 16
ATTN_SAMPLE_BT = 16


def _attn_sample_kernel(q_ref, kn_ref, vn_ref, ck_ref, cv_ref, ts_ref, tn_ref, sink_ref,
                        wo_ref, wg_ref, wu_ref, wd_ref,
                        a_ref, nk_ref, nv_ref, wo16_ref, wg16_ref, wu16_ref, wd16_ref):
    for src, dst in ((wo_ref, wo16_ref), (wg_ref, wg16_ref), (wu_ref, wu16_ref), (wd_ref, wd16_ref)):
        dst[...] = src[...].astype(BF16)

    rowid = lax.broadcasted_iota(jnp.int32, (N_HEADS, HEAD_DIM), 0)
    ts = ts_ref[...]
    tn = tn_ref[:, 0:1]
    sink = sink_ref[:, 0:1]
    pad = jnp.zeros((WINDOW - ATTN_SAMPLE_BT, WINDOW), F32)
    kn_t = jnp.concatenate([kn_ref[...], pad], axis=0).T
    vn_t = jnp.concatenate([vn_ref[...], pad], axis=0).T
    newest = lax.broadcasted_iota(jnp.int32, (WINDOW, WINDOW), 1) == WINDOW - 1

    def shifted(cache_ref, new_t, bb):
        return jnp.where(newest, new_t[:, bb:bb + 1], pltpu.roll(cache_ref[bb], WINDOW - 1, axis=1))

    qbds, scores = [], []
    for bb in range(ATTN_SAMPLE_BT):
        qb = q_ref[bb]
        qbd = jnp.concatenate([jnp.where(rowid < KV_REP, qb, 0.0), jnp.where(rowid >= KV_REP, qb, 0.0)], axis=1)
        qbds.append(qbd)
        scores.append(jnp.dot(qbd.astype(BF16), ck_ref[bb].astype(BF16), preferred_element_type=F32))
    probs, new_terms = [], []
    for bb in range(ATTN_SAMPLE_BT):
        s_c = scores[bb] * ATTN_SCALE + ts
        s_n = jnp.sum(qbds[bb] * kn_ref[bb:bb + 1, :], axis=-1, keepdims=True) * ATTN_SCALE + tn
        m = jnp.maximum(jnp.maximum(jnp.max(s_c, axis=-1, keepdims=True), s_n), sink)
        e_c = jnp.exp(s_c - m)
        e_n = jnp.exp(s_n - m)
        den = jnp.sum(e_c, axis=-1, keepdims=True) + e_n + jnp.exp(sink - m)
        probs.append((e_c / den).astype(BF16))
        new_terms.append((e_n / den) * vn_ref[bb:bb + 1, :])
    for bb in range(ATTN_SAMPLE_BT):
        o = lax.dot_general(probs[bb], cv_ref[bb].astype(BF16), _NT, preferred_element_type=F32) + new_terms[bb]
        a_ref[bb] = jnp.where(rowid < KV_REP, o[:, :HEAD_DIM], o[:, HEAD_DIM:])
    for bb in range(ATTN_SAMPLE_BT):
        nk_ref[bb] = shifted(ck_ref, kn_t, bb)
        nv_ref[bb] = shifted(cv_ref, vn_t, bb)


def _attn_sample(q3, proj, ck, cv, ts, tn, sink_b, weights):
    nb = q3.shape[0]
    bt = ATTN_SAMPLE_BT
    steps = nb // bt
    kv = N_KV_HEADS * HEAD_DIM

    def slab(w):
        rows, cols = w.shape
        assert rows % (steps * 16) == 0
        return pl.BlockSpec((rows // steps, cols), lambda i: (i, 0))

    return pl.pallas_call(
        _attn_sample_kernel,
        out_shape=(jax.ShapeDtypeStruct((nb, N_HEADS, HEAD_DIM), F32),
                   jax.ShapeDtypeStruct((nb, WINDOW, kv), F32),
                   jax.ShapeDtypeStruct((nb, WINDOW, kv), F32))
        + tuple(jax.ShapeDtypeStruct(w.shape, BF16) for w in weights),
        grid=(steps,),
        in_specs=[pl.BlockSpec((bt, N_HEADS, HEAD_DIM), lambda i: (i, 0, 0)),
                  pl.BlockSpec((bt, kv), lambda i: (i, COL_K // kv)),
                  pl.BlockSpec((bt, kv), lambda i: (i, COL_V // kv)),
                  pl.BlockSpec((bt, WINDOW, kv), lambda i: (i, 0, 0)),
                  pl.BlockSpec((bt, WINDOW, kv), lambda i: (i, 0, 0)),
                  _const_spec((N_HEADS, WINDOW)),
                  _const_spec((N_HEADS, WINDOW)),
                  _const_spec((N_HEADS, WINDOW))] + [slab(w) for w in weights],
        out_specs=(pl.BlockSpec((bt, N_HEADS, HEAD_DIM), lambda i: (i, 0, 0)),
                   pl.BlockSpec((bt, WINDOW, kv), lambda i: (i, 0, 0)),
                   pl.BlockSpec((bt, WINDOW, kv), lambda i: (i, 0, 0))) + tuple(slab(w) for w in weights),
        compiler_params=pltpu.CompilerParams(dimension_semantics=("arbitrary",),
                                             vmem_limit_bytes=VMEM_LIMIT),
        name="attn_sample",
    )(q3, proj, proj, ck, cv, ts, tn, sink_b, *weights)


def _ssm_sample_prep_kernel(xbc_ref, dt_ref, cst_ref, cw_ref, cb_ref, dtb_ref, alog_ref,
                            xs_ref, bm_ref, cm_ref, xdt_t_ref, dec_ref, nc_ref):
    xbc = xbc_ref[...]
    tot = cst_ref[0] * cw_ref[0:1, :]
    tot = tot + cst_ref[1] * cw_ref[1:2, :]
    tot = tot + cst_ref[2] * cw_ref[2:3, :]
    tot = tot + xbc * cw_ref[3:4, :]
    xc = _silu(cb_ref[...] + tot)
    nc_ref[0] = cst_ref[1]
    nc_ref[1] = cst_ref[2]
    nc_ref[2] = xbc
    xs = xc[:, :SSM_WIDTH]
    xs_ref[...] = xs
    bm_ref[...] = xc[:, SSM_WIDTH:SSM_WIDTH + SSM_GROUPS * D_STATE]
    cm_ref[...] = xc[:, SSM_WIDTH + SSM_GROUPS * D_STATE:]
    dt = _softplus(dt_ref[...] + dtb_ref[...])
    dec_ref[...] = jnp.exp(dt * (-jnp.exp(alog_ref[...])))
    dt_t = dt.T
    for k in range(SSM_WIDTH // 128):
        xt = xs[:, k * 128:(k + 1) * 128].T
        for half in range(2):
            h = 2 * k + half
            lo = h * SSM_HEAD_DIM
            xdt_t_ref[lo:lo + SSM_HEAD_DIM, :] = xt[half * SSM_HEAD_DIM:(half + 1) * SSM_HEAD_DIM, :] * dt_t[h:h + 1, :]


def _ssm_sample_prep(proj, conv_state_t, conv_w, conv_b, dtb, alog):
    nb = proj.shape[0]
    return pl.pallas_call(
        _ssm_sample_prep_kernel,
        out_shape=(jax.ShapeDtypeStruct((nb, SSM_WIDTH), F32),
                   jax.ShapeDtypeStruct((nb, SSM_GROUPS * D_STATE), F32),
                   jax.ShapeDtypeStruct((nb, SSM_GROUPS * D_STATE), F32),
                   jax.ShapeDtypeStruct((SSM_WIDTH, nb), F32),
                   jax.ShapeDtypeStruct((nb, DT_PAD), F32),
                   jax.ShapeDtypeStruct((CONV_W - 1, nb, CONV_DIM), F32)),
        grid=(1,),
        in_specs=[pl.BlockSpec((nb, CONV_DIM), lambda i: (0, COL_XBC // CONV_DIM)),
                  pl.BlockSpec((nb, DT_PAD), lambda i: (0, COL_DT // DT_PAD)),
                  _const_spec((CONV_W - 1, nb, CONV_DIM)),
                  _const_spec((CONV_W, CONV_DIM)),
                  _const_spec((1, CONV_DIM)),
                  _const_spec((1, DT_PAD)),
                  _const_spec((1, DT_PAD))],
        out_specs=(_const_spec((nb, SSM_WIDTH)),
                   _const_spec((nb, SSM_GROUPS * D_STATE)),
                   _const_spec((nb, SSM_GROUPS * D_STATE)),
                   _const_spec((SSM_WIDTH, nb)),
                   _const_spec((nb, DT_PAD)),
                   _const_spec((CONV_W - 1, nb, CONV_DIM))),
        compiler_params=pltpu.CompilerParams(dimension_semantics=("arbitrary",),
                                             vmem_limit_bytes=VMEM_LIMIT),
        name="ssm_sample_prep",
    )(proj, proj, conv_state_t, conv_w, conv_b, dtb, alog)


def _ssm_sample_state_kernel(dec_ref, xdt_t_ref, bm_ref, cm_ref, xs_ref, z_ref, dsk_ref, nw_ref, st_ref,
                             nst_ref, s_ref, yt_ref):
    i = pl.program_id(0)
    last = pl.num_programs(0) - 1
    rows = SSM_REP * SSM_HEAD_DIM

    @pl.when(i == 0)
    def _():
        yt_ref[...] = jnp.zeros_like(yt_ref)

    lane = lax.broadcasted_iota(jnp.int32, (rows, 128), 1)
    for bb in range(SAMPLE_BT):
        b = i * SAMPLE_BT + bb
        mine = lane == b
        for g in range(SSM_GROUPS):
            bg = bm_ref[:, g * D_STATE:(g + 1) * D_STATE].astype(BF16)
            cg = cm_ref[:, g * D_STATE:(g + 1) * D_STATE].astype(BF16)
            xsel = jnp.where(mine, xdt_t_ref[g * rows:(g + 1) * rows, :], 0.0).astype(BF16)
            outer = jnp.dot(xsel, bg, preferred_element_type=F32)
            new = []
            for r in range(SSM_REP):
                h = g * SSM_REP + r
                hn = st_ref[bb, h] * dec_ref[b * SSM_HEADS + h] + outer[r * SSM_HEAD_DIM:(r + 1) * SSM_HEAD_DIM, :]
                nst_ref[bb, h] = hn
                new.append(hn)
            hcat = jnp.concatenate(new, axis=0).astype(BF16)
            res = lax.dot_general(hcat, cg, _NT, preferred_element_type=F32)
            yt_ref[g * rows:(g + 1) * rows, :] += jnp.where(mine, res, 0.0)

    @pl.when(i == last)
    def _():
        y = jnp.concatenate([yt_ref[k * 128:(k + 1) * 128, :].T for k in range(SSM_WIDTH // 128)], axis=1)
        y = y + xs_ref[...] * dsk_ref[...]
        s_ref[...] = _gated_group_norm(y, z_ref[...], nw_ref[...])


def _ssm_sample_state(dec_flat, xdt_t, bm, cm, xs, proj, dsk, nw, state):
    nb = state.shape[0]
    bt = SAMPLE_BT
    return pl.pallas_call(
        _ssm_sample_state_kernel,
        out_shape=(jax.ShapeDtypeStruct(state.shape, F32),
                   jax.ShapeDtypeStruct((nb, SSM_WIDTH), F32)),
        grid=(nb // bt,),
        in_specs=[pl.BlockSpec(memory_space=pltpu.SMEM),
                  _const_spec((SSM_WIDTH, nb)),
                  _const_spec((nb, SSM_GROUPS * D_STATE)),
                  _const_spec((nb, SSM_GROUPS * D_STATE)),
                  _const_spec((nb, SSM_WIDTH)),
                  pl.BlockSpec((nb, SSM_WIDTH), lambda i: (0, COL_Z // SSM_WIDTH)),
                  _const_spec((1, SSM_WIDTH)),
                  _const_spec((1, SSM_WIDTH)),
                  pl.BlockSpec((bt, SSM_HEADS, SSM_HEAD_DIM, D_STATE), lambda i: (i, 0, 0, 0))],
        out_specs=(pl.BlockSpec((bt, SSM_HEADS, SSM_HEAD_DIM, D_STATE), lambda i: (i, 0, 0, 0)),
                   pl.BlockSpec((nb, SSM_WIDTH), lambda i: (0, 0))),
        scratch_shapes=[pltpu.VMEM((SSM_WIDTH, nb), F32)],
        compiler_params=pltpu.CompilerParams(dimension_semantics=("arbitrary",),
                                             vmem_limit_bytes=VMEM_LIMIT),
        name="ssm_sample_state",
    )(dec_flat, xdt_t, bm, cm, xs, proj, dsk, nw, state)


def _pad_lanes(v, width):
    return jnp.pad(v.reshape(1, -1), ((0, 0), (0, width - v.shape[-1])))


def kernel(x_prompt, x_sample, cache_k, cache_v, state_conv, state_ssm, rel_bias, norm1_w, w_in, attn_sinks,
           conv_w, conv_b, dt_bias, A_log, D_skip, ssm_norm_w, w_out, norm2_w, w_gate, w_up, w_down, final_norm_w):
    depth = norm1_w.shape[0]
    assert depth == 1, "single-layer trunk"
    batch, seq, _ = x_prompt.shape
    nb = x_sample.shape[0]
    assert x_sample.shape[1] == 1 and nb == 128

    wi = w_in[0]
    q_c, k_c, v_c, z_c, xbc_c, dt_c = (wi[:, 0:512], wi[:, 512:640], wi[:, 640:768], wi[:, 768:1280],
                                       wi[:, 1280:2304], wi[:, 2304:2312])
    w_perm = jnp.concatenate([q_c, z_c, xbc_c, k_c, v_c, jnp.pad(dt_c, ((0, 0), (0, DT_PAD - SSM_HEADS)))],
                             axis=1).astype(BF16)
    n1 = norm1_w[0].reshape(1, D_MODEL)
    n2 = norm2_w[0].reshape(1, D_MODEL)
    fn = final_norm_w.reshape(1, D_MODEL)
    cw = conv_w[0]
    cb = conv_b[0].reshape(1, CONV_DIM)
    dtb = _pad_lanes(dt_bias[0], DT_PAD)
    alog = _pad_lanes(A_log[0], DT_PAD)
    dtb_t = jnp.broadcast_to(dt_bias[0][:, None], (SSM_HEADS, CHUNK))
    alog_t = jnp.broadcast_to(A_log[0][:, None], (SSM_HEADS, CHUNK))
    dsk = jnp.repeat(D_skip[0], SSM_HEAD_DIM).reshape(1, SSM_WIDTH)
    nw = ssm_norm_w[0].reshape(1, SSM_WIDTH)
    sinks = attn_sinks[0]
    sink_b = jnp.broadcast_to(sinks[:, None], (N_HEADS, WINDOW))

    tab_p, tab_s, tab_n = _bias_tables(rel_bias)

    xs2 = x_sample.reshape(nb, D_MODEL)
    proj_s = _inproj(xs2, n1, w_perm, nb)
    q3 = proj_s[:, COL_Q:COL_Q + ATTN_WIDTH].reshape(nb, N_HEADS, HEAD_DIM)
    ck = jnp.transpose(cache_k[0], (0, 2, 3, 1)).reshape(nb, N_KV_HEADS * HEAD_DIM, WINDOW)
    cv = jnp.transpose(cache_v[0], (0, 2, 3, 1)).reshape(nb, N_KV_HEADS * HEAD_DIM, WINDOW)
    a_s3, nk, nv, wo, wg, wu, wd = _attn_sample(q3, proj_s, ck, cv, tab_s, tab_n, sink_b,
                                                 (w_out[0], w_gate[0], w_up[0], w_down[0]))
    conv_t = jnp.transpose(state_conv[0], (1, 0, 2))
    xs_s, bm_s, cm_s, xdt_t, dec, nconv_t = _ssm_sample_prep(proj_s, conv_t, cw, cb, dtb, alog)
    dec_flat = dec[:, :SSM_HEADS].reshape(nb * SSM_HEADS)
    nssm, s_s = _ssm_sample_state(dec_flat, xdt_t, bm_s, cm_s, xs_s, proj_s, dsk, nw, state_ssm[0])
    mix_s = jnp.concatenate([a_s3.reshape(nb, ATTN_WIDTH), s_s], axis=1)
    y_s = _tail(xs2, mix_s, wo, n2, wg, wu, wd, fn, nb)

    xp2 = x_prompt.reshape(batch * seq, D_MODEL)
    y_p, kv_p, conv_p, ssm_p = _prompt_layer(xp2, n1, w_perm, sinks, tab_p, cw, cb, dtb_t, alog_t, dsk, nw,
                                              wo, n2, wg, wu, wd, fn, batch, seq)
    kv_p = kv_p.reshape(batch, WINDOW, 2, N_KV_HEADS, HEAD_DIM)

    return (y_p.reshape(batch, seq, D_MODEL),
            y_s.reshape(nb, 1, D_MODEL),
            kv_p[:, :, 0][None],
            kv_p[:, :, 1][None],
            conv_p[None],
            ssm_p[None],
            jnp.transpose(nk.reshape(nb, N_KV_HEADS, HEAD_DIM, WINDOW), (0, 3, 1, 2))[None],
            jnp.transpose(nv.reshape(nb, N_KV_HEADS, HEAD_DIM, WINDOW), (0, 3, 1, 2))[None],
            jnp.transpose(nconv_t, (1, 0, 2))[None],
            nssm[None])
```
